```python
import math
import jax
import jax.numpy as jnp
from jax import lax
import numpy as np

D_MODEL = 1024
BATCH = 4
SEQ = 4096
DEPTH = 1
DEC_BATCH = 128
DEC_SEQ = 1
PAST_LEN = 2048
PAGE_SIZE = 128

SSD_EXPAND = 2
D_INNER = SSD_EXPAND * D_MODEL
SSD_HEAD_DIM = 64
SSD_HEADS = D_INNER // SSD_HEAD_DIM
SSD_GROUPS = 4
SSD_HPG = SSD_HEADS // SSD_GROUPS
SSD_STATE = 128
SSD_CONV = 4
SSD_CONV_DIM = D_INNER + 2 * SSD_GROUPS * SSD_STATE
SSD_CHUNK = 128
NSA_HEADS = 16
NSA_KV_HEADS = 4
NSA_GQ = NSA_HEADS // NSA_KV_HEADS
NSA_HEAD_DIM = 64
CMP_LEN = 32
CMP_STRIDE = 16
CMP_RATIO = CMP_LEN // CMP_STRIDE
CMP_HIDDEN = 2 * NSA_HEAD_DIM
SEL_BLOCK = 64
SEL_TOPN = 16
SEL_FORCE = 1.0e4
WINDOW = 512
QUERY_BLOCK = 128
MEM_TOKENS = 256
MEM_HEADS = 4
MEM_HEAD_DIM = 256
D_FF = 2816
FFN_CONV = 3
ROPE_THETA = 10000.0
EPS = 1e-6
N_BRANCH = 3
IN_SIZES = (D_INNER, SSD_CONV_DIM, SSD_HEADS, NSA_HEADS * NSA_HEAD_DIM,
            6 * NSA_KV_HEADS * NSA_HEAD_DIM, 3 * NSA_HEADS, MEM_HEADS * MEM_HEAD_DIM,
            N_BRANCH * D_MODEL)
D_IN_PROJ = sum(IN_SIZES)

kernel_name = 'hybrid_ssd_nsa_memory_convffn_step'


def rmsnorm(x, g):
    xf = x.astype(jnp.float32)
    y = xf * lax.rsqrt(jnp.mean(xf * xf, axis=-1, keepdims=True) + EPS)
    return (y * g.astype(jnp.float32)).astype(x.dtype)


def rope(x, pos):
    half = x.shape[-1] // 2
    inv = ROPE_THETA ** (-jnp.arange(half, dtype=jnp.float32) / half)
    ang = pos.astype(jnp.float32)[:, None] * inv[None, :]
    cos = jnp.cos(ang)[None, :, None, :]
    sin = jnp.sin(ang)[None, :, None, :]
    xf = x.astype(jnp.float32)
    x1, x2 = xf[..., :half], xf[..., half:]
    return jnp.concatenate([x1 * cos - x2 * sin, x2 * cos + x1 * sin], axis=-1).astype(x.dtype)


def masked_softmax(s, mask):
    s = jnp.where(mask, s.astype(jnp.float32), -jnp.inf)
    m = jnp.max(s, axis=-1, keepdims=True)
    m = jnp.where(jnp.isfinite(m), m, 0.0)
    p = jnp.exp(s - m)
    den = jnp.sum(p, axis=-1, keepdims=True)
    return p / jnp.where(den > 0, den, 1.0)


def causal_dwconv(u, prev, w, bias):
    width, length = w.shape[0], u.shape[1]
    up = jnp.concatenate([prev.astype(u.dtype), u], axis=1)
    out = bias + w[0] * up[:, :length]
    for k in range(1, width):
        out = out + w[k] * up[:, k:k + length]
    return out, up[:, length:]


def ssd_scan(xh, dt, a, bm, cm, h0):
    b, length = xh.shape[:2]
    q = min(SSD_CHUNK, length)
    pad = (-length) % q
    if pad:
        padf = lambda t: jnp.pad(t, [(0, 0), (0, pad)] + [(0, 0)] * (t.ndim - 2))
        xh, dt, bm, cm = padf(xh), padf(dt), padf(bm), padf(cm)
    nc = (length + pad) // q
    chunk = lambda t: t.reshape((b, nc, q) + t.shape[2:])
    xh, dt, bm, cm = chunk(xh), chunk(dt), chunk(bm), chunk(cm)
    cum = jnp.cumsum(dt * a, axis=2)
    cum_t = jnp.moveaxis(cum, 2, -1)
    seg = cum_t[..., :, None] - cum_t[..., None, :]
    causal = jnp.tril(jnp.ones((q, q), dtype=bool))
    decay = jnp.exp(jnp.where(causal, seg, -jnp.inf))
    cb = jnp.einsum('bcign,bcjgn->bcgij', cm, bm)
    xdt = xh * dt[..., None]
    y_diag = jnp.einsum('bcgeij,bcjgep->bcigep', cb[:, :, :, None] * decay, xdt)
    decay_end = jnp.exp(cum[:, :, -1:] - cum)
    states = jnp.einsum('bcjgn,bcjgep->bcgepn', bm, xdt * decay_end[..., None])
    chunk_decay = jnp.exp(cum[:, :, -1])

    def step(h, inp):
        dec, st = inp
        return dec[..., None, None] * h + st, h

    h_last, h_prev = lax.scan(step, h0, (jnp.moveaxis(chunk_decay, 1, 0), jnp.moveaxis(states, 1, 0)))
    h_prev = jnp.moveaxis(h_prev, 0, 1)
    y_off = jnp.einsum('bcign,bcgepn->bcigep', cm, h_prev) * jnp.exp(cum)[..., None]
    y = (y_diag + y_off).reshape((b, nc * q) + xh.shape[3:])[:, :length]
    return y, h_last


def compress(rows, pe, w1, w2):
    b, t = rows.shape[:2]
    n_chunk = t // CMP_STRIDE
    n_cmp = n_chunk - CMP_RATIO + 1
    chunks = rows[:, :n_chunk * CMP_STRIDE].reshape(b, n_chunk, CMP_STRIDE, NSA_KV_HEADS, NSA_HEAD_DIM)
    w1r = w1.reshape(CMP_LEN, NSA_HEAD_DIM, CMP_HIDDEN)
    hid = jnp.einsum('ld,ldm->m', pe, w1r)
    for r in range(CMP_RATIO):
        hid = hid + jnp.einsum('bcskd,sdm->bckm', chunks[:, r:r + n_cmp],
                               w1r[r * CMP_STRIDE:(r + 1) * CMP_STRIDE])
    return jax.nn.silu(hid) @ w2


def nsa_attend(q, gates, q_pos0, kc, vc, ks, vs, kw, vw, kw_pos0):
    b, lq = q.shape[:2]
    qb_len = min(QUERY_BLOCK, lq)
    nb = lq // qb_len
    scale = NSA_HEAD_DIM ** -0.5
    nc = kc.shape[1]
    kc_end = jnp.arange(nc, dtype=jnp.int32) * CMP_STRIDE + (CMP_LEN - 1)
    t_s = ks.shape[1]
    ns = -(-t_s // SEL_BLOCK)
    padn = ns * SEL_BLOCK - t_s
    to_blocks = lambda t: jnp.pad(t, ((0, 0), (0, padn), (0, 0), (0, 0))).reshape(
        b, ns, SEL_BLOCK, NSA_KV_HEADS, NSA_HEAD_DIM).transpose(0, 3, 1, 2, 4)
    ks_b, vs_b = to_blocks(ks), to_blocks(vs)
    k_top = min(SEL_TOPN, ns)
    ci = np.arange(nc)[:, None] * CMP_STRIDE
    sj = np.arange(ns)[None, :] * SEL_BLOCK
    overlap = jnp.asarray(((ci <= sj + SEL_BLOCK - 1) & (ci + CMP_LEN - 1 >= sj)).astype(np.float32))
    span = WINDOW + qb_len
    padw = lambda t: jnp.pad(t, ((0, 0), (WINDOW, 0), (0, 0), (0, 0)))
    kw_p, vw_p = padw(kw), padw(vw)
    gather = jax.vmap(jax.vmap(lambda tab, ix: tab[ix]))
    jb = jnp.arange(ns, dtype=jnp.int32)

    def block(args):
        qb, gb, qs = args
        t = q_pos0 + qs + jnp.arange(qb_len, dtype=jnp.int32)
        s_c = jnp.einsum('bqkgd,bnkd->bkgqn', qb, kc) * scale
        p_c = masked_softmax(s_c, kc_end[None, :] <= t[:, None])
        o_c = jnp.einsum('bkgqn,bnkd->bqkgd', p_c.astype(vc.dtype), vc)
        imp = jnp.einsum('bkgqn,nj->bqkj', p_c, overlap)
        cur = (t // SEL_BLOCK)[:, None]
        forced = ((jb == 0) | (jb == cur) | (jb == cur - 1))[:, None, :]
        valid = (jb * SEL_BLOCK <= t[:, None])[:, None, :]
        score = jnp.where(valid, imp + SEL_FORCE * forced.astype(jnp.float32), -jnp.inf)
        _, idx = lax.top_k(score, k_top)
        idx = idx.transpose(0, 2, 1, 3)
        k_sel = gather(ks_b, idx).reshape(b, NSA_KV_HEADS, qb_len, k_top * SEL_BLOCK, NSA_HEAD_DIM)
        v_sel = gather(vs_b, idx).reshape(b, NSA_KV_HEADS, qb_len, k_top * SEL_BLOCK, NSA_HEAD_DIM)
        pos_s = (idx[..., None] * SEL_BLOCK + jnp.arange(SEL_BLOCK, dtype=jnp.int32)).reshape(
            b, NSA_KV_HEADS, qb_len, k_top * SEL_BLOCK)
        s_s = jnp.einsum('bqkgd,bkqmd->bkgqm', qb, k_sel) * scale
        p_s = masked_softmax(s_s, (pos_s <= t[:, None])[:, :, None])
        o_s = jnp.einsum('bkgqm,bkqmd->bqkgd', p_s.astype(v_sel.dtype), v_sel)
        start = q_pos0 + qs - kw_pos0
        k_win = lax.dynamic_slice_in_dim(kw_p, start, span, axis=1)
        v_win = lax.dynamic_slice_in_dim(vw_p, start, span, axis=1)
        pos_w = q_pos0 + qs - WINDOW + jnp.arange(span, dtype=jnp.int32)
        mask_w = (pos_w[None, :] <= t[:, None]) & (pos_w[None, :] >= t[:, None] - WINDOW) & (pos_w[None, :] >= 0)
        s_w = jnp.einsum('bqkgd,bskd->bkgqs', qb, k_win) * scale
        p_w = masked_softmax(s_w, mask_w)
        o_w = jnp.einsum('bkgqs,bskd->bqkgd', p_w.astype(v_win.dtype), v_win)
        return gb[..., 0:1] * o_c + gb[..., 1:2] * o_s + gb[..., 2:3] * o_w

    q_blocks = jnp.moveaxis(q.reshape((b, nb, qb_len) + q.shape[2:]), 1, 0)
    g_blocks = jnp.moveaxis(gates.reshape((b, nb, qb_len) + gates.shape[2:]), 1, 0)
    starts = jnp.arange(nb, dtype=jnp.int32) * qb_len
    out = lax.map(block, (q_blocks, g_blocks, starts))
    return jnp.moveaxis(out, 0, 1).reshape(b, lq, NSA_HEADS * NSA_HEAD_DIM)


def memory_kv(mem, norm, w_kv, k_norm):
    b, m, _ = mem.shape
    kv = rmsnorm(mem, norm) @ w_kv
    mk, mv = jnp.split(kv, 2, axis=-1)
    mk = rmsnorm(mk.reshape(b, m, MEM_HEADS, MEM_HEAD_DIM), k_norm)
    return mk, mv.reshape(b, m, MEM_HEADS, MEM_HEAD_DIM)


def hybrid_layer(x, pos0, past, mem_k, mem_v, w):
    (cmp_k_past, cmp_v_past, sel_k_past, sel_v_past, win_k_buf, win_v_buf,
     ssd_conv_prev, ssd_h0, ffn_conv_prev) = past
    b, length, _ = x.shape
    pos = pos0 + jnp.arange(length, dtype=jnp.int32)
    h = rmsnorm(x, w['norm_mix'])
    proj = h @ w['w_in']
    splits = np.cumsum(IN_SIZES)[:-1].tolist()
    z, xbc, dt_raw, q, kv, g_nsa, q_mem, g_merge = jnp.split(proj, splits, axis=-1)

    xbc, ssd_conv_new = causal_dwconv(xbc, ssd_conv_prev, w['ssd_conv_w'], w['ssd_conv_b'])
    xbc = jax.nn.silu(xbc).astype(jnp.float32)
    xs, bm, cm = jnp.split(xbc, [D_INNER, D_INNER + SSD_GROUPS * SSD_STATE], axis=-1)
    xh = xs.reshape(b, length, SSD_GROUPS, SSD_HPG, SSD_HEAD_DIM)
    bm = bm.reshape(b, length, SSD_GROUPS, SSD_STATE)
    cm = cm.reshape(b, length, SSD_GROUPS, SSD_STATE)
    dt = jax.nn.softplus(dt_raw.astype(jnp.float32) + w['ssd_dt_bias'].astype(jnp.float32))
    dt = dt.reshape(b, length, SSD_GROUPS, SSD_HPG)
    a = -jnp.exp(w['ssd_a_log'].astype(jnp.float32)).reshape(SSD_GROUPS, SSD_HPG)
    h0 = ssd_h0.astype(jnp.float32).reshape(b, SSD_GROUPS, SSD_HPG, SSD_HEAD_DIM, SSD_STATE)
    y, h_last = ssd_scan(xh, dt, a, bm, cm, h0)
    y = y + w['ssd_d'].astype(jnp.float32).reshape(SSD_GROUPS, SSD_HPG)[..., None] * xh
    y = y.reshape(b, length, D_INNER).astype(x.dtype)
    o_ssd = rmsnorm(y * jax.nn.silu(z), w['ssd_norm']) @ w['w_ssd_o']
    ssd_state_new = h_last.reshape(b, SSD_HEADS, SSD_HEAD_DIM, SSD_STATE)

    q = rope(rmsnorm(q.reshape(b, length, NSA_HEADS, NSA_HEAD_DIM), w['nsa_q_norm']), pos)
    q = q.reshape(b, length, NSA_KV_HEADS, NSA_GQ, NSA_HEAD_DIM)
    ck, cv, sk, sv, wk, wv = [t.reshape(b, length, NSA_KV_HEADS, NSA_HEAD_DIM) for t in jnp.split(kv, 6, axis=-1)]
    sk = rope(rmsnorm(sk, w['nsa_k_norm'][1]), pos)
    wk = rope(rmsnorm(wk, w['nsa_k_norm'][2]), pos)
    ck_all = jnp.concatenate([cmp_k_past.astype(ck.dtype), ck], axis=1)
    cv_all = jnp.concatenate([cmp_v_past.astype(cv.dtype), cv], axis=1)
    sk_all = jnp.concatenate([sel_k_past.astype(sk.dtype), sk], axis=1)
    sv_all = jnp.concatenate([sel_v_past.astype(sv.dtype), sv], axis=1)
    kc = compress(ck_all, w['cmp_k_pe'], w['cmp_k_w1'], w['cmp_k_w2'])
    kc_pos = jnp.arange(kc.shape[1], dtype=jnp.int32) * CMP_STRIDE + (CMP_LEN - 1)
    kc = rope(rmsnorm(kc, w['nsa_k_norm'][0]), kc_pos)
    vc = compress(cv_all, w['cmp_v_pe'], w['cmp_v_w1'], w['cmp_v_w2'])
    kw_all = jnp.concatenate([win_k_buf.astype(wk.dtype), wk], axis=1)
    vw_all = jnp.concatenate([win_v_buf.astype(wv.dtype), wv], axis=1)
    kw_pos0 = pos0 - win_k_buf.shape[1]
    gates = jax.nn.sigmoid(g_nsa).reshape(b, length, NSA_KV_HEADS, NSA_GQ, 3)
    o_nsa = nsa_attend(q, gates, pos0, kc, vc, sk_all, sv_all, kw_all, vw_all, kw_pos0) @ w['w_nsa_o']
    keep = min(WINDOW, kw_all.shape[1])
    win_k_new, win_v_new = kw_all[:, -keep:], vw_all[:, -keep:]

    qm = rmsnorm(q_mem.reshape(b, length, MEM_HEADS, MEM_HEAD_DIM), w['mem_q_norm'])
    s_m = jnp.einsum('bqhd,bmhd->bhqm', qm, mem_k.astype(qm.dtype)) * (MEM_HEAD_DIM ** -0.5)
    p_m = jax.nn.softmax(s_m.astype(jnp.float32), axis=-1)
    o_mem = jnp.einsum('bhqm,bmhd->bqhd', p_m.astype(x.dtype), mem_v.astype(x.dtype))
    o_mem = o_mem.reshape(b, length, MEM_HEADS * MEM_HEAD_DIM) @ w['w_mem_o']

    gm = jax.nn.sigmoid(g_merge.reshape(b, length, N_BRANCH, D_MODEL) + w['b_merge'])
    mixed = gm[:, :, 0] * o_ssd + gm[:, :, 1] * o_nsa + gm[:, :, 2] * o_mem
    x = x + mixed @ w['w_out']

    u = rmsnorm(x, w['norm_ffn']) @ w['w_up']
    u, ffn_conv_new = causal_dwconv(u, ffn_conv_prev, w['ffn_conv_w'], w['ffn_conv_b'])
    x = x + (jax.nn.silu(u[..., :D_FF]) * u[..., D_FF:]) @ w['w_down']
    return x, (ck, cv, sk, sv, win_k_new, win_v_new, ssd_conv_new, ssd_state_new, ffn_conv_new)


def setup_inputs(seed: int = 0) -> dict:
    key = jax.random.key(seed)
    keys = iter(jax.random.split(key, 64))
    f32 = jnp.float32

    def nrm(shape, scale=1.0):
        return jax.random.normal(next(keys), shape, f32) * scale

    def gain(shape):
        return 1.0 + 0.05 * nrm(shape)

    L = (DEPTH,)
    n_pages = PAST_LEN // PAGE_SIZE
    n_pool = -(-(DEC_BATCH * n_pages * 5) // 4)
    kvr = (NSA_KV_HEADS, NSA_HEAD_DIM)
    wbuf = min(WINDOW, PAST_LEN)
    perm = jax.random.permutation(next(keys), n_pool)[:DEC_BATCH * n_pages]
    page_table = perm.reshape(DEC_BATCH, n_pages).astype(jnp.int32)
    dt0 = jnp.exp(jax.random.uniform(next(keys), L + (SSD_HEADS,), f32, math.log(1e-3), math.log(1e-1)))
    dt_bias = dt0 + jnp.log(-jnp.expm1(-dt0))
    a_log = jnp.log(jax.random.uniform(next(keys), L + (SSD_HEADS,), f32, 1.0, 16.0))
    return {
        'x_prompt': nrm((BATCH, SEQ, D_MODEL)),
        'x_sample': nrm((DEC_BATCH, DEC_SEQ, D_MODEL)),
        'cache_nsa_cmp_k': nrm(L + (n_pool, PAGE_SIZE) + kvr),
        'cache_nsa_cmp_v': nrm(L + (n_pool, PAGE_SIZE) + kvr),
        'cache_nsa_sel_k': nrm(L + (n_pool, PAGE_SIZE) + kvr),
        'cache_nsa_sel_v': nrm(L + (n_pool, PAGE_SIZE) + kvr),
        'state_nsa_win_k': nrm(L + (DEC_BATCH, wbuf) + kvr),
        'state_nsa_win_v': nrm(L + (DEC_BATCH, wbuf) + kvr),
        'state_ssd_conv': nrm(L + (DEC_BATCH, SSD_CONV - 1, SSD_CONV_DIM)),
        'state_ssd': nrm(L + (DEC_BATCH, SSD_HEADS, SSD_HEAD_DIM, SSD_STATE), 0.1),
        'cache_mem_k': nrm(L + (DEC_BATCH, MEM_TOKENS, MEM_HEADS, MEM_HEAD_DIM)),
        'cache_mem_v': nrm(L + (DEC_BATCH, MEM_TOKENS, MEM_HEADS, MEM_HEAD_DIM)),
        'state_ffn_conv': nrm(L + (DEC_BATCH, FFN_CONV - 1, 2 * D_FF)),
        'page_table': page_table,
        'mem_prompt': nrm((BATCH, MEM_TOKENS, D_MODEL)),
        'norm_mix': gain(L + (D_MODEL,)),
        'w_in': nrm(L + (D_MODEL, D_IN_PROJ), D_MODEL ** -0.5),
        'b_merge': nrm(L + (N_BRANCH, D_MODEL), 0.1),
        'ssd_conv_w': nrm(L + (SSD_CONV, SSD_CONV_DIM), SSD_CONV ** -0.5),
        'ssd_conv_b': nrm(L + (SSD_CONV_DIM,), 0.02),
        'ssd_dt_bias': dt_bias,
        'ssd_a_log': a_log,
        'ssd_d': 1.0 + 0.1 * nrm(L + (SSD_HEADS,)),
        'ssd_norm': gain(L + (D_INNER,)),
        'w_ssd_o': nrm(L + (D_INNER, D_MODEL), D_INNER ** -0.5),
        'nsa_q_norm': gain(L + (NSA_HEAD_DIM,)),
        'nsa_k_norm': gain(L + (3, NSA_HEAD_DIM)),
        'cmp_k_pe': nrm(L + (CMP_LEN, NSA_HEAD_DIM), 0.02),
        'cmp_k_w1': nrm(L + (CMP_LEN * NSA_HEAD_DIM, CMP_HIDDEN), (CMP_LEN * NSA_HEAD_DIM) ** -0.5),
        'cmp_k_w2': nrm(L + (CMP_HIDDEN, NSA_HEAD_DIM), CMP_HIDDEN ** -0.5),
        'cmp_v_pe': nrm(L + (CMP_LEN, NSA_HEAD_DIM), 0.02),
        'cmp_v_w1': nrm(L + (CMP_LEN * NSA_HEAD_DIM, CMP_HIDDEN), (CMP_LEN * NSA_HEAD_DIM) ** -0.5),
        'cmp_v_w2': nrm(L + (CMP_HIDDEN, NSA_HEAD_DIM), CMP_HIDDEN ** -0.5),
        'w_nsa_o': nrm(L + (NSA_HEADS * NSA_HEAD_DIM, D_MODEL), (NSA_HEADS * NSA_HEAD_DIM) ** -0.5),
        'mem_norm': gain(L + (D_MODEL,)),
        'w_mem_kv': nrm(L + (D_MODEL, 2 * MEM_HEADS * MEM_HEAD_DIM), D_MODEL ** -0.5),
        'mem_q_norm': gain(L + (MEM_HEAD_DIM,)),
        'mem_k_norm': gain(L + (MEM_HEAD_DIM,)),
        'w_mem_o': nrm(L + (MEM_HEADS * MEM_HEAD_DIM, D_MODEL), (MEM_HEADS * MEM_HEAD_DIM) ** -0.5),
        'w_out': nrm(L + (D_MODEL, D_MODEL), D_MODEL ** -0.5),
        'norm_ffn': gain(L + (D_MODEL,)),
        'w_up': nrm(L + (D_MODEL, 2 * D_FF), D_MODEL ** -0.5),
        'ffn_conv_w': nrm(L + (FFN_CONV, 2 * D_FF), FFN_CONV ** -0.5),
        'ffn_conv_b': nrm(L + (2 * D_FF,), 0.02),
        'w_down': nrm(L + (D_FF, D_MODEL), D_FF ** -0.5),
    }


def reference(x_prompt, x_sample, cache_nsa_cmp_k, cache_nsa_cmp_v, cache_nsa_sel_k, cache_nsa_sel_v,
              state_nsa_win_k, state_nsa_win_v, state_ssd_conv, state_ssd, cache_mem_k, cache_mem_v,
              state_ffn_conv, page_table, mem_prompt, norm_mix, w_in, b_merge, ssd_conv_w, ssd_conv_b,
              ssd_dt_bias, ssd_a_log, ssd_d, ssd_norm, w_ssd_o, nsa_q_norm, nsa_k_norm, cmp_k_pe,
              cmp_k_w1, cmp_k_w2, cmp_v_pe, cmp_v_w1, cmp_v_w2, w_nsa_o, mem_norm, w_mem_kv,
              mem_q_norm, mem_k_norm, w_mem_o, w_out, norm_ffn, w_up, ffn_conv_w, ffn_conv_b, w_down):
    weights = dict(norm_mix=norm_mix, w_in=w_in, b_merge=b_merge, ssd_conv_w=ssd_conv_w,
                   ssd_conv_b=ssd_conv_b, ssd_dt_bias=ssd_dt_bias, ssd_a_log=ssd_a_log, ssd_d=ssd_d,
                   ssd_norm=ssd_norm, w_ssd_o=w_ssd_o, nsa_q_norm=nsa_q_norm, nsa_k_norm=nsa_k_norm,
                   cmp_k_pe=cmp_k_pe, cmp_k_w1=cmp_k_w1, cmp_k_w2=cmp_k_w2, cmp_v_pe=cmp_v_pe,
                   cmp_v_w1=cmp_v_w1, cmp_v_w2=cmp_v_w2, w_nsa_o=w_nsa_o, mem_norm=mem_norm,
                   w_mem_kv=w_mem_kv, mem_q_norm=mem_q_norm, mem_k_norm=mem_k_norm, w_mem_o=w_mem_o,
                   w_out=w_out, norm_ffn=norm_ffn, w_up=w_up, ffn_conv_w=ffn_conv_w,
                   ffn_conv_b=ffn_conv_b, w_down=w_down)
    bp, bs = x_prompt.shape[0], x_sample.shape[0]
    past_len = page_table.shape[1] * cache_nsa_cmp_k.shape[2]

    def pages(pool):
        return pool[page_table].reshape((bs, past_len) + pool.shape[2:])

    kv_empty = jnp.zeros((bp, 0, NSA_KV_HEADS, NSA_HEAD_DIM), x_prompt.dtype)
    y_p, y_s = x_prompt, x_sample
    out_p, out_s = [], []
    for layer in range(DEPTH):
        w = {name: arr[layer] for name, arr in weights.items()}
        mem_k, mem_v = memory_kv(mem_prompt, w['mem_norm'], w['w_mem_kv'], w['mem_k_norm'])
        past_p = (kv_empty, kv_empty, kv_empty, kv_empty, kv_empty, kv_empty,
                  jnp.zeros((bp, SSD_CONV - 1, SSD_CONV_DIM), x_prompt.dtype),
                  jnp.zeros((bp, SSD_HEADS, SSD_HEAD_DIM, SSD_STATE), jnp.float32),
                  jnp.zeros((bp, FFN_CONV - 1, 2 * D_FF), x_prompt.dtype))
        y_p, st_p = hybrid_layer(y_p, 0, past_p, mem_k, mem_v, w)
        out_p.append(st_p + (mem_k, mem_v))
        past_s = (pages(cache_nsa_cmp_k[layer]), pages(cache_nsa_cmp_v[layer]),
                  pages(cache_nsa_sel_k[layer]), pages(cache_nsa_sel_v[layer]),
                  state_nsa_win_k[layer], state_nsa_win_v[layer], state_ssd_conv[layer],
                  state_ssd[layer], state_ffn_conv[layer])
        y_s, st_s = hybrid_layer(y_s, past_len, past_s, cache_mem_k[layer], cache_mem_v[layer], w)
        out_s.append(st_s)
    (p_cmp_k, p_cmp_v, p_sel_k, p_sel_v, p_win_k, p_win_v, p_ssd_conv, p_ssd, p_ffn_conv,
     p_mem_k, p_mem_v) = [jnp.stack(s) for s in zip(*out_p)]
    (s_cmp_k, s_cmp_v, s_sel_k, s_sel_v, s_win_k, s_win_v, s_ssd_conv, s_ssd,
     s_ffn_conv) = [jnp.stack(s) for s in zip(*out_s)]
    return (y_p, y_s, p_cmp_k, p_cmp_v, p_sel_k, p_sel_v, p_win_k, p_win_v, p_ssd_conv, p_ssd,
            p_ffn_conv, p_mem_k, p_mem_v, s_cmp_k, s_cmp_v, s_sel_k, s_sel_v, s_win_k, s_win_v,
            s_ssd_conv, s_ssd, s_ffn_conv)
```

```python
import functools
import math

import numpy as np
import jax
import jax.numpy as jnp
from jax import lax
from jax.experimental import pallas as pl
from jax.experimental.pallas import tpu as pltpu

D_MODEL = 1024
D_INNER = 2048
SSD_HEAD_DIM = 64
SSD_HEADS = 32
SSD_GROUPS = 4
SSD_HPG = 8
SSD_STATE = 128
SSD_CONV = 4
SSD_CONV_DIM = 3072
SSD_CHUNK = 128
NSA_HEADS = 16
NSA_KV = 4
NSA_GQ = 4
NSA_HD = 64
CMP_LEN = 32
CMP_STRIDE = 16
CMP_HIDDEN = 128
SEL_BLOCK = 64
SEL_TOPN = 16
SEL_FORCE = 1.0e4
WINDOW = 512
QUERY_BLOCK = 128
MEM_HEADS = 4
MEM_HD = 256
D_FF = 2816
FFN_CONV = 3
ROPE_THETA = 10000.0
EPS = 1e-6
IN_SIZES = (2048, 3072, 32, 1024, 1536, 48, 1024, 3072)

VMEM_LIMIT = 48 * 1024 * 1024
MASK_BIG = 2.0 ** 20
NEG = -1.0e30

F32 = jnp.float32
BF16 = jnp.bfloat16
HI = lax.Precision.HIGHEST


def _cparams(*sem):
    return pltpu.CompilerParams(dimension_semantics=sem, vmem_limit_bytes=VMEM_LIMIT)


def _sigmoid(x):
    return 1.0 / (1.0 + jnp.exp(-x))


def _silu(x):
    return x * _sigmoid(x)


def _dot(a, b):
    return jnp.dot(a, b, preferred_element_type=F32)


def _dot_nt(a, b):
    return lax.dot_general(a, b, (((1,), (1,)), ((), ())), preferred_element_type=F32)


def _dot_hi(a, b):
    return jnp.dot(a, b, preferred_element_type=F32, precision=HI)


def _mm_kernel(x_ref, g_ref, w_ref, o_ref, xn_ref, *, norm):
    @pl.when(pl.program_id(1) == 0)
    def _():
        x = x_ref[...].astype(F32)
        if norm:
            ms = jnp.mean(x * x, axis=-1, keepdims=True)
            x = x * lax.rsqrt(ms + EPS) * g_ref[...]
        xn_ref[...] = x.astype(BF16)

    o_ref[...] = _dot(xn_ref[...], w_ref[...]).astype(o_ref.dtype)


def _matmul(x, w, g=None, out_dtype=F32, tm=512, tn=512):
    m, k = x.shape
    n = w.shape[1]
    tm = min(tm, m)
    tn = min(tn, n)
    assert m % tm == 0 and n % tn == 0, (m, n, tm, tn)
    norm = g is not None
    if g is None:
        g = jnp.ones((1, k), F32)
    return pl.pallas_call(
        functools.partial(_mm_kernel, norm=norm),
        grid=(m // tm, n // tn),
        in_specs=[pl.BlockSpec((tm, k), lambda i, j: (i, 0)),
                  pl.BlockSpec((1, k), lambda i, j: (0, 0)),
                  pl.BlockSpec((k, tn), lambda i, j: (0, j))],
        out_specs=pl.BlockSpec((tm, tn), lambda i, j: (i, j)),
        out_shape=jax.ShapeDtypeStruct((m, n), out_dtype),
        scratch_shapes=[pltpu.VMEM((tm, k), BF16)],
        compiler_params=_cparams("parallel", "arbitrary"),
        name="rms_matmul" if norm else "matmul",
    )(x, g.reshape(1, k).astype(F32), w)


def _shift_rows(x, k, prev, row):
    r = pltpu.roll(x, k, axis=0)
    for i in range(k):
        r = jnp.where(row == i, prev[8 - k + i:8 - k + i + 1, :], r)
    return r


def _ssd_prompt_kernel(xbc_ref, z_ref, dt_ref, dtT_ref, cw_ref, cb_ref, dtb_ref, dtbT_ref, a_ref, aT_ref,
                       dskip_ref, nw_ref, e_ref, y_ref, hT_ref, carry_ref, h_ref, yacc_ref):
    c = pl.program_id(1)
    q = SSD_CHUNK

    @pl.when(c == 0)
    def _():
        carry_ref[...] = jnp.zeros_like(carry_ref)
        h_ref[...] = jnp.zeros_like(h_ref)

    xbc = xbc_ref[0]
    row = lax.broadcasted_iota(jnp.int32, (q, 1), 0)
    prev = carry_ref[...]
    conv = cb_ref[...] + cw_ref[3:4, :] * xbc
    for k in range(1, SSD_CONV):
        conv = conv + cw_ref[3 - k:4 - k, :] * _shift_rows(xbc, k, prev, row)
    carry_ref[...] = xbc[q - 8:, :]
    xc = _silu(conv)
    xs = xc[:, :D_INNER]
    bm = xc[:, D_INNER:D_INNER + SSD_GROUPS * SSD_STATE].astype(BF16)
    cm = xc[:, D_INNER + SSD_GROUPS * SSD_STATE:].astype(BF16)

    def softplus(v):
        return jnp.maximum(v, 0.0) + jnp.log(1.0 + jnp.exp(-jnp.abs(v)))

    dt = softplus(dt_ref[0] + dtb_ref[...])
    dtT = softplus(dtT_ref[0] + dtbT_ref[...])
    ii = lax.broadcasted_iota(jnp.int32, (q, q), 0)
    jj = lax.broadcasted_iota(jnp.int32, (q, q), 1)
    causal = ii >= jj
    tri = causal.astype(F32)
    cum = _dot_hi(tri, dt * a_ref[...])
    cumT = _dot_hi(dtT * aT_ref[...], (jj >= ii).astype(F32))
    ecum = jnp.exp(cum)
    dend = jnp.exp(cum[q - 1:q, :] - cum)
    e = e_ref[...]
    dt_x = _dot_hi(dt, e)
    ecum_x = _dot_hi(ecum, e)
    dend_x = _dot_hi(dend, e)
    xdt = xs * dt_x
    xdt_b = xdt.astype(BF16)
    xw_b = (xdt * dend_x).astype(BF16)
    gw = SSD_HPG * SSD_HEAD_DIM
    for g in range(SSD_GROUPS):
        bg = bm[:, g * SSD_STATE:(g + 1) * SSD_STATE]
        cg = cm[:, g * SSD_STATE:(g + 1) * SSD_STATE]
        cb = _dot_nt(cg, bg)
        h_prev = h_ref[g]
        yoff = _dot(cg, h_prev.astype(BF16)) * ecum_x[:, g * gw:(g + 1) * gw]
        st = _dot(bg.astype(F32).T.astype(BF16), xw_b[:, g * gw:(g + 1) * gw])
        h_ref[g] = h_prev * ecum_x[q - 1:q, g * gw:(g + 1) * gw] + st
        for eh in range(SSD_HPG):
            hh = g * SSD_HPG + eh
            seg = cum[:, hh:hh + 1] - cumT[hh:hh + 1, :]
            decay = jnp.exp(jnp.where(causal, seg, NEG))
            mm = (cb * decay).astype(BF16)
            lo = hh * SSD_HEAD_DIM
            yd = _dot(mm, xdt_b[:, lo:lo + SSD_HEAD_DIM])
            yacc_ref[:, lo:lo + SSD_HEAD_DIM] = yd + yoff[:, eh * SSD_HEAD_DIM:(eh + 1) * SSD_HEAD_DIM]
    y = yacc_ref[...] + dskip_ref[...] * xs
    yz = y * _silu(z_ref[0])
    ms = jnp.mean(yz * yz, axis=-1, keepdims=True)
    y_ref[0] = (yz * lax.rsqrt(ms + EPS) * nw_ref[...]).astype(y_ref.dtype)

    @pl.when(c == pl.num_programs(1) - 1)
    def _():
        hT_ref[0] = h_ref[...]


def _head_expand():
    e = np.zeros((SSD_HEADS, D_INNER), np.float32)
    for h in range(SSD_HEADS):
        e[h, h * SSD_HEAD_DIM:(h + 1) * SSD_HEAD_DIM] = 1.0
    return jnp.asarray(e)


def _ssd_prompt(xbc, z, dt_raw, w):
    b, length, _ = xbc.shape
    q = SSD_CHUNK
    nc = length // q
    dtT = jnp.swapaxes(dt_raw, 1, 2)
    a = -jnp.exp(w['ssd_a_log'].astype(F32))
    full = lambda shape: pl.BlockSpec(shape, lambda i, j: (0,) * len(shape))
    y, hT = pl.pallas_call(
        _ssd_prompt_kernel,
        grid=(b, nc),
        in_specs=[pl.BlockSpec((1, q, SSD_CONV_DIM), lambda i, j: (i, j, 0)),
                  pl.BlockSpec((1, q, D_INNER), lambda i, j: (i, j, 0)),
                  pl.BlockSpec((1, q, SSD_HEADS), lambda i, j: (i, j, 0)),
                  pl.BlockSpec((1, SSD_HEADS, q), lambda i, j: (i, 0, j)),
                  full((SSD_CONV, SSD_CONV_DIM)), full((1, SSD_CONV_DIM)),
                  full((1, SSD_HEADS)), full((SSD_HEADS, 1)), full((1, SSD_HEADS)), full((SSD_HEADS, 1)),
                  full((1, D_INNER)), full((1, D_INNER)), full((SSD_HEADS, D_INNER))],
        out_specs=[pl.BlockSpec((1, q, D_INNER), lambda i, j: (i, j, 0)),
                   pl.BlockSpec((1, SSD_GROUPS, SSD_STATE, SSD_HPG * SSD_HEAD_DIM), lambda i, j: (i, 0, 0, 0))],
        out_shape=[jax.ShapeDtypeStruct((b, length, D_INNER), BF16),
                   jax.ShapeDtypeStruct((b, SSD_GROUPS, SSD_STATE, SSD_HPG * SSD_HEAD_DIM), F32)],
        scratch_shapes=[pltpu.VMEM((8, SSD_CONV_DIM), F32),
                        pltpu.VMEM((SSD_GROUPS, SSD_STATE, SSD_HPG * SSD_HEAD_DIM), F32),
                        pltpu.VMEM((q, D_INNER), F32)],
        compiler_params=_cparams("parallel", "arbitrary"),
        name="ssd_prompt",
    )(xbc, z, dt_raw, dtT, w['ssd_conv_w'], w['ssd_conv_b'].reshape(1, -1),
      w['ssd_dt_bias'].reshape(1, -1), w['ssd_dt_bias'].reshape(-1, 1), a.reshape(1, -1), a.reshape(-1, 1),
      jnp.repeat(w['ssd_d'].astype(F32), SSD_HEAD_DIM).reshape(1, -1), w['ssd_norm'].reshape(1, -1),
      _head_expand())
    state = hT.reshape(b, SSD_GROUPS, SSD_STATE, SSD_HPG, SSD_HEAD_DIM).transpose(0, 1, 3, 4, 2)
    return y, state.reshape(b, SSD_HEADS, SSD_HEAD_DIM, SSD_STATE)


def _segment_ones():
    i = np.arange(128)
    return jnp.asarray((i[:, None] // NSA_HD == i[None, :] // NSA_HD).astype(np.float32), BF16)


def _head_rmsnorm(x, g_row, segm):
    sq = x * x
    hi = sq.astype(BF16)
    lo = (sq - hi.astype(F32)).astype(BF16)
    parts = []
    for c in range(x.shape[1] // 128):
        sl = slice(c * 128, (c + 1) * 128)
        parts.append(_dot(hi[:, sl], segm) + _dot(lo[:, sl], segm))
    ss = parts[0] if len(parts) == 1 else jnp.concatenate(parts, axis=1)
    return x * lax.rsqrt(ss * (1.0 / NSA_HD) + EPS) * g_row


def _tile_lanes(t, width):
    reps = width // t.shape[1]
    return t if reps == 1 else jnp.concatenate([t] * reps, axis=1)


def _rope(x, cos128, sin128):
    width = x.shape[1]
    lane = lax.broadcasted_iota(jnp.int32, x.shape, 1)
    first = (lane & (NSA_HD // 2)) == 0
    rot = jnp.where(first, pltpu.roll(x, width - NSA_HD // 2, axis=1), pltpu.roll(x, NSA_HD // 2, axis=1))
    return x * _tile_lanes(cos128, width) + rot * _tile_lanes(sin128, width)


def _rope_tables(pos):
    half = NSA_HD // 2
    inv = ROPE_THETA ** (-jnp.arange(half, dtype=F32) / half)
    ang = pos.astype(F32)[:, None] * inv[None, :]
    cos, sin = jnp.cos(ang), jnp.sin(ang)
    cos128 = jnp.concatenate([cos, cos, cos, cos], axis=1)
    sin128 = jnp.concatenate([-sin, sin, -sin, sin], axis=1)
    return cos128, sin128


def _nsa_prep_kernel(q_ref, kv_ref, cos_ref, sin_ref, qg_ref, kg_ref, segm_ref,
                     qh_ref, skaug_ref, skf_ref, svh_ref, wkh_ref, wkf_ref, wvh_ref):
    tr = q_ref.shape[0]
    i = pl.program_id(1)
    cos, sin, segm = cos_ref[...], sin_ref[...], segm_ref[...]
    kvw = NSA_KV * NSA_HD
    q = _rope(_head_rmsnorm(q_ref[...], qg_ref[...], segm), cos, sin) * (NSA_HD ** -0.5)
    for h in range(NSA_HEADS):
        qh_ref[0, h] = q[:, h * NSA_HD:(h + 1) * NSA_HD].astype(BF16)
    sk = _rope(_head_rmsnorm(kv_ref[:, 2 * kvw:3 * kvw], kg_ref[1:2, :], segm), cos, sin)
    wk = _rope(_head_rmsnorm(kv_ref[:, 4 * kvw:5 * kvw], kg_ref[2:3, :], segm), cos, sin)
    skf_ref[...] = sk
    wkf_ref[...] = wk
    sv = kv_ref[:, 3 * kvw:4 * kvw]
    wv = kv_ref[:, 5 * kvw:6 * kvw]
    pos = i * tr + lax.broadcasted_iota(jnp.int32, (tr, NSA_HD), 0)
    blk = lax.broadcasted_iota(jnp.int32, (tr, NSA_HD), 1)
    onehot = jnp.where((pos >> 6) == blk, MASK_BIG, 0.0).astype(BF16)
    for k in range(NSA_KV):
        sl = slice(k * NSA_HD, (k + 1) * NSA_HD)
        skaug_ref[0, k] = jnp.concatenate([sk[:, sl].astype(BF16), onehot], axis=1)
        svh_ref[0, k] = sv[:, sl].astype(BF16)
        wkh_ref[0, k] = wk[:, sl].astype(BF16)
        wvh_ref[0, k] = wv[:, sl].astype(BF16)


def _nsa_prep(q, kv, b, length, w, tr=256):
    n = b * length
    nl = length // tr
    cos128, sin128 = _rope_tables(jnp.arange(length, dtype=jnp.int32))
    kvw = NSA_KV * NSA_HD
    row = lambda wd: pl.BlockSpec((tr, wd), lambda bi, i: (bi * nl + i, 0))
    tab = pl.BlockSpec((tr, 128), lambda bi, i: (i, 0))
    full = lambda shape: pl.BlockSpec(shape, lambda bi, i: (0,) * len(shape))
    hm = lambda nh, wd: pl.BlockSpec((1, nh, tr, wd), lambda bi, i: (bi, 0, i, 0))
    return pl.pallas_call(
        _nsa_prep_kernel,
        grid=(b, nl),
        in_specs=[row(NSA_HEADS * NSA_HD), row(6 * kvw), tab, tab, full((1, NSA_HEADS * NSA_HD)), full((3, kvw)),
                  full((128, 128))],
        out_specs=[hm(NSA_HEADS, NSA_HD), hm(NSA_KV, 2 * NSA_HD), row(kvw), hm(NSA_KV, NSA_HD), hm(NSA_KV, NSA_HD),
                   row(kvw), hm(NSA_KV, NSA_HD)],
        out_shape=[jax.ShapeDtypeStruct((b, NSA_HEADS, length, NSA_HD), BF16),
                   jax.ShapeDtypeStruct((b, NSA_KV, length, 2 * NSA_HD), BF16),
                   jax.ShapeDtypeStruct((n, kvw), F32),
                   jax.ShapeDtypeStruct((b, NSA_KV, length, NSA_HD), BF16),
                   jax.ShapeDtypeStruct((b, NSA_KV, length, NSA_HD), BF16),
                   jax.ShapeDtypeStruct((n, kvw), F32),
                   jax.ShapeDtypeStruct((b, NSA_KV, length, NSA_HD), BF16)],
        compiler_params=_cparams("parallel", "parallel"),
        name="nsa_prep",
    )(q, kv, cos128, sin128, jnp.tile(w['nsa_q_norm'], NSA_HEADS).reshape(1, -1),
      jnp.tile(w['nsa_k_norm'], (1, NSA_KV)), _segment_ones())


def _compress_kernel(*refs, n_x, is_k, n_prefetch):
    refs = refs[n_prefetch:]
    x_refs = (refs[:n_x], refs[n_x:2 * n_x])
    wbd_ref, pe_ref, w2_ref, g_ref, cos_ref, sin_ref, segm_ref, o_ref = refs[2 * n_x:]
    per = x_refs[0][0].shape[-2] // CMP_STRIDE
    nchunk = per * n_x
    acc = [jnp.zeros((nchunk, 4 * CMP_HIDDEN), F32) for _ in range(2)]
    for s in range(CMP_STRIDE):
        for p in range(2):
            xs = [r[pl.ds(s, per, stride=CMP_STRIDE), :] for r in x_refs[p]]
            x = (xs[0] if n_x == 1 else jnp.concatenate(xs, axis=0)).astype(BF16)
            acc[p] = acc[p] + _dot(x, wbd_ref[s])
    row = lax.broadcasted_iota(jnp.int32, (nchunk, 1), 0)
    outs = []
    for p in range(2):
        hid = []
        for kl in range(2):
            lo = kl * 2 * CMP_HIDDEN
            first = acc[p][:, lo:lo + CMP_HIDDEN]
            second = pltpu.roll(acc[p][:, lo + CMP_HIDDEN:lo + 2 * CMP_HIDDEN], nchunk - 1, axis=0)
            hid.append(_silu(first + second + pe_ref[...]))
        outs.append(_dot(jnp.concatenate(hid, axis=1).astype(BF16), w2_ref[...]))
    out = jnp.concatenate(outs, axis=1)
    if is_k:
        out = _rope(_head_rmsnorm(out, g_ref[...], segm_ref[...]), cos_ref[...], sin_ref[...])
    o_ref[0] = jnp.where(row < nchunk - 1, out, 0.0)


def _compress_weights(pe, w1, w2):
    w1r = w1.reshape(CMP_LEN, NSA_HD, CMP_HIDDEN)
    pe_term = jnp.einsum('ld,ldm->m', pe, w1r, precision=HI).reshape(1, CMP_HIDDEN)
    both = jnp.concatenate([w1r[:CMP_STRIDE], w1r[CMP_STRIDE:]], axis=-1)
    zero = jnp.zeros_like(both)
    wbd = jnp.concatenate([jnp.concatenate([both, zero], axis=-1), jnp.concatenate([zero, both], axis=-1)], axis=1)
    zero2 = jnp.zeros_like(w2)
    w2bd = jnp.concatenate([jnp.concatenate([w2, zero2], axis=-1), jnp.concatenate([zero2, w2], axis=-1)], axis=0)
    return wbd.astype(BF16), pe_term, w2bd.astype(BF16)


def _compress_call(x_args, x_specs, grid, nchunk, batch, pe, w1, w2, knorm, is_k, n_prefetch=0, prefetch=()):
    wbd, pe_term, w2bd = _compress_weights(pe, w1, w2)
    cos128, sin128 = _rope_tables(jnp.arange(nchunk, dtype=jnp.int32) * CMP_STRIDE + (CMP_LEN - 1))
    kvw = NSA_KV * NSA_HD
    full = lambda shape: pl.BlockSpec(shape, lambda *a: (0,) * len(shape))
    gs = pltpu.PrefetchScalarGridSpec(
        num_scalar_prefetch=n_prefetch, grid=grid,
        in_specs=list(x_specs) + [full(wbd.shape), full((1, CMP_HIDDEN)), full(w2bd.shape), full((1, kvw)),
                                  full((nchunk, 128)), full((nchunk, 128)), full((128, 128))],
        out_specs=pl.BlockSpec((1, nchunk, kvw), lambda i, *a: (i, 0, 0)))
    return pl.pallas_call(
        functools.partial(_compress_kernel, n_x=len(x_specs) // 2, is_k=is_k, n_prefetch=n_prefetch),
        grid_spec=gs,
        out_shape=jax.ShapeDtypeStruct((batch, nchunk, kvw), F32),
        compiler_params=_cparams("parallel"),
        name="compress_k" if is_k else "compress_v",
    )(*prefetch, *x_args, wbd, pe_term, w2bd, jnp.tile(knorm, NSA_KV).reshape(1, -1), cos128, sin128,
      _segment_ones())


def _compress_prompt(kv, col, b, length, pe, w1, w2, knorm, is_k):
    specs = [pl.BlockSpec((length, 128), functools.partial(lambda i, c: (i, c), c=2 * col + p)) for p in range(2)]
    return _compress_call([kv, kv], specs, (b,), length // CMP_STRIDE, b, pe, w1, w2, knorm, is_k)


SEL_TILE = 512
WIN_TILE = 128


def _flash_step(carry, s, v):
    m, l, acc = carry
    m_new = jnp.maximum(m, jnp.max(s, axis=-1, keepdims=True))
    alpha = jnp.exp(m - m_new)
    p = jnp.exp(s - m_new)
    l = alpha * l + jnp.sum(p, axis=-1, keepdims=True)
    acc = alpha * acc + _dot(p.astype(BF16), v)
    return m_new, l, acc


def _nsa_prompt_kernel(q_ref, kc_ref, vc_ref, sk_ref, sv_ref, wk_ref, wv_ref, g_ref, ov_ref, o_ref):
    qb = pl.program_id(2)
    qlen = QUERY_BLOCK
    rows = NSA_GQ * qlen
    t0 = qb * qlen
    q4 = q_ref[0].reshape(rows, NSA_HD)
    trow = t0 + (lax.broadcasted_iota(jnp.int32, (rows, 1), 0) & (qlen - 1))
    ncmp = kc_ref.shape[2]
    s = _dot_nt(q4, kc_ref[0, 0])
    kend = lax.broadcasted_iota(jnp.int32, (1, ncmp), 1) * CMP_STRIDE + (CMP_LEN - 1)
    valid = kend <= trow
    s = jnp.where(valid, s, NEG)
    m = jnp.max(s, axis=-1, keepdims=True)
    p = jnp.where(valid, jnp.exp(s - m), 0.0)
    den = jnp.sum(p, axis=-1, keepdims=True)
    p = p / jnp.where(den > 0, den, 1.0)
    o_c = _dot(p.astype(BF16), vc_ref[0, 0])
    psum = p[0:qlen] + p[qlen:2 * qlen] + p[2 * qlen:3 * qlen] + p[3 * qlen:4 * qlen]
    imp = _dot_hi(psum, ov_ref[...])
    nblk = ov_ref.shape[1]
    jb = lax.broadcasted_iota(jnp.int32, (1, nblk), 1)
    jbf = jb.astype(F32)
    t = t0 + lax.broadcasted_iota(jnp.int32, (qlen, 1), 0)
    cur = t >> 6
    forced = (jb == 0) | (jb == cur) | (jb == cur - 1)
    score = jnp.where(jb * SEL_BLOCK <= t, imp + jnp.where(forced, SEL_FORCE, 0.0), NEG)
    bias = jnp.full((qlen, nblk), -1.0, F32)
    for _ in range(min(SEL_TOPN, nblk)):
        mx = jnp.max(score, axis=-1, keepdims=True)
        idx = jnp.min(jnp.where(score == mx, jbf, float(nblk)), axis=-1, keepdims=True)
        pick = jbf == idx
        bias = jnp.where(pick, 0.0, bias)
        score = jnp.where(pick, -3.0e38, score)
    bias = bias.astype(BF16)
    qaug = jnp.concatenate([q4, jnp.concatenate([bias] * NSA_GQ, axis=0)], axis=1)
    init = (jnp.full((rows, 1), NEG, F32), jnp.zeros((rows, 1), F32), jnp.zeros((rows, NSA_HD), F32))

    def sel_body(kt, carry):
        k0 = pl.multiple_of(kt * SEL_TILE, SEL_TILE)
        sc = _dot_nt(qaug, sk_ref[0, 0, pl.ds(k0, SEL_TILE), :])
        return _flash_step(carry, sc, sv_ref[0, 0, pl.ds(k0, SEL_TILE), :])

    n_full = t0 // SEL_TILE
    carry = lax.fori_loop(0, n_full, sel_body, init)
    k0 = pl.multiple_of(n_full * SEL_TILE, SEL_TILE)
    kpos = k0 + lax.broadcasted_iota(jnp.int32, (1, SEL_TILE), 1)
    sc = jnp.where(kpos <= trow, _dot_nt(qaug, sk_ref[0, 0, pl.ds(k0, SEL_TILE), :]), NEG)
    _, l_s, acc_s = _flash_step(carry, sc, sv_ref[0, 0, pl.ds(k0, SEL_TILE), :])
    o_s = acc_s / l_s

    def win_body(kt, carry):
        k0 = pl.multiple_of(kt * WIN_TILE, WIN_TILE)
        kpos = k0 + lax.broadcasted_iota(jnp.int32, (1, WIN_TILE), 1)
        ok = (kpos <= trow) & (kpos >= trow - WINDOW)
        sc = jnp.where(ok, _dot_nt(q4, wk_ref[0, 0, pl.ds(k0, WIN_TILE), :]), NEG)
        return _flash_step(carry, sc, wv_ref[0, 0, pl.ds(k0, WIN_TILE), :])

    _, l_w, acc_w = lax.fori_loop(jnp.maximum(qb - WINDOW // WIN_TILE, 0), qb + 1, win_body, init)
    o_w = acc_w / l_w

    gate = _sigmoid(g_ref[0, 0])
    for g in range(NSA_GQ):
        r = slice(g * qlen, (g + 1) * qlen)
        og = (gate[:, 3 * g:3 * g + 1] * o_c[r] + gate[:, 3 * g + 1:3 * g + 2] * o_s[r]
              + gate[:, 3 * g + 2:3 * g + 3] * o_w[r])
        o_ref[:, g * NSA_HD:(g + 1) * NSA_HD] = og.astype(o_ref.dtype)


def _overlap_matrix(ncmp, nblk):
    ci = np.arange(ncmp)[:, None] * CMP_STRIDE
    sj = np.arange(nblk)[None, :] * SEL_BLOCK
    return jnp.asarray(((ci <= sj + SEL_BLOCK - 1) & (ci + CMP_LEN - 1 >= sj)).astype(np.float32))


def _nsa_prompt(qh, kc, vc, skaug, svh, wkh, wvh, gates, b, length):
    nb = length // QUERY_BLOCK
    ncmp = kc.shape[2]
    nblk = NSA_HD
    assert length // SEL_BLOCK <= nblk
    seq = lambda wd: pl.BlockSpec((1, 1, length, wd), lambda bi, k, i: (bi, k, 0, 0))
    cmp_spec = pl.BlockSpec((1, 1, ncmp, NSA_HD), lambda bi, k, i: (bi, k, 0, 0))
    return pl.pallas_call(
        _nsa_prompt_kernel,
        grid=(b, NSA_KV, nb),
        in_specs=[pl.BlockSpec((1, NSA_GQ, QUERY_BLOCK, NSA_HD), lambda bi, k, i: (bi, k, i, 0)),
                  cmp_spec, cmp_spec, seq(2 * NSA_HD), seq(NSA_HD), seq(NSA_HD), seq(NSA_HD),
                  pl.BlockSpec((1, 1, QUERY_BLOCK, 3 * NSA_GQ), lambda bi, k, i: (bi, k, i, 0)),
                  pl.BlockSpec((ncmp, nblk), lambda bi, k, i: (0, 0))],
        out_specs=pl.BlockSpec((QUERY_BLOCK, NSA_GQ * NSA_HD), lambda bi, k, i: (bi * nb + i, k)),
        out_shape=jax.ShapeDtypeStruct((b * length, NSA_HEADS * NSA_HD), BF16),
        compiler_params=_cparams("parallel", "parallel", "arbitrary"),
        name="nsa_prompt",
    )(qh, kc, vc, skaug, svh, wkh, wvh, gates, _overlap_matrix(ncmp, nblk))


def _mem_prompt_kernel(q_ref, mk_ref, mv_ref, g_ref, o_ref):
    for h in range(MEM_HEADS):
        sl = slice(h * MEM_HD, (h + 1) * MEM_HD)
        q = q_ref[:, sl]
        ms = jnp.mean(q * q, axis=-1, keepdims=True)
        qn = (q * lax.rsqrt(ms + EPS) * g_ref[...] * (MEM_HD ** -0.5)).astype(BF16)
        s = _dot_nt(qn, mk_ref[0, :, sl])
        m = jnp.max(s, axis=-1, keepdims=True)
        p = jnp.exp(s - m)
        p = p / jnp.sum(p, axis=-1, keepdims=True)
        o_ref[:, sl] = _dot(p.astype(BF16), mv_ref[0, :, sl]).astype(o_ref.dtype)


def _mem_prompt(q_mem, mk, mv, qnorm, b, length, tq=256):
    nl = length // tq
    width = MEM_HEADS * MEM_HD
    mem = pl.BlockSpec((1, mk.shape[1], width), lambda bi, i: (bi, 0, 0))
    return pl.pallas_call(
        _mem_prompt_kernel,
        grid=(b, nl),
        in_specs=[pl.BlockSpec((tq, width), lambda bi, i: (bi * nl + i, 0)), mem, mem,
                  pl.BlockSpec((1, MEM_HD), lambda bi, i: (0, 0))],
        out_specs=pl.BlockSpec((tq, width), lambda bi, i: (bi * nl + i, 0)),
        out_shape=jax.ShapeDtypeStruct((b * length, width), BF16),
        compiler_params=_cparams("parallel", "parallel"),
        name="mem_prompt",
    )(q_mem, mk, mv, qnorm.reshape(1, -1))


def _merge_kernel(x_ref, ys_ref, yn_ref, ym_ref, gm_ref, bm_ref, ws_ref, wn_ref, wm_ref, wo_ref, o_ref):
    gate = _sigmoid(gm_ref[...] + bm_ref[...])
    mixed = (gate[:, :D_MODEL] * _dot(ys_ref[...], ws_ref[...])
             + gate[:, D_MODEL:2 * D_MODEL] * _dot(yn_ref[...], wn_ref[...])
             + gate[:, 2 * D_MODEL:] * _dot(ym_ref[...], wm_ref[...]))
    o_ref[...] = x_ref[...] + _dot(mixed.astype(BF16), wo_ref[...])


def _merge(x, ys, yn, ym, gm, wb, tm=256):
    m = x.shape[0]
    tm = min(tm, m)
    row = lambda wd: pl.BlockSpec((tm, wd), lambda i: (i, 0))
    full = lambda shape: pl.BlockSpec(shape, lambda i: (0,) * len(shape))
    return pl.pallas_call(
        _merge_kernel,
        grid=(m // tm,),
        in_specs=[row(D_MODEL), row(D_INNER), row(D_MODEL), row(D_MODEL), row(3 * D_MODEL), full((1, 3 * D_MODEL)),
                  full((D_INNER, D_MODEL)), full((D_MODEL, D_MODEL)), full((D_MODEL, D_MODEL)),
                  full((D_MODEL, D_MODEL))],
        out_specs=row(D_MODEL),
        out_shape=jax.ShapeDtypeStruct((m, D_MODEL), F32),
        compiler_params=_cparams("parallel"),
        name="merge",
    )(x, ys, yn, ym, gm, wb['b_merge'].reshape(1, -1), wb['w_ssd_o'], wb['w_nsa_o'], wb['w_mem_o'], wb['w_out'])


def _ffn_down_kernel(*refs, seq):
    if seq:
        u_ref, x_ref, cw_ref, cb_ref, wd_ref, o_ref, carry_ref = refs
    else:
        u_ref, p0_ref, p1_ref, x_ref, cw_ref, cb_ref, wd_ref, o_ref = refs
    u = u_ref[...]
    if seq:
        @pl.when(pl.program_id(1) == 0)
        def _():
            carry_ref[...] = jnp.zeros_like(carry_ref)

        row = lax.broadcasted_iota(jnp.int32, (u.shape[0], 1), 0)
        prev = carry_ref[...]
        u1 = _shift_rows(u, 1, prev, row)
        u2 = _shift_rows(u, 2, prev, row)
        carry_ref[...] = u[u.shape[0] - 8:, :]
    else:
        u2, u1 = p0_ref[...], p1_ref[...]
    conv = cb_ref[...] + cw_ref[0:1, :] * u2 + cw_ref[1:2, :] * u1 + cw_ref[2:3, :] * u
    act = (_silu(conv[:, :D_FF]) * conv[:, D_FF:]).astype(BF16)
    o_ref[...] = x_ref[...] + _dot(act, wd_ref[...])


def _ffn_down_prompt(u, x1, wb, b, length, tm=256):
    nl = length // tm
    row = lambda wd: pl.BlockSpec((tm, wd), lambda bi, i: (bi * nl + i, 0))
    full = lambda shape: pl.BlockSpec(shape, lambda bi, i: (0,) * len(shape))
    return pl.pallas_call(
        functools.partial(_ffn_down_kernel, seq=True),
        grid=(b, nl),
        in_specs=[row(2 * D_FF), row(D_MODEL), full((FFN_CONV, 2 * D_FF)), full((1, 2 * D_FF)),
                  full((D_FF, D_MODEL))],
        out_specs=row(D_MODEL),
        out_shape=jax.ShapeDtypeStruct((b * length, D_MODEL), F32),
        scratch_shapes=[pltpu.VMEM((8, 2 * D_FF), F32)],
        compiler_params=_cparams("parallel", "arbitrary"),
        name="ffn_down_prompt",
    )(u, x1, wb['ffn_conv_w'], wb['ffn_conv_b'].reshape(1, -1), wb['w_down'])


def _ffn_down_sample(u, prev, x1, wb):
    m = u.shape[0]
    full = lambda shape: pl.BlockSpec(shape, lambda i: (0,) * len(shape))
    return pl.pallas_call(
        functools.partial(_ffn_down_kernel, seq=False),
        grid=(1,),
        in_specs=[full((m, 2 * D_FF)), full((m, 2 * D_FF)), full((m, 2 * D_FF)), full((m, D_MODEL)),
                  full((FFN_CONV, 2 * D_FF)), full((1, 2 * D_FF)), full((D_FF, D_MODEL))],
        out_specs=full((m, D_MODEL)),
        out_shape=jax.ShapeDtypeStruct((m, D_MODEL), F32),
        compiler_params=_cparams("arbitrary"),
        name="ffn_down_sample",
    )(u, prev[:, 0], prev[:, 1], x1, wb['ffn_conv_w'], wb['ffn_conv_b'].reshape(1, -1), wb['w_down'])


def _compress_sample(pool, page_table, pe, w1, w2, knorm, is_k):
    bs, npages = page_table.shape
    page = pool.shape[1]
    pool3 = pool.reshape(pool.shape[0], page, NSA_KV * NSA_HD)
    specs = [pl.BlockSpec((None, page, 128), functools.partial(lambda b, pt, j, p: (pt[b, j], 0, p), j=j, p=p))
             for p in range(2) for j in range(npages)]
    return _compress_call([pool3] * len(specs), specs, (bs,), npages * page // CMP_STRIDE, bs, pe, w1, w2, knorm,
                          is_k, n_prefetch=1, prefetch=(page_table,))


def _rows8(x):
    return jnp.broadcast_to(x, (8, x.shape[1]))


def _split3(x):
    hi = x.astype(BF16)
    r1 = x - hi.astype(F32)
    mid = r1.astype(BF16)
    lo = (r1 - mid.astype(F32)).astype(BF16)
    return hi, mid, lo


def _ssd_step_kernel(xbc_ref, prev_ref, z_ref, dt_ref, h_ref, cw_ref, cb_ref, dtb_ref, a_ref, dskip_ref, nw_ref,
                     e64_ref, e128_ref, y_ref, hn_ref):
    conv = cb_ref[...] + cw_ref[3:4, :] * xbc_ref[0]
    for k in range(SSD_CONV - 1):
        conv = conv + cw_ref[k:k + 1, :] * prev_ref[0, k:k + 1, :]
    xc = _silu(conv)
    xs = xc[:, :D_INNER]
    nb = SSD_GROUPS * SSD_STATE
    bm = xc[:, D_INNER:D_INNER + nb]
    cm = xc[:, D_INNER + nb:]
    v = dt_ref[0] + dtb_ref[...]
    dt = jnp.maximum(v, 0.0) + jnp.log(1.0 + jnp.exp(-jnp.abs(v)))
    dec = jnp.exp(dt * a_ref[...])
    xdt = xs * _dot_hi(_rows8(dt), e64_ref[...])[0:1]
    dec128 = _dot_hi(_rows8(dec), e128_ref[...])
    ii = lax.broadcasted_iota(jnp.int32, (128, 128), 0)
    jj = lax.broadcasted_iota(jnp.int32, (128, 128), 1)
    eye = ii == jj
    ones = jnp.ones((128, 128), BF16)
    pieces = []
    for c in range(D_INNER // 128):
        g = c // (SSD_HPG // 2)
        xrow = xdt[:, c * 128:(c + 1) * 128]
        diag = jnp.where(eye, jnp.broadcast_to(xrow, (128, 128)), 0.0)
        dh, dm, dl = _split3(diag)
        xcol = _dot(dh, ones) + _dot(dm, ones) + _dot(dl, ones)
        bg = jnp.broadcast_to(bm[:, g * SSD_STATE:(g + 1) * SSD_STATE], (128, SSD_STATE))
        decv = jnp.concatenate(
            [jnp.concatenate([dec128[:, hh * 128:(hh + 1) * 128]] * (SSD_HEAD_DIM // 8), axis=0)
             for hh in (2 * c, 2 * c + 1)], axis=0)
        hnew = decv * h_ref[0, c * 128:(c + 1) * 128, :] + xcol * bg
        hn_ref[0, c * 128:(c + 1) * 128, :] = hnew
        cg = _rows8(cm[:, g * SSD_STATE:(g + 1) * SSD_STATE]).astype(BF16)
        pieces.append(_dot_nt(cg, hnew.astype(BF16))[0:1])
    y = jnp.concatenate(pieces, axis=1) + dskip_ref[...] * xs
    yz = y * _silu(z_ref[0])
    ms = jnp.mean(yz * yz, axis=-1, keepdims=True)
    y_ref[0] = (yz * lax.rsqrt(ms + EPS) * nw_ref[...]).astype(y_ref.dtype)


def _head_expand128():
    e = np.zeros((SSD_HEADS, SSD_HEADS * 128), np.float32)
    for h in range(SSD_HEADS):
        e[h, h * 128:(h + 1) * 128] = 1.0
    return jnp.asarray(e)


def _ssd_step(xbc, prev, z, dt_raw, h0, w):
    bs = xbc.shape[0]
    a = -jnp.exp(w['ssd_a_log'].astype(F32))
    row = lambda wd: pl.BlockSpec((1, 1, wd), lambda i: (i, 0, 0))
    full = lambda shape: pl.BlockSpec(shape, lambda i: (0,) * len(shape))
    st = pl.BlockSpec((1, D_INNER, SSD_STATE), lambda i: (i, 0, 0))
    y, hn = pl.pallas_call(
        _ssd_step_kernel,
        grid=(bs,),
        in_specs=[row(SSD_CONV_DIM), pl.BlockSpec((1, SSD_CONV - 1, SSD_CONV_DIM), lambda i: (i, 0, 0)),
                  row(D_INNER), row(SSD_HEADS), st,
                  full((SSD_CONV, SSD_CONV_DIM)), full((1, SSD_CONV_DIM)), full((1, SSD_HEADS)), full((1, SSD_HEADS)),
                  full((1, D_INNER)), full((1, D_INNER)), full((SSD_HEADS, D_INNER)),
                  full((SSD_HEADS, SSD_HEADS * 128))],
        out_specs=[row(D_INNER), st],
        out_shape=[jax.ShapeDtypeStruct((bs, 1, D_INNER), BF16),
                   jax.ShapeDtypeStruct((bs, D_INNER, SSD_STATE), F32)],
        compiler_params=_cparams("parallel"),
        name="ssd_step",
    )(xbc.reshape(bs, 1, -1), prev, z.reshape(bs, 1, -1), dt_raw.reshape(bs, 1, -1),
      h0.reshape(bs, D_INNER, SSD_STATE), w['ssd_conv_w'], w['ssd_conv_b'].reshape(1, -1),
      w['ssd_dt_bias'].reshape(1, -1), a.reshape(1, -1),
      jnp.repeat(w['ssd_d'].astype(F32), SSD_HEAD_DIM).reshape(1, -1), w['ssd_norm'].reshape(1, -1),
      _head_expand(), _head_expand128())
    return y.reshape(bs, D_INNER), hn.reshape(bs, SSD_HEADS, SSD_HEAD_DIM, SSD_STATE)


def _softmax_with_extra(s, s_new, valid=None):
    if valid is not None:
        s = jnp.where(valid, s, NEG)
    m = jnp.maximum(jnp.max(s, axis=-1, keepdims=True), s_new)
    p = jnp.exp(s - m)
    if valid is not None:
        p = jnp.where(valid, p, 0.0)
    p_new = jnp.exp(s_new - m)
    inv = 1.0 / (jnp.sum(p, axis=-1, keepdims=True) + p_new)
    return p * inv, p_new * inv


def _nsa_sample_kernel(*refs, npages, past_len):
    pt_ref = refs[0]
    del pt_ref
    q_ref, kv_ref, gate_ref = refs[1:4]
    skp = refs[4:4 + npages]
    svp = refs[4 + npages:4 + 2 * npages]
    (kc_ref, vc_ref, wk_ref, wv_ref, qg_ref, kg_ref, cos_ref, sin_ref, segm_ref, ov_ref, eblk_ref, fold_ref,
     foldt_ref, o_ref, skn_ref, wkn_ref, wko_ref, wvo_ref) = refs[4 + 2 * npages:]
    kvw = NSA_KV * NSA_HD
    cos, sin, segm = cos_ref[...], sin_ref[...], segm_ref[...]
    q = _rope(_head_rmsnorm(_rows8(q_ref[0]), qg_ref[...], segm), _rows8(cos), _rows8(sin)) * (NSA_HD ** -0.5)
    kv = kv_ref[0]
    sk_new = _rope(_head_rmsnorm(_rows8(kv[:, 2 * kvw:3 * kvw]), kg_ref[1:2, :], segm), _rows8(cos), _rows8(sin))[0:1]
    wk_new = _rope(_head_rmsnorm(_rows8(kv[:, 4 * kvw:5 * kvw]), kg_ref[2:3, :], segm), _rows8(cos), _rows8(sin))[0:1]
    sv_new = kv[:, 3 * kvw:4 * kvw]
    wv_new = kv[:, 5 * kvw:6 * kvw]
    skn_ref[0] = sk_new
    wkn_ref[0] = wk_new
    hrow = lax.broadcasted_iota(jnp.int32, (NSA_HEADS, NSA_HEADS * NSA_HD), 0)
    hcol = lax.broadcasted_iota(jnp.int32, (NSA_HEADS, NSA_HEADS * NSA_HD), 1)
    own = (hcol >> 6) == hrow
    q16 = jnp.where(own, jnp.concatenate([q, q], axis=0), 0.0).astype(BF16)
    qbd = _dot(q16, fold_ref[...])
    qbd_b = qbd.astype(BF16)
    ncmp = kc_ref.shape[1]
    kend = lax.broadcasted_iota(jnp.int32, (1, ncmp), 1) * CMP_STRIDE + (CMP_LEN - 1)
    valid_c = kend <= past_len
    s_c = jnp.where(valid_c, _dot_nt(qbd_b, kc_ref[0].astype(BF16)), NEG)
    m_c = jnp.max(s_c, axis=-1, keepdims=True)
    p_c = jnp.where(valid_c, jnp.exp(s_c - m_c), 0.0)
    den = jnp.sum(p_c, axis=-1, keepdims=True)
    p_c = p_c / jnp.where(den > 0, den, 1.0)
    o_c = _dot(p_c.astype(BF16), vc_ref[0].astype(BF16))
    gi = lax.broadcasted_iota(jnp.int32, (NSA_HEADS, NSA_HEADS), 0) // NSA_GQ
    gj = lax.broadcasted_iota(jnp.int32, (NSA_HEADS, NSA_HEADS), 1) // NSA_GQ
    imp = _dot_hi(_dot_hi((gi == gj).astype(F32), p_c), ov_ref[...])
    nslot = ov_ref.shape[1]
    jb = lax.broadcasted_iota(jnp.int32, (1, nslot), 1)
    jbf = jb.astype(F32)
    cur = past_len // SEL_BLOCK
    forced = (jb == 0) | (jb == cur) | (jb == cur - 1)
    score = jnp.where(jb * SEL_BLOCK <= past_len, imp + jnp.where(forced, SEL_FORCE, 0.0), NEG)
    bias = jnp.full((NSA_HEADS, nslot), -1.0, F32)
    for _ in range(SEL_TOPN):
        mx = jnp.max(score, axis=-1, keepdims=True)
        idx = jnp.min(jnp.where(score == mx, jbf, float(nslot)), axis=-1, keepdims=True)
        pick = jbf == idx
        bias = jnp.where(pick, 0.0, bias)
        score = jnp.where(pick, -3.0e38, score)
    kmask = _dot(bias.astype(BF16), eblk_ref[...])
    sk_all = jnp.concatenate([r[...] for r in skp], axis=0).astype(BF16)
    sv_all = jnp.concatenate([r[...] for r in svp], axis=0).astype(BF16)
    s_s = _dot_nt(qbd_b, sk_all) + kmask
    s_new = jnp.sum(qbd * sk_new, axis=-1, keepdims=True)
    p_s, p_new = _softmax_with_extra(s_s, s_new)
    o_s = _dot(p_s.astype(BF16), sv_all) + p_new * sv_new
    wk = wk_ref[0]
    wv = wv_ref[0]
    s_w = _dot_nt(qbd_b, wk.astype(BF16))
    s_wn = jnp.sum(qbd * wk_new, axis=-1, keepdims=True)
    p_w, p_wn = _softmax_with_extra(s_w, s_wn)
    o_w = _dot(p_w.astype(BF16), wv.astype(BF16)) + p_wn * wv_new
    g16 = jnp.where((lax.broadcasted_iota(jnp.int32, (NSA_HEADS, 3 * NSA_HEADS), 1) // 3)
                    == lax.broadcasted_iota(jnp.int32, (NSA_HEADS, 3 * NSA_HEADS), 0),
                    jnp.broadcast_to(_sigmoid(gate_ref[0]), (NSA_HEADS, 3 * NSA_HEADS)), 0.0)
    br = lax.broadcasted_iota(jnp.int32, (NSA_HEADS, 3 * NSA_HEADS), 1) % 3
    gsel = lambda r: jnp.sum(jnp.where(br == r, g16, 0.0), axis=-1, keepdims=True)
    o16 = gsel(0) * o_c + gsel(1) * o_s + gsel(2) * o_w
    ox = _dot(o16.astype(BF16), foldt_ref[...])
    o_ref[0] = jnp.sum(jnp.where(own, ox, 0.0), axis=0, keepdims=True).astype(o_ref.dtype)
    nwin = wk.shape[0]
    wrow = lax.broadcasted_iota(jnp.int32, (nwin, 1), 0)
    wko_ref[0] = jnp.where(wrow == nwin - 1, wk_new, pltpu.roll(wk, nwin - 1, axis=0))
    wvo_ref[0] = jnp.where(wrow == nwin - 1, wv_new, pltpu.roll(wv, nwin - 1, axis=0))


def _nsa_sample(q, kv, gates, pool_sk, pool_sv, page_table, kc, vc, win_k, win_v, w):
    bs, npages = page_table.shape
    page = pool_sk.shape[1]
    past_len = npages * page
    kvw = NSA_KV * NSA_HD
    nwin = win_k.shape[1]
    assert nwin == WINDOW and past_len % SEL_BLOCK == 0
    ncmp = kc.shape[1]
    nslot = 128
    assert past_len // SEL_BLOCK + 1 <= nslot
    cos128, sin128 = _rope_tables(jnp.full((1,), past_len, jnp.int32))
    overlap = _overlap_matrix(ncmp, nslot)
    key_blk = np.arange(past_len) // SEL_BLOCK
    eblk = jnp.asarray((np.arange(nslot)[:, None] == key_blk[None, :]).astype(np.float32) * MASK_BIG, BF16)
    src = np.arange(NSA_HEADS * NSA_HD)
    dst = (src // NSA_HD // NSA_GQ) * NSA_HD + src % NSA_HD
    fold_np = np.zeros((NSA_HEADS * NSA_HD, kvw), np.float32)
    fold_np[src, dst] = 1.0
    fold = jnp.asarray(fold_np, BF16)
    foldt = jnp.asarray(fold_np.T, BF16)
    r3 = lambda t: t.reshape(bs, 1, -1)
    row = lambda wd: pl.BlockSpec((1, 1, wd), lambda b, pt: (b, 0, 0))
    full = lambda shape: pl.BlockSpec(shape, lambda b, pt: (0,) * len(shape))
    per_b = lambda r, c: pl.BlockSpec((1, r, c), lambda b, pt: (b, 0, 0))
    pages = [pl.BlockSpec((None, page, kvw), functools.partial(lambda b, pt, j: (pt[b, j], 0, 0), j=j))
             for j in range(npages)]
    pk3 = pool_sk.reshape(pool_sk.shape[0], page, kvw)
    pv3 = pool_sv.reshape(pool_sv.shape[0], page, kvw)
    gs = pltpu.PrefetchScalarGridSpec(
        num_scalar_prefetch=1, grid=(bs,),
        in_specs=[row(NSA_HEADS * NSA_HD), row(6 * kvw), row(3 * NSA_HEADS)] + pages + pages
        + [per_b(ncmp, kvw), per_b(ncmp, kvw), per_b(nwin, kvw), per_b(nwin, kvw),
           full((1, NSA_HEADS * NSA_HD)), full((3, kvw)), full((1, 128)), full((1, 128)), full((128, 128)),
           full((ncmp, nslot)), full((nslot, past_len)), full(fold.shape), full(foldt.shape)],
        out_specs=[row(NSA_HEADS * NSA_HD), row(kvw), row(kvw), per_b(nwin, kvw), per_b(nwin, kvw)])
    return pl.pallas_call(
        functools.partial(_nsa_sample_kernel, npages=npages, past_len=past_len),
        grid_spec=gs,
        out_shape=[jax.ShapeDtypeStruct((bs, 1, NSA_HEADS * NSA_HD), BF16),
                   jax.ShapeDtypeStruct((bs, 1, kvw), F32), jax.ShapeDtypeStruct((bs, 1, kvw), F32),
                   jax.ShapeDtypeStruct((bs, nwin, kvw), F32), jax.ShapeDtypeStruct((bs, nwin, kvw), F32)],
        compiler_params=_cparams("parallel"),
        name="nsa_sample",
    )(page_table, r3(q), r3(kv), r3(gates), *([pk3] * npages), *([pv3] * npages), kc, vc,
      win_k.reshape(bs, nwin, kvw), win_v.reshape(bs, nwin, kvw),
      jnp.tile(w['nsa_q_norm'], NSA_HEADS).reshape(1, -1), jnp.tile(w['nsa_k_norm'], (1, NSA_KV)),
      cos128, sin128, _segment_ones(), overlap, eblk, fold, foldt)


def _mem_sample_kernel(q_ref, mk_ref, mv_ref, g_ref, o_ref):
    width = MEM_HEADS * MEM_HD
    q = q_ref[0]
    parts = []
    for h in range(MEM_HEADS):
        qh = q[:, h * MEM_HD:(h + 1) * MEM_HD]
        ms = jnp.mean(qh * qh, axis=-1, keepdims=True)
        parts.append(qh * lax.rsqrt(ms + EPS) * g_ref[...] * (MEM_HD ** -0.5))
    qn = jnp.concatenate(parts, axis=1)
    hrow = lax.broadcasted_iota(jnp.int32, (8, width), 0)
    hcol = lax.broadcasted_iota(jnp.int32, (8, width), 1)
    own = (hcol >> 8) == hrow
    qbd = jnp.where(own, _rows8(qn), 0.0).astype(BF16)
    s = _dot_nt(qbd, mk_ref[0].astype(BF16))
    m = jnp.max(s, axis=-1, keepdims=True)
    p = jnp.exp(s - m)
    p = p / jnp.sum(p, axis=-1, keepdims=True)
    o8 = _dot(p.astype(BF16), mv_ref[0].astype(BF16))
    o_ref[0] = jnp.sum(jnp.where(own, o8, 0.0), axis=0, keepdims=True).astype(o_ref.dtype)


def _mem_sample(q_mem, mem_k, mem_v, qnorm):
    bs = q_mem.shape[0]
    width = MEM_HEADS * MEM_HD
    mtok = mem_k.shape[1]
    row = pl.BlockSpec((1, 1, width), lambda i: (i, 0, 0))
    mem = pl.BlockSpec((1, mtok, width), lambda i: (i, 0, 0))
    return pl.pallas_call(
        _mem_sample_kernel,
        grid=(bs,),
        in_specs=[row, mem, mem, pl.BlockSpec((1, MEM_HD), lambda i: (0, 0))],
        out_specs=row,
        out_shape=jax.ShapeDtypeStruct((bs, 1, width), BF16),
        compiler_params=_cparams("parallel"),
        name="mem_sample",
    )(q_mem.reshape(bs, 1, width), mem_k.reshape(bs, mtok, width), mem_v.reshape(bs, mtok, width),
      qnorm.reshape(1, -1)).reshape(bs, width)


def _split_w_in(w_in):
    offs = np.cumsum((0,) + IN_SIZES)
    piece = lambda i: w_in[:, int(offs[i]):int(offs[i + 1])]
    small = jnp.concatenate([piece(2), piece(5)], axis=1)
    small = jnp.pad(small, ((0, 0), (0, 128 - small.shape[1])))
    return dict(z=piece(0), xbc=piece(1), small=small, q=piece(3), kv=piece(4), qmem=piece(6), gmerge=piece(7))


def _in_proj(x, norm_w, wi):
    proj = {name: _matmul(x, wmat, norm_w) for name, wmat in wi.items()}
    proj['dt'] = proj['small'][:, :SSD_HEADS]
    proj['gnsa'] = proj['small'][:, SSD_HEADS:SSD_HEADS + 3 * NSA_HEADS]
    return proj


def _prompt_layer(x_prompt, mem_prompt, w, wb, wi):
    b, length, _ = x_prompt.shape
    n = b * length
    kvw = NSA_KV * NSA_HD
    x = x_prompt.reshape(n, D_MODEL)
    proj = _in_proj(x, w['norm_mix'], wi)
    xbc = proj['xbc'].reshape(b, length, SSD_CONV_DIM)
    y_ssd, ssd_state = _ssd_prompt(xbc, proj['z'].reshape(b, length, D_INNER),
                                   proj['dt'].reshape(b, length, SSD_HEADS), w)
    kv = proj['kv']
    qh, skaug, skf, svh, wkh, wkf, wvh = _nsa_prep(proj['q'], kv, b, length, w)
    kc = _compress_prompt(kv, 0, b, length, w['cmp_k_pe'], w['cmp_k_w1'], w['cmp_k_w2'], w['nsa_k_norm'][0], True)
    vc = _compress_prompt(kv, 1, b, length, w['cmp_v_pe'], w['cmp_v_w1'], w['cmp_v_w2'], w['nsa_k_norm'][0], False)
    to_heads = lambda t: t.reshape(b, -1, NSA_KV, NSA_HD).transpose(0, 2, 1, 3).astype(BF16)
    gates = proj['gnsa'].reshape(b, length, NSA_KV, 3 * NSA_GQ).transpose(0, 2, 1, 3)
    o_nsa = _nsa_prompt(qh, to_heads(kc), to_heads(vc), skaug, svh, wkh, wvh, gates, b, length)
    mem = mem_prompt.reshape(-1, D_MODEL)
    mkv = _matmul(mem, wb['w_mem_kv'], w['mem_norm'], tm=256)
    mtok = mem_prompt.shape[1]
    mk = mkv[:, :MEM_HEADS * MEM_HD].reshape(b, mtok, MEM_HEADS, MEM_HD)
    mk = mk * lax.rsqrt(jnp.mean(mk * mk, axis=-1, keepdims=True) + EPS) * w['mem_k_norm']
    mv = mkv[:, MEM_HEADS * MEM_HD:].reshape(b, mtok, MEM_HEADS, MEM_HD)
    o_mem = _mem_prompt(proj['qmem'], mk.reshape(b, mtok, -1).astype(BF16), mv.reshape(b, mtok, -1).astype(BF16),
                        w['mem_q_norm'], b, length)
    x1 = _merge(x, y_ssd.reshape(n, D_INNER), o_nsa, o_mem, proj['gmerge'], wb)
    u = _matmul(x1, wb['w_up'], w['norm_ffn'])
    y = _ffn_down_prompt(u, x1, wb, b, length)
    r4 = lambda t: t.reshape(b, length, NSA_KV, NSA_HD)
    keep = min(WINDOW, length)
    state = (r4(kv[:, 0:kvw]), r4(kv[:, kvw:2 * kvw]), r4(skf), r4(kv[:, 3 * kvw:4 * kvw]),
             r4(wkf)[:, -keep:], r4(kv[:, 5 * kvw:6 * kvw])[:, -keep:],
             xbc[:, -(SSD_CONV - 1):], ssd_state, u.reshape(b, length, 2 * D_FF)[:, -(FFN_CONV - 1):], mk, mv)
    return y.reshape(b, length, D_MODEL), state


def _sample_layer(x_sample, caches, page_table, w, wb, wi):
    (pool_ck, pool_cv, pool_sk, pool_sv, win_k, win_v, conv_prev, ssd_h0, mem_k, mem_v, ffn_prev) = caches
    bs = x_sample.shape[0]
    kvw = NSA_KV * NSA_HD
    x = x_sample.reshape(bs, D_MODEL)
    proj = _in_proj(x, w['norm_mix'], wi)
    y_ssd, ssd_state = _ssd_step(proj['xbc'], conv_prev, proj['z'], proj['dt'], ssd_h0, w)
    kv = proj['kv']
    kc = _compress_sample(pool_ck, page_table, w['cmp_k_pe'], w['cmp_k_w1'], w['cmp_k_w2'], w['nsa_k_norm'][0], True)
    vc = _compress_sample(pool_cv, page_table, w['cmp_v_pe'], w['cmp_v_w1'], w['cmp_v_w2'], w['nsa_k_norm'][0], False)
    o_nsa, sk_new, wk_new, win_k_new, win_v_new = _nsa_sample(
        proj['q'], kv, proj['gnsa'], pool_sk, pool_sv, page_table, kc, vc, win_k, win_v, w)
    o_mem = _mem_sample(proj['qmem'], mem_k, mem_v, w['mem_q_norm'])
    x1 = _merge(x, y_ssd, o_nsa.reshape(bs, -1), o_mem, proj['gmerge'], wb)
    u = _matmul(x1, wb['w_up'], w['norm_ffn'])
    y = _ffn_down_sample(u, ffn_prev, x1, wb)
    r4 = lambda t: t.reshape(bs, -1, NSA_KV, NSA_HD)
    state = (r4(kv[:, 0:kvw]), r4(kv[:, kvw:2 * kvw]), r4(sk_new), r4(kv[:, 3 * kvw:4 * kvw]),
             r4(win_k_new), r4(win_v_new),
             jnp.concatenate([conv_prev[:, 1:], proj['xbc'][:, None]], axis=1), ssd_state,
             jnp.concatenate([ffn_prev[:, 1:], u[:, None]], axis=1))
    return y.reshape(bs, 1, D_MODEL), state


def kernel(x_prompt, x_sample, cache_nsa_cmp_k, cache_nsa_cmp_v, cache_nsa_sel_k, cache_nsa_sel_v, state_nsa_win_k, state_nsa_win_v, state_ssd_conv, state_ssd, cache_mem_k, cache_mem_v, state_ffn_conv, page_table, mem_prompt, norm_mix, w_in, b_merge, ssd_conv_w, ssd_conv_b, ssd_dt_bias, ssd_a_log, ssd_d, ssd_norm, w_ssd_o, nsa_q_norm, nsa_k_norm, cmp_k_pe, cmp_k_w1, cmp_k_w2, cmp_v_pe, cmp_v_w1, cmp_v_w2, w_nsa_o, mem_norm, w_mem_kv, mem_q_norm, mem_k_norm, w_mem_o, w_out, norm_ffn, w_up, ffn_conv_w, ffn_conv_b, w_down):
    weights = dict(norm_mix=norm_mix, w_in=w_in, b_merge=b_merge, ssd_conv_w=ssd_conv_w,
                   ssd_conv_b=ssd_conv_b, ssd_dt_bias=ssd_dt_bias, ssd_a_log=ssd_a_log, ssd_d=ssd_d,
                   ssd_norm=ssd_norm, w_ssd_o=w_ssd_o, nsa_q_norm=nsa_q_norm, nsa_k_norm=nsa_k_norm,
                   cmp_k_pe=cmp_k_pe, cmp_k_w1=cmp_k_w1, cmp_k_w2=cmp_k_w2, cmp_v_pe=cmp_v_pe,
                   cmp_v_w1=cmp_v_w1, cmp_v_w2=cmp_v_w2, w_nsa_o=w_nsa_o, mem_norm=mem_norm,
                   w_mem_kv=w_mem_kv, mem_q_norm=mem_q_norm, mem_k_norm=mem_k_norm, w_mem_o=w_mem_o,
                   w_out=w_out, norm_ffn=norm_ffn, w_up=w_up, ffn_conv_w=ffn_conv_w,
                   ffn_conv_b=ffn_conv_b, w_down=w_down)
    w = {name: arr[0] for name, arr in weights.items()}
    wb = dict(w)
    for name in ('w_ssd_o', 'w_nsa_o', 'w_mem_o', 'w_out', 'w_up', 'w_down', 'w_mem_kv'):
        wb[name] = w[name].astype(BF16)
    wi = {name: piece.astype(BF16) for name, piece in _split_w_in(w['w_in']).items()}
    y_p, st_p = _prompt_layer(x_prompt, mem_prompt, w, wb, wi)
    caches = (cache_nsa_cmp_k[0], cache_nsa_cmp_v[0], cache_nsa_sel_k[0], cache_nsa_sel_v[0], state_nsa_win_k[0],
              state_nsa_win_v[0], state_ssd_conv[0], state_ssd[0], cache_mem_k[0], cache_mem_v[0], state_ffn_conv[0])
    y_s, st_s = _sample_layer(x_sample, caches, page_table, w, wb, wi)
    return (y_p, y_s) + tuple(s[None] for s in st_p) + tuple(s[None] for s in st_s)
```

```python
import functools
import math

import numpy as np
import jax
import jax.numpy as jnp
from jax import lax
from jax.experimental import pallas as pl
from jax.experimental.pallas import tpu as pltpu

D_MODEL = 1024
D_INNER = 2048
SSD_HEAD_DIM = 64
SSD_HEADS = 32
SSD_GROUPS = 4
SSD_HPG = 8
SSD_STATE = 128
SSD_CONV = 4
SSD_CONV_DIM = 3072
SSD_CHUNK = 128
NSA_HEADS = 16
NSA_KV = 4
NSA_GQ = 4
NSA_HD = 64
CMP_LEN = 32
CMP_STRIDE = 16
CMP_HIDDEN = 128
SEL_BLOCK = 64
SEL_TOPN = 16
SEL_FORCE = 1.0e4
WINDOW = 512
QUERY_BLOCK = 128
MEM_HEADS = 4
MEM_HD = 256
D_FF = 2816
FFN_CONV = 3
ROPE_THETA = 10000.0
EPS = 1e-6
IN_SIZES = (2048, 3072, 32, 1024, 1536, 48, 1024, 3072)

VMEM_LIMIT = 48 * 1024 * 1024
MASK_BIG = 2.0 ** 20
NEG = -1.0e30

F32 = jnp.float32
BF16 = jnp.bfloat16
HI = lax.Precision.HIGHEST


def _cparams(*sem):
    return pltpu.CompilerParams(dimension_semantics=sem, vmem_limit_bytes=VMEM_LIMIT)


def _sigmoid(x):
    return 1.0 / (1.0 + jnp.exp(-x))


def _silu(x):
    return x * _sigmoid(x)


def _dot(a, b):
    return jnp.dot(a, b, preferred_element_type=F32)


def _dot_nt(a, b):
    return lax.dot_general(a, b, (((1,), (1,)), ((), ())), preferred_element_type=F32)


def _dot_hi(a, b):
    return jnp.dot(a, b, preferred_element_type=F32, precision=HI)


def _split3(x):
    hi = x.astype(BF16)
    r1 = x - hi.astype(F32)
    mid = r1.astype(BF16)
    lo = (r1 - mid.astype(F32)).astype(BF16)
    return hi, mid, lo


def _dot_x3(a, b, split):
    if split == 0:
        parts = [_dot(t, b) for t in _split3(a)]
    else:
        parts = [_dot(a, t) for t in _split3(b)]
    return parts[0] + parts[1] + parts[2]


def _mm_kernel(x_ref, g_ref, w_ref, o_ref, xn_ref, *, norm):
    @pl.when(pl.program_id(1) == 0)
    def _():
        x = x_ref[...].astype(F32)
        if norm:
            ms = jnp.mean(x * x, axis=-1, keepdims=True)
            x = x * lax.rsqrt(ms + EPS) * g_ref[...]
        xn_ref[...] = x.astype(BF16)

    o_ref[...] = _dot(xn_ref[...], w_ref[...]).astype(o_ref.dtype)


def _matmul(x, w, g=None, out_dtype=F32, tm=512, tn=512):
    m, k = x.shape
    n = w.shape[1]
    tm = min(tm, m)
    tn = min(tn, n)
    assert m % tm == 0 and n % tn == 0, (m, n, tm, tn)
    norm = g is not None
    if g is None:
        g = jnp.ones((1, k), F32)
    return pl.pallas_call(
        functools.partial(_mm_kernel, norm=norm),
        grid=(m // tm, n // tn),
        in_specs=[pl.BlockSpec((tm, k), lambda i, j: (i, 0)),
                  pl.BlockSpec((1, k), lambda i, j: (0, 0)),
                  pl.BlockSpec((k, tn), lambda i, j: (0, j))],
        out_specs=pl.BlockSpec((tm, tn), lambda i, j: (i, j)),
        out_shape=jax.ShapeDtypeStruct((m, n), out_dtype),
        scratch_shapes=[pltpu.VMEM((tm, k), BF16)],
        compiler_params=_cparams("parallel", "arbitrary"),
        name="rms_matmul" if norm else "matmul",
    )(x, g.reshape(1, k).astype(F32), w)


def _shift_rows(x, k, prev, row):
    r = pltpu.roll(x, k, axis=0)
    for i in range(k):
        r = jnp.where(row == i, prev[8 - k + i:8 - k + i + 1, :], r)
    return r


def _ssd_prompt_kernel(xbc_ref, z_ref, dt_ref, dtT_ref, cw_ref, cb_ref, dtb_ref, dtbT_ref, a_ref, aT_ref,
                       dskip_ref, nw_ref, e_ref, y_ref, hT_ref, carry_ref, h_ref, yacc_ref):
    c = pl.program_id(1)
    q = SSD_CHUNK

    @pl.when(c == 0)
    def _():
        carry_ref[...] = jnp.zeros_like(carry_ref)
        h_ref[...] = jnp.zeros_like(h_ref)

    xbc = xbc_ref[0]
    row = lax.broadcasted_iota(jnp.int32, (q, 1), 0)
    prev = carry_ref[...]
    conv = cb_ref[...] + cw_ref[3:4, :] * xbc
    for k in range(1, SSD_CONV):
        conv = conv + cw_ref[3 - k:4 - k, :] * _shift_rows(xbc, k, prev, row)
    carry_ref[...] = xbc[q - 8:, :]
    xc = _silu(conv)
    xs = xc[:, :D_INNER]
    bm = xc[:, D_INNER:D_INNER + SSD_GROUPS * SSD_STATE].astype(BF16)
    cm = xc[:, D_INNER + SSD_GROUPS * SSD_STATE:].astype(BF16)

    def softplus(v):
        return jnp.maximum(v, 0.0) + jnp.log(1.0 + jnp.exp(-jnp.abs(v)))

    dt = softplus(dt_ref[0] + dtb_ref[...])
    dtT = softplus(dtT_ref[0] + dtbT_ref[...])
    ii = lax.broadcasted_iota(jnp.int32, (q, q), 0)
    jj = lax.broadcasted_iota(jnp.int32, (q, q), 1)
    causal = ii >= jj
    cum = _dot_x3(jnp.where(causal, 1.0, 0.0).astype(BF16), dt * a_ref[...], 1)
    cumT = _dot_x3(dtT * aT_ref[...], jnp.where(jj >= ii, 1.0, 0.0).astype(BF16), 0)
    ecum = jnp.exp(cum)
    dend = jnp.exp(cum[q - 1:q, :] - cum)
    spread = _dot_x3(jnp.concatenate([dt, ecum, dend], axis=0), e_ref[...], 0)
    dt_x, ecum_x, dend_x = spread[:q], spread[q:2 * q], spread[2 * q:]
    xdt = xs * dt_x
    xdt_b = xdt.astype(BF16)
    xw_b = (xdt * dend_x).astype(BF16)
    gw = SSD_HPG * SSD_HEAD_DIM
    for g in range(SSD_GROUPS):
        bg = bm[:, g * SSD_STATE:(g + 1) * SSD_STATE]
        cg = cm[:, g * SSD_STATE:(g + 1) * SSD_STATE]
        cb = _dot_nt(cg, bg)
        h_prev = h_ref[g]
        yoff = _dot(cg, h_prev.astype(BF16)) * ecum_x[:, g * gw:(g + 1) * gw]
        st = _dot(bg.astype(F32).T.astype(BF16), xw_b[:, g * gw:(g + 1) * gw])
        h_ref[g] = h_prev * ecum_x[q - 1:q, g * gw:(g + 1) * gw] + st
        for eh in range(SSD_HPG):
            hh = g * SSD_HPG + eh
            seg = cum[:, hh:hh + 1] - cumT[hh:hh + 1, :]
            decay = jnp.exp(jnp.where(causal, seg, NEG))
            mm = (cb * decay).astype(BF16)
            lo = hh * SSD_HEAD_DIM
            yd = _dot(mm, xdt_b[:, lo:lo + SSD_HEAD_DIM])
            yacc_ref[:, lo:lo + SSD_HEAD_DIM] = yd + yoff[:, eh * SSD_HEAD_DIM:(eh + 1) * SSD_HEAD_DIM]
    y = yacc_ref[...] + dskip_ref[...] * xs
    yz = y * _silu(z_ref[0])
    ms = jnp.mean(yz * yz, axis=-1, keepdims=True)
    y_ref[0] = (yz * lax.rsqrt(ms + EPS) * nw_ref[...]).astype(y_ref.dtype)

    @pl.when(c == pl.num_programs(1) - 1)
    def _():
        hT_ref[0] = h_ref[...]


def _head_expand(width=SSD_HEAD_DIM):
    e = np.zeros((SSD_HEADS, SSD_HEADS * width), np.float32)
    for h in range(SSD_HEADS):
        e[h, h * width:(h + 1) * width] = 1.0
    return jnp.asarray(e, BF16)


def _ssd_prompt(xbc, z, dt_raw, w):
    b, length, _ = xbc.shape
    q = SSD_CHUNK
    nc = length // q
    dtT = jnp.swapaxes(dt_raw, 1, 2)
    a = -jnp.exp(w['ssd_a_log'].astype(F32))
    full = lambda shape: pl.BlockSpec(shape, lambda i, j: (0,) * len(shape))
    y, hT = pl.pallas_call(
        _ssd_prompt_kernel,
        grid=(b, nc),
        in_specs=[pl.BlockSpec((1, q, SSD_CONV_DIM), lambda i, j: (i, j, 0)),
                  pl.BlockSpec((1, q, D_INNER), lambda i, j: (i, j, 0)),
                  pl.BlockSpec((1, q, SSD_HEADS), lambda i, j: (i, j, 0)),
                  pl.BlockSpec((1, SSD_HEADS, q), lambda i, j: (i, 0, j)),
                  full((SSD_CONV, SSD_CONV_DIM)), full((1, SSD_CONV_DIM)),
                  full((1, SSD_HEADS)), full((SSD_HEADS, 1)), full((1, SSD_HEADS)), full((SSD_HEADS, 1)),
                  full((1, D_INNER)), full((1, D_INNER)), full((SSD_HEADS, D_INNER))],
        out_specs=[pl.BlockSpec((1, q, D_INNER), lambda i, j: (i, j, 0)),
                   pl.BlockSpec((1, SSD_GROUPS, SSD_STATE, SSD_HPG * SSD_HEAD_DIM), lambda i, j: (i, 0, 0, 0))],
        out_shape=[jax.ShapeDtypeStruct((b, length, D_INNER), BF16),
                   jax.ShapeDtypeStruct((b, SSD_GROUPS, SSD_STATE, SSD_HPG * SSD_HEAD_DIM), F32)],
        scratch_shapes=[pltpu.VMEM((8, SSD_CONV_DIM), F32),
                        pltpu.VMEM((SSD_GROUPS, SSD_STATE, SSD_HPG * SSD_HEAD_DIM), F32),
                        pltpu.VMEM((q, D_INNER), F32)],
        compiler_params=_cparams("parallel", "arbitrary"),
        name="ssd_prompt",
    )(xbc, z, dt_raw, dtT, w['ssd_conv_w'], w['ssd_conv_b'].reshape(1, -1),
      w['ssd_dt_bias'].reshape(1, -1), w['ssd_dt_bias'].reshape(-1, 1), a.reshape(1, -1), a.reshape(-1, 1),
      jnp.repeat(w['ssd_d'].astype(F32), SSD_HEAD_DIM).reshape(1, -1), w['ssd_norm'].reshape(1, -1),
      _head_expand())
    state = hT.reshape(b, SSD_GROUPS, SSD_STATE, SSD_HPG, SSD_HEAD_DIM).transpose(0, 1, 3, 4, 2)
    return y, state.reshape(b, SSD_HEADS, SSD_HEAD_DIM, SSD_STATE)


def _segment_ones():
    i = np.arange(128)
    return jnp.asarray((i[:, None] // NSA_HD == i[None, :] // NSA_HD).astype(np.float32), BF16)


def _head_rmsnorm(x, g_row, segm):
    sq = x * x
    hi = sq.astype(BF16)
    lo = (sq - hi.astype(F32)).astype(BF16)
    parts = []
    for c in range(x.shape[1] // 128):
        sl = slice(c * 128, (c + 1) * 128)
        parts.append(_dot(hi[:, sl], segm) + _dot(lo[:, sl], segm))
    ss = parts[0] if len(parts) == 1 else jnp.concatenate(parts, axis=1)
    return x * lax.rsqrt(ss * (1.0 / NSA_HD) + EPS) * g_row


def _tile_lanes(t, width):
    reps = width // t.shape[1]
    return t if reps == 1 else jnp.concatenate([t] * reps, axis=1)


def _rope(x, cos128, sin128):
    width = x.shape[1]
    lane = lax.broadcasted_iota(jnp.int32, x.shape, 1)
    first = (lane & (NSA_HD // 2)) == 0
    rot = jnp.where(first, pltpu.roll(x, width - NSA_HD // 2, axis=1), pltpu.roll(x, NSA_HD // 2, axis=1))
    return x * _tile_lanes(cos128, width) + rot * _tile_lanes(sin128, width)


def _rope_tables(pos):
    half = NSA_HD // 2
    inv = ROPE_THETA ** (-jnp.arange(half, dtype=F32) / half)
    ang = pos.astype(F32)[:, None] * inv[None, :]
    cos, sin = jnp.cos(ang), jnp.sin(ang)
    cos128 = jnp.concatenate([cos, cos, cos, cos], axis=1)
    sin128 = jnp.concatenate([-sin, sin, -sin, sin], axis=1)
    return cos128, sin128


def _nsa_prep_kernel(q_ref, kv_ref, cos_ref, sin_ref, qg_ref, kg_ref, segm_ref,
                     qh_ref, skaug_ref, skf_ref, svh_ref, wkh_ref, wkf_ref, wvh_ref):
    tr = q_ref.shape[0]
    i = pl.program_id(1)
    cos, sin, segm = cos_ref[...], sin_ref[...], segm_ref[...]
    kvw = NSA_KV * NSA_HD
    q = _rope(_head_rmsnorm(q_ref[...], qg_ref[...], segm), cos, sin) * (NSA_HD ** -0.5)
    for h in range(NSA_HEADS):
        qh_ref[0, h] = q[:, h * NSA_HD:(h + 1) * NSA_HD].astype(BF16)
    sk = _rope(_head_rmsnorm(kv_ref[:, 2 * kvw:3 * kvw], kg_ref[1:2, :], segm), cos, sin)
    wk = _rope(_head_rmsnorm(kv_ref[:, 4 * kvw:5 * kvw], kg_ref[2:3, :], segm), cos, sin)
    skf_ref[...] = sk
    wkf_ref[...] = wk
    sv = kv_ref[:, 3 * kvw:4 * kvw]
    wv = kv_ref[:, 5 * kvw:6 * kvw]
    pos = i * tr + lax.broadcasted_iota(jnp.int32, (tr, NSA_HD), 0)
    blk = lax.broadcasted_iota(jnp.int32, (tr, NSA_HD), 1)
    onehot = jnp.where((pos >> 6) == blk, MASK_BIG, 0.0).astype(BF16)
    for k in range(NSA_KV):
        sl = slice(k * NSA_HD, (k + 1) * NSA_HD)
        skaug_ref[0, k] = jnp.concatenate([sk[:, sl].astype(BF16), onehot], axis=1)
        svh_ref[0, k] = sv[:, sl].astype(BF16)
        wkh_ref[0, k] = wk[:, sl].astype(BF16)
        wvh_ref[0, k] = wv[:, sl].astype(BF16)


def _nsa_prep(q, kv, b, length, w, tr=256):
    n = b * length
    nl = length // tr
    cos128, sin128 = _rope_tables(jnp.arange(length, dtype=jnp.int32))
    kvw = NSA_KV * NSA_HD
    row = lambda wd: pl.BlockSpec((tr, wd), lambda bi, i: (bi * nl + i, 0))
    tab = pl.BlockSpec((tr, 128), lambda bi, i: (i, 0))
    full = lambda shape: pl.BlockSpec(shape, lambda bi, i: (0,) * len(shape))
    hm = lambda nh, wd: pl.BlockSpec((1, nh, tr, wd), lambda bi, i: (bi, 0, i, 0))
    return pl.pallas_call(
        _nsa_prep_kernel,
        grid=(b, nl),
        in_specs=[row(NSA_HEADS * NSA_HD), row(6 * kvw), tab, tab, full((1, NSA_HEADS * NSA_HD)), full((3, kvw)),
                  full((128, 128))],
        out_specs=[hm(NSA_HEADS, NSA_HD), hm(NSA_KV, 2 * NSA_HD), row(kvw), hm(NSA_KV, NSA_HD), hm(NSA_KV, NSA_HD),
                   row(kvw), hm(NSA_KV, NSA_HD)],
        out_shape=[jax.ShapeDtypeStruct((b, NSA_HEADS, length, NSA_HD), BF16),
                   jax.ShapeDtypeStruct((b, NSA_KV, length, 2 * NSA_HD), BF16),
                   jax.ShapeDtypeStruct((n, kvw), F32),
                   jax.ShapeDtypeStruct((b, NSA_KV, length, NSA_HD), BF16),
                   jax.ShapeDtypeStruct((b, NSA_KV, length, NSA_HD), BF16),
                   jax.ShapeDtypeStruct((n, kvw), F32),
                   jax.ShapeDtypeStruct((b, NSA_KV, length, NSA_HD), BF16)],
        compiler_params=_cparams("parallel", "parallel"),
        name="nsa_prep",
    )(q, kv, cos128, sin128, jnp.tile(w['nsa_q_norm'], NSA_HEADS).reshape(1, -1),
      jnp.tile(w['nsa_k_norm'], (1, NSA_KV)), _segment_ones())


def _compress_kernel(*refs, n_x, is_k, n_prefetch, from_pages):
    refs = refs[n_prefetch:]
    if from_pages:
        page_refs = refs[:n_x]
        wbd_ref, pe_ref, w2_ref, g_ref, cos_ref, sin_ref, segm_ref, o_ref, x0_ref, x1_ref = refs[n_x:]
        plen = page_refs[0].shape[-1]
        for j, r in enumerate(page_refs):
            rows = r[...].T
            x0_ref[j * plen:(j + 1) * plen, :] = rows[:, :128]
            x1_ref[j * plen:(j + 1) * plen, :] = rows[:, 128:]
        x_refs = ((x0_ref,), (x1_ref,))
        n_x = 1
    else:
        x_refs = (refs[:n_x], refs[n_x:2 * n_x])
        wbd_ref, pe_ref, w2_ref, g_ref, cos_ref, sin_ref, segm_ref, o_ref = refs[2 * n_x:]
    per = x_refs[0][0].shape[-2] // CMP_STRIDE
    nchunk = per * n_x
    acc = [jnp.zeros((nchunk, 4 * CMP_HIDDEN), F32) for _ in range(2)]
    for s in range(CMP_STRIDE):
        for p in range(2):
            xs = [r[pl.ds(s, per, stride=CMP_STRIDE), :] for r in x_refs[p]]
            x = (xs[0] if n_x == 1 else jnp.concatenate(xs, axis=0)).astype(BF16)
            acc[p] = acc[p] + _dot(x, wbd_ref[s])
    row = lax.broadcasted_iota(jnp.int32, (nchunk, 1), 0)
    outs = []
    for p in range(2):
        hid = []
        for kl in range(2):
            lo = kl * 2 * CMP_HIDDEN
            first = acc[p][:, lo:lo + CMP_HIDDEN]
            second = pltpu.roll(acc[p][:, lo + CMP_HIDDEN:lo + 2 * CMP_HIDDEN], nchunk - 1, axis=0)
            hid.append(_silu(first + second + pe_ref[...]))
        outs.append(_dot(jnp.concatenate(hid, axis=1).astype(BF16), w2_ref[...]))
    out = jnp.concatenate(outs, axis=1)
    if is_k:
        out = _rope(_head_rmsnorm(out, g_ref[...], segm_ref[...]), cos_ref[...], sin_ref[...])
    o_ref[0] = jnp.where(row < nchunk - 1, out, 0.0)


def _compress_weights(pe, w1, w2):
    w1r = w1.reshape(CMP_LEN, NSA_HD, CMP_HIDDEN)
    pe_term = jnp.einsum('ld,ldm->m', pe, w1r, precision=HI).reshape(1, CMP_HIDDEN)
    both = jnp.concatenate([w1r[:CMP_STRIDE], w1r[CMP_STRIDE:]], axis=-1)
    zero = jnp.zeros_like(both)
    wbd = jnp.concatenate([jnp.concatenate([both, zero], axis=-1), jnp.concatenate([zero, both], axis=-1)], axis=1)
    zero2 = jnp.zeros_like(w2)
    w2bd = jnp.concatenate([jnp.concatenate([w2, zero2], axis=-1), jnp.concatenate([zero2, w2], axis=-1)], axis=0)
    return wbd.astype(BF16), pe_term, w2bd.astype(BF16)


def _compress_call(x_args, x_specs, grid, nchunk, batch, pe, w1, w2, knorm, is_k, n_prefetch=0, prefetch=(),
                   from_pages=False):
    wbd, pe_term, w2bd = _compress_weights(pe, w1, w2)
    cos128, sin128 = _rope_tables(jnp.arange(nchunk, dtype=jnp.int32) * CMP_STRIDE + (CMP_LEN - 1))
    kvw = NSA_KV * NSA_HD
    full = lambda shape: pl.BlockSpec(shape, lambda *a: (0,) * len(shape))
    gs = pltpu.PrefetchScalarGridSpec(
        num_scalar_prefetch=n_prefetch, grid=grid,
        in_specs=list(x_specs) + [full(wbd.shape), full((1, CMP_HIDDEN)), full(w2bd.shape), full((1, kvw)),
                                  full((nchunk, 128)), full((nchunk, 128)), full((128, 128))],
        out_specs=pl.BlockSpec((1, nchunk, kvw), lambda i, *a: (i, 0, 0)),
        scratch_shapes=[pltpu.VMEM((nchunk * CMP_STRIDE, 128), F32)] * 2 if from_pages else [])
    return pl.pallas_call(
        functools.partial(_compress_kernel, n_x=len(x_specs) if from_pages else len(x_specs) // 2, is_k=is_k,
                          n_prefetch=n_prefetch, from_pages=from_pages),
        grid_spec=gs,
        out_shape=jax.ShapeDtypeStruct((batch, nchunk, kvw), F32),
        compiler_params=_cparams("parallel"),
        name="compress_k" if is_k else "compress_v",
    )(*prefetch, *x_args, wbd, pe_term, w2bd, jnp.tile(knorm, NSA_KV).reshape(1, -1), cos128, sin128,
      _segment_ones())


def _compress_prompt(kv, col, b, length, pe, w1, w2, knorm, is_k):
    specs = [pl.BlockSpec((length, 128), functools.partial(lambda i, c: (i, c), c=2 * col + p)) for p in range(2)]
    return _compress_call([kv, kv], specs, (b,), length // CMP_STRIDE, b, pe, w1, w2, knorm, is_k)


SEL_TILE = 512
WIN_TILE = 128


def _flash_step(carry, s, v):
    m, l, acc = carry
    m_new = jnp.maximum(m, jnp.max(s, axis=-1, keepdims=True))
    alpha = jnp.exp(m - m_new)
    p = jnp.exp(s - m_new)
    l = alpha * l + jnp.sum(p, axis=-1, keepdims=True)
    acc = alpha * acc + _dot(p.astype(BF16), v)
    return m_new, l, acc


def _topn_bias(score, n):
    nblk, nq = score.shape
    groups = [score[8 * v:8 * v + 8] for v in range(nblk // 8)]
    sub = lax.broadcasted_iota(jnp.int32, (8, nq), 0)
    cnt = [jnp.zeros((8, nq), F32) for _ in groups]
    for jp in range(nblk):
        row = score[jp:jp + 1]
        for v, grp in enumerate(groups):
            if v < jp // 8:
                inc = jnp.where(row > grp, 1.0, 0.0)
            elif v > jp // 8:
                inc = jnp.where(row >= grp, 1.0, 0.0)
            else:
                inc = jnp.where(sub > jp % 8, jnp.where(row >= grp, 1.0, 0.0), jnp.where(row > grp, 1.0, 0.0))
            cnt[v] = cnt[v] + inc
    return jnp.concatenate([jnp.where(c < n, 0.0, -1.0) for c in cnt], axis=0)


def _nsa_prompt_kernel(q_ref, kc_ref, vc_ref, sk_ref, sv_ref, wk_ref, wv_ref, g_ref, ov_ref, o_ref):
    qb = pl.program_id(2)
    qlen = QUERY_BLOCK
    rows = NSA_GQ * qlen
    t0 = qb * qlen
    q4 = q_ref[0].reshape(rows, NSA_HD)
    trow = t0 + (lax.broadcasted_iota(jnp.int32, (rows, 1), 0) & (qlen - 1))
    ncmp = kc_ref.shape[2]
    s = _dot_nt(q4, kc_ref[0, 0])
    kend = lax.broadcasted_iota(jnp.int32, (1, ncmp), 1) * CMP_STRIDE + (CMP_LEN - 1)
    valid = kend <= trow
    s = jnp.where(valid, s, NEG)
    m = jnp.max(s, axis=-1, keepdims=True)
    p = jnp.where(valid, jnp.exp(s - m), 0.0)
    den = jnp.sum(p, axis=-1, keepdims=True)
    p = p / jnp.where(den > 0, den, 1.0)
    o_c = _dot(p.astype(BF16), vc_ref[0, 0])
    psum = p[0:qlen] + p[qlen:2 * qlen] + p[2 * qlen:3 * qlen] + p[3 * qlen:4 * qlen]
    nblk = ov_ref.shape[0]
    imp = lax.dot_general(ov_ref[...], psum, (((1,), (1,)), ((), ())), preferred_element_type=F32,
                          precision=HI)
    jb = lax.broadcasted_iota(jnp.int32, (nblk, 1), 0)
    t = t0 + lax.broadcasted_iota(jnp.int32, (1, qlen), 1)
    cur = t >> 6
    forced = (jb == 0) | (jb == cur) | (jb == cur - 1)
    score = jnp.where(jb * SEL_BLOCK <= t, imp + jnp.where(forced, SEL_FORCE, 0.0), NEG)
    bias = _topn_bias(score, min(SEL_TOPN, nblk)).T.astype(BF16)
    qaug = jnp.concatenate([q4, jnp.concatenate([bias] * NSA_GQ, axis=0)], axis=1)
    init = (jnp.full((rows, 1), NEG, F32), jnp.zeros((rows, 1), F32), jnp.zeros((rows, NSA_HD), F32))

    def sel_body(kt, carry):
        k0 = pl.multiple_of(kt * SEL_TILE, SEL_TILE)
        sc = _dot_nt(qaug, sk_ref[0, 0, pl.ds(k0, SEL_TILE), :])
        return _flash_step(carry, sc, sv_ref[0, 0, pl.ds(k0, SEL_TILE), :])

    n_full = t0 // SEL_TILE
    carry = lax.fori_loop(0, n_full, sel_body, init)
    k0 = pl.multiple_of(n_full * SEL_TILE, SEL_TILE)
    kpos = k0 + lax.broadcasted_iota(jnp.int32, (1, SEL_TILE), 1)
    sc = jnp.where(kpos <= trow, _dot_nt(qaug, sk_ref[0, 0, pl.ds(k0, SEL_TILE), :]), NEG)
    _, l_s, acc_s = _flash_step(carry, sc, sv_ref[0, 0, pl.ds(k0, SEL_TILE), :])
    o_s = acc_s / l_s

    span = min(WINDOW + qlen, wk_ref.shape[2])
    kw0 = pl.multiple_of(jnp.maximum(t0 - WINDOW, 0), qlen)
    kpos = kw0 + lax.broadcasted_iota(jnp.int32, (1, span), 1)
    s_w = _dot_nt(q4, wk_ref[0, 0, pl.ds(kw0, span), :])
    s_w = jnp.where(kpos <= trow, jnp.where(kpos >= trow - WINDOW, s_w, NEG), NEG)
    p_w = jnp.exp(s_w - jnp.max(s_w, axis=-1, keepdims=True))
    o_w = _dot(p_w.astype(BF16), wv_ref[0, 0, pl.ds(kw0, span), :]) / jnp.sum(p_w, axis=-1, keepdims=True)

    gate = _sigmoid(g_ref[0, 0])
    for g in range(NSA_GQ):
        r = slice(g * qlen, (g + 1) * qlen)
        og = (gate[:, 3 * g:3 * g + 1] * o_c[r] + gate[:, 3 * g + 1:3 * g + 2] * o_s[r]
              + gate[:, 3 * g + 2:3 * g + 3] * o_w[r])
        o_ref[:, g * NSA_HD:(g + 1) * NSA_HD] = og.astype(o_ref.dtype)


def _overlap_matrix(ncmp, nblk):
    ci = np.arange(ncmp)[:, None] * CMP_STRIDE
    sj = np.arange(nblk)[None, :] * SEL_BLOCK
    return jnp.asarray(((ci <= sj + SEL_BLOCK - 1) & (ci + CMP_LEN - 1 >= sj)).astype(np.float32))


def _nsa_prompt(qh, kc, vc, skaug, svh, wkh, wvh, gates, b, length):
    nb = length // QUERY_BLOCK
    ncmp = kc.shape[2]
    nblk = NSA_HD
    assert length // SEL_BLOCK <= nblk
    seq = lambda wd: pl.BlockSpec((1, 1, length, wd), lambda bi, k, i: (bi, k, 0, 0))
    cmp_spec = pl.BlockSpec((1, 1, ncmp, NSA_HD), lambda bi, k, i: (bi, k, 0, 0))
    return pl.pallas_call(
        _nsa_prompt_kernel,
        grid=(b, NSA_KV, nb),
        in_specs=[pl.BlockSpec((1, NSA_GQ, QUERY_BLOCK, NSA_HD), lambda bi, k, i: (bi, k, i, 0)),
                  cmp_spec, cmp_spec, seq(2 * NSA_HD), seq(NSA_HD), seq(NSA_HD), seq(NSA_HD),
                  pl.BlockSpec((1, 1, QUERY_BLOCK, 3 * NSA_GQ), lambda bi, k, i: (bi, k, i, 0)),
                  pl.BlockSpec((nblk, ncmp), lambda bi, k, i: (0, 0))],
        out_specs=pl.BlockSpec((QUERY_BLOCK, NSA_GQ * NSA_HD), lambda bi, k, i: (bi * nb + i, k)),
        out_shape=jax.ShapeDtypeStruct((b * length, NSA_HEADS * NSA_HD), BF16),
        compiler_params=_cparams("parallel", "parallel", "arbitrary"),
        name="nsa_prompt",
    )(qh, kc, vc, skaug, svh, wkh, wvh, gates, _overlap_matrix(ncmp, nblk).T)


def _mem_prompt_kernel(q_ref, mk_ref, mv_ref, g_ref, o_ref):
    for h in range(MEM_HEADS):
        sl = slice(h * MEM_HD, (h + 1) * MEM_HD)
        q = q_ref[:, sl]
        ms = jnp.mean(q * q, axis=-1, keepdims=True)
        qn = (q * lax.rsqrt(ms + EPS) * g_ref[...] * (MEM_HD ** -0.5)).astype(BF16)
        s = _dot_nt(qn, mk_ref[0, :, sl])
        m = jnp.max(s, axis=-1, keepdims=True)
        p = jnp.exp(s - m)
        p = p / jnp.sum(p, axis=-1, keepdims=True)
        o_ref[:, sl] = _dot(p.astype(BF16), mv_ref[0, :, sl]).astype(o_ref.dtype)


def _mem_prompt(q_mem, mk, mv, qnorm, b, length, tq=256):
    nl = length // tq
    width = MEM_HEADS * MEM_HD
    mem = pl.BlockSpec((1, mk.shape[1], width), lambda bi, i: (bi, 0, 0))
    return pl.pallas_call(
        _mem_prompt_kernel,
        grid=(b, nl),
        in_specs=[pl.BlockSpec((tq, width), lambda bi, i: (bi * nl + i, 0)), mem, mem,
                  pl.BlockSpec((1, MEM_HD), lambda bi, i: (0, 0))],
        out_specs=pl.BlockSpec((tq, width), lambda bi, i: (bi * nl + i, 0)),
        out_shape=jax.ShapeDtypeStruct((b * length, width), BF16),
        compiler_params=_cparams("parallel", "parallel"),
        name="mem_prompt",
    )(q_mem, mk, mv, qnorm.reshape(1, -1))


def _merge_kernel(x_ref, ys_ref, yn_ref, ym_ref, gm_ref, bm_ref, ws_ref, wn_ref, wm_ref, wo_ref, o_ref):
    gate = _sigmoid(gm_ref[...] + bm_ref[...])
    mixed = (gate[:, :D_MODEL] * _dot(ys_ref[...], ws_ref[...])
             + gate[:, D_MODEL:2 * D_MODEL] * _dot(yn_ref[...], wn_ref[...])
             + gate[:, 2 * D_MODEL:] * _dot(ym_ref[...], wm_ref[...]))
    o_ref[...] = x_ref[...] + _dot(mixed.astype(BF16), wo_ref[...])


def _merge(x, ys, yn, ym, gm, wb, tm=256):
    m = x.shape[0]
    tm = min(tm, m)
    row = lambda wd: pl.BlockSpec((tm, wd), lambda i: (i, 0))
    full = lambda shape: pl.BlockSpec(shape, lambda i: (0,) * len(shape))
    return pl.pallas_call(
        _merge_kernel,
        grid=(m // tm,),
        in_specs=[row(D_MODEL), row(D_INNER), row(D_MODEL), row(D_MODEL), row(3 * D_MODEL), full((1, 3 * D_MODEL)),
                  full((D_INNER, D_MODEL)), full((D_MODEL, D_MODEL)), full((D_MODEL, D_MODEL)),
                  full((D_MODEL, D_MODEL))],
        out_specs=row(D_MODEL),
        out_shape=jax.ShapeDtypeStruct((m, D_MODEL), F32),
        compiler_params=_cparams("parallel"),
        name="merge",
    )(x, ys, yn, ym, gm, wb['b_merge'].reshape(1, -1), wb['w_ssd_o'], wb['w_nsa_o'], wb['w_mem_o'], wb['w_out'])


def _ffn_down_kernel(*refs, seq):
    if seq:
        u_ref, x_ref, cw_ref, cb_ref, wd_ref, o_ref, carry_ref = refs
    else:
        u_ref, p0_ref, p1_ref, x_ref, cw_ref, cb_ref, wd_ref, o_ref = refs
    u = u_ref[...]
    if seq:
        @pl.when(pl.program_id(1) == 0)
        def _():
            carry_ref[...] = jnp.zeros_like(carry_ref)

        row = lax.broadcasted_iota(jnp.int32, (u.shape[0], 1), 0)
        prev = carry_ref[...]
        u1 = _shift_rows(u, 1, prev, row)
        u2 = _shift_rows(u, 2, prev, row)
        carry_ref[...] = u[u.shape[0] - 8:, :]
    else:
        u2, u1 = p0_ref[...], p1_ref[...]
    conv = cb_ref[...] + cw_ref[0:1, :] * u2 + cw_ref[1:2, :] * u1 + cw_ref[2:3, :] * u
    act = (_silu(conv[:, :D_FF]) * conv[:, D_FF:]).astype(BF16)
    o_ref[...] = x_ref[...] + _dot(act, wd_ref[...])


def _ffn_down_prompt(u, x1, wb, b, length, tm=256):
    nl = length // tm
    row = lambda wd: pl.BlockSpec((tm, wd), lambda bi, i: (bi * nl + i, 0))
    full = lambda shape: pl.BlockSpec(shape, lambda bi, i: (0,) * len(shape))
    return pl.pallas_call(
        functools.partial(_ffn_down_kernel, seq=True),
        grid=(b, nl),
        in_specs=[row(2 * D_FF), row(D_MODEL), full((FFN_CONV, 2 * D_FF)), full((1, 2 * D_FF)),
                  full((D_FF, D_MODEL))],
        out_specs=row(D_MODEL),
        out_shape=jax.ShapeDtypeStruct((b * length, D_MODEL), F32),
        scratch_shapes=[pltpu.VMEM((8, 2 * D_FF), F32)],
        compiler_params=_cparams("parallel", "arbitrary"),
        name="ffn_down_prompt",
    )(u, x1, wb['ffn_conv_w'], wb['ffn_conv_b'].reshape(1, -1), wb['w_down'])


def _ffn_down_sample(u, prev, x1, wb):
    m = u.shape[0]
    full = lambda shape: pl.BlockSpec(shape, lambda i: (0,) * len(shape))
    return pl.pallas_call(
        functools.partial(_ffn_down_kernel, seq=False),
        grid=(1,),
        in_specs=[full((m, 2 * D_FF)), full((m, 2 * D_FF)), full((m, 2 * D_FF)), full((m, D_MODEL)),
                  full((FFN_CONV, 2 * D_FF)), full((1, 2 * D_FF)), full((D_FF, D_MODEL))],
        out_specs=full((m, D_MODEL)),
        out_shape=jax.ShapeDtypeStruct((m, D_MODEL), F32),
        compiler_params=_cparams("arbitrary"),
        name="ffn_down_sample",
    )(u, prev[:, 0], prev[:, 1], x1, wb['ffn_conv_w'], wb['ffn_conv_b'].reshape(1, -1), wb['w_down'])


def _compress_sample(pool, page_table, pe, w1, w2, knorm, is_k):
    bs, npages = page_table.shape
    page = pool.shape[1]
    pool_t = _pool_t(pool)
    specs = [pl.BlockSpec((None, NSA_KV * NSA_HD, page), functools.partial(lambda b, pt, j: (pt[b, j], 0, 0), j=j))
             for j in range(npages)]
    return _compress_call([pool_t] * npages, specs, (bs,), npages * page // CMP_STRIDE, bs, pe, w1, w2, knorm,
                          is_k, n_prefetch=1, prefetch=(page_table,), from_pages=True)


def _pool_t(pool):
    n, page = pool.shape[:2]
    return pool.transpose(0, 2, 3, 1).reshape(n, NSA_KV * NSA_HD, page)


def _rows8(x):
    return jnp.broadcast_to(x, (8, x.shape[1]))


def _ssd_step_kernel(xbc_ref, prev_ref, z_ref, dt_ref, h_ref, cw_ref, cb_ref, dtb_ref, a_ref, dskip_ref, nw_ref,
                     e64_ref, e128_ref, y_ref, hn_ref):
    conv = cb_ref[...] + cw_ref[3:4, :] * xbc_ref[0]
    for k in range(SSD_CONV - 1):
        conv = conv + cw_ref[k:k + 1, :] * prev_ref[0, k:k + 1, :]
    xc = _silu(conv)
    xs = xc[:, :D_INNER]
    nb = SSD_GROUPS * SSD_STATE
    bm = xc[:, D_INNER:D_INNER + nb]
    cm = xc[:, D_INNER + nb:]
    v = dt_ref[0] + dtb_ref[...]
    dt = jnp.maximum(v, 0.0) + jnp.log(1.0 + jnp.exp(-jnp.abs(v)))
    dec = jnp.exp(dt * a_ref[...])
    xdt = xs * _dot_x3(_rows8(dt), e64_ref[...], 0)[0:1]
    dec128 = _dot_x3(_rows8(dec), e128_ref[...], 0)
    pieces = []
    for c in range(D_INNER // 128):
        g = c // (SSD_HPG // 2)
        xcol = jnp.broadcast_to(xdt[:, c * 128:(c + 1) * 128], (128, 128)).T
        bg = jnp.broadcast_to(bm[:, g * SSD_STATE:(g + 1) * SSD_STATE], (128, SSD_STATE))
        decv = jnp.concatenate(
            [jnp.concatenate([dec128[:, hh * 128:(hh + 1) * 128]] * (SSD_HEAD_DIM // 8), axis=0)
             for hh in (2 * c, 2 * c + 1)], axis=0)
        hnew = decv * h_ref[0, c * 128:(c + 1) * 128, :] + xcol * bg
        hn_ref[0, c * 128:(c + 1) * 128, :] = hnew
        cg = _rows8(cm[:, g * SSD_STATE:(g + 1) * SSD_STATE]).astype(BF16)
        pieces.append(_dot_nt(cg, hnew.astype(BF16))[0:1])
    y = jnp.concatenate(pieces, axis=1) + dskip_ref[...] * xs
    yz = y * _silu(z_ref[0])
    ms = jnp.mean(yz * yz, axis=-1, keepdims=True)
    y_ref[0] = (yz * lax.rsqrt(ms + EPS) * nw_ref[...]).astype(y_ref.dtype)


def _ssd_step(xbc, prev, z, dt_raw, h0, w):
    bs = xbc.shape[0]
    a = -jnp.exp(w['ssd_a_log'].astype(F32))
    row = lambda wd: pl.BlockSpec((1, 1, wd), lambda i: (i, 0, 0))
    full = lambda shape: pl.BlockSpec(shape, lambda i: (0,) * len(shape))
    st = pl.BlockSpec((1, D_INNER, SSD_STATE), lambda i: (i, 0, 0))
    y, hn = pl.pallas_call(
        _ssd_step_kernel,
        grid=(bs,),
        in_specs=[row(SSD_CONV_DIM), pl.BlockSpec((1, SSD_CONV - 1, SSD_CONV_DIM), lambda i: (i, 0, 0)),
                  row(D_INNER), row(SSD_HEADS), st,
                  full((SSD_CONV, SSD_CONV_DIM)), full((1, SSD_CONV_DIM)), full((1, SSD_HEADS)), full((1, SSD_HEADS)),
                  full((1, D_INNER)), full((1, D_INNER)), full((SSD_HEADS, D_INNER)),
                  full((SSD_HEADS, SSD_HEADS * 128))],
        out_specs=[row(D_INNER), st],
        out_shape=[jax.ShapeDtypeStruct((bs, 1, D_INNER), BF16),
                   jax.ShapeDtypeStruct((bs, D_INNER, SSD_STATE), F32)],
        compiler_params=_cparams("parallel"),
        name="ssd_step",
    )(xbc.reshape(bs, 1, -1), prev, z.reshape(bs, 1, -1), dt_raw.reshape(bs, 1, -1),
      h0.reshape(bs, D_INNER, SSD_STATE), w['ssd_conv_w'], w['ssd_conv_b'].reshape(1, -1),
      w['ssd_dt_bias'].reshape(1, -1), a.reshape(1, -1),
      jnp.repeat(w['ssd_d'].astype(F32), SSD_HEAD_DIM).reshape(1, -1), w['ssd_norm'].reshape(1, -1),
      _head_expand(), _head_expand(128))
    return y.reshape(bs, D_INNER), hn.reshape(bs, SSD_HEADS, SSD_HEAD_DIM, SSD_STATE)


def _softmax_with_extra(s, s_new, valid=None):
    if valid is not None:
        s = jnp.where(valid, s, NEG)
    m = jnp.maximum(jnp.max(s, axis=-1, keepdims=True), s_new)
    p = jnp.exp(s - m)
    if valid is not None:
        p = jnp.where(valid, p, 0.0)
    p_new = jnp.exp(s_new - m)
    inv = 1.0 / (jnp.sum(p, axis=-1, keepdims=True) + p_new)
    return p * inv, p_new * inv


def _nsa_sample_kernel(*refs, npages, past_len):
    pt_ref = refs[0]
    del pt_ref
    q_ref, kv_ref, gate_ref = refs[1:4]
    skp = refs[4:4 + npages]
    svp = refs[4 + npages:4 + 2 * npages]
    (kc_ref, vc_ref, wk_ref, wv_ref, qg_ref, kg_ref, cos_ref, sin_ref, segm_ref, ov_ref, eblk_ref, fold_ref,
     foldt_ref, o_ref, skn_ref, wkn_ref, wko_ref, wvo_ref) = refs[4 + 2 * npages:]
    kvw = NSA_KV * NSA_HD
    cos, sin, segm = cos_ref[...], sin_ref[...], segm_ref[...]
    q = _rope(_head_rmsnorm(_rows8(q_ref[0]), qg_ref[...], segm), _rows8(cos), _rows8(sin)) * (NSA_HD ** -0.5)
    kv = kv_ref[0]
    sk_new = _rope(_head_rmsnorm(_rows8(kv[:, 2 * kvw:3 * kvw]), kg_ref[1:2, :], segm), _rows8(cos), _rows8(sin))[0:1]
    wk_new = _rope(_head_rmsnorm(_rows8(kv[:, 4 * kvw:5 * kvw]), kg_ref[2:3, :], segm), _rows8(cos), _rows8(sin))[0:1]
    sv_new = kv[:, 3 * kvw:4 * kvw]
    wv_new = kv[:, 5 * kvw:6 * kvw]
    skn_ref[0] = sk_new
    wkn_ref[0] = wk_new
    hrow = lax.broadcasted_iota(jnp.int32, (NSA_HEADS, NSA_HEADS * NSA_HD), 0)
    hcol = lax.broadcasted_iota(jnp.int32, (NSA_HEADS, NSA_HEADS * NSA_HD), 1)
    own = (hcol >> 6) == hrow
    q16 = jnp.where(own, jnp.concatenate([q, q], axis=0), 0.0).astype(BF16)
    qbd = _dot(q16, fold_ref[...])
    qbd_b = qbd.astype(BF16)
    ncmp = kc_ref.shape[1]
    kend = lax.broadcasted_iota(jnp.int32, (1, ncmp), 1) * CMP_STRIDE + (CMP_LEN - 1)
    valid_c = kend <= past_len
    s_c = jnp.where(valid_c, _dot_nt(qbd_b, kc_ref[0].astype(BF16)), NEG)
    m_c = jnp.max(s_c, axis=-1, keepdims=True)
    p_c = jnp.where(valid_c, jnp.exp(s_c - m_c), 0.0)
    den = jnp.sum(p_c, axis=-1, keepdims=True)
    p_c = p_c / jnp.where(den > 0, den, 1.0)
    o_c = _dot(p_c.astype(BF16), vc_ref[0].astype(BF16))
    gi = lax.broadcasted_iota(jnp.int32, (NSA_HEADS, NSA_HEADS), 0) // NSA_GQ
    gj = lax.broadcasted_iota(jnp.int32, (NSA_HEADS, NSA_HEADS), 1) // NSA_GQ
    psum = _dot_hi(jnp.where(gi == gj, 1.0, 0.0), p_c)
    psum = jnp.concatenate([psum, jnp.zeros((128 - NSA_HEADS, ncmp), F32)], axis=0)
    imp = lax.dot_general(ov_ref[...], psum, (((1,), (1,)), ((), ())), preferred_element_type=F32,
                          precision=HI)
    nslot = ov_ref.shape[0]
    jb = lax.broadcasted_iota(jnp.int32, (nslot, 1), 0)
    cur = past_len // SEL_BLOCK
    forced = (jb == 0) | (jb == cur) | (jb == cur - 1)
    score = jnp.where(jb * SEL_BLOCK <= past_len, imp + jnp.where(forced, SEL_FORCE, 0.0), NEG)
    bias = _topn_bias(score, SEL_TOPN).T[:NSA_HEADS]
    kmask = _dot(bias.astype(BF16), eblk_ref[...])
    sk_all = jnp.concatenate([r[...] for r in skp], axis=1).astype(BF16)
    sv_all = jnp.concatenate([r[...] for r in svp], axis=1).astype(BF16)
    s_s = _dot(qbd_b, sk_all) + kmask
    s_new = jnp.sum(qbd * sk_new, axis=-1, keepdims=True)
    p_s, p_new = _softmax_with_extra(s_s, s_new)
    o_s = _dot_nt(p_s.astype(BF16), sv_all) + p_new * sv_new
    wk = wk_ref[0]
    wv = wv_ref[0]
    s_w = _dot(qbd_b, wk.astype(BF16))
    s_wn = jnp.sum(qbd * wk_new, axis=-1, keepdims=True)
    p_w, p_wn = _softmax_with_extra(s_w, s_wn)
    o_w = _dot_nt(p_w.astype(BF16), wv.astype(BF16)) + p_wn * wv_new
    g16 = jnp.where((lax.broadcasted_iota(jnp.int32, (NSA_HEADS, 3 * NSA_HEADS), 1) // 3)
                    == lax.broadcasted_iota(jnp.int32, (NSA_HEADS, 3 * NSA_HEADS), 0),
                    jnp.broadcast_to(_sigmoid(gate_ref[0]), (NSA_HEADS, 3 * NSA_HEADS)), 0.0)
    br = lax.broadcasted_iota(jnp.int32, (NSA_HEADS, 3 * NSA_HEADS), 1) % 3
    gsel = lambda r: jnp.sum(jnp.where(br == r, g16, 0.0), axis=-1, keepdims=True)
    o16 = gsel(0) * o_c + gsel(1) * o_s + gsel(2) * o_w
    ox = _dot(o16.astype(BF16), foldt_ref[...])
    o_ref[0] = jnp.sum(jnp.where(own, ox, 0.0), axis=0, keepdims=True).astype(o_ref.dtype)
    nwin = wk.shape[1]
    lane = lax.broadcasted_iota(jnp.int32, (kvw, nwin), 1)

    def column(rowvec):
        col = jnp.broadcast_to(rowvec, (128, kvw)).T
        return jnp.concatenate([col] * (nwin // 128), axis=1)

    wko_ref[0] = jnp.where(lane == nwin - 1, column(wk_new), pltpu.roll(wk, nwin - 1, axis=1))
    wvo_ref[0] = jnp.where(lane == nwin - 1, column(wv_new), pltpu.roll(wv, nwin - 1, axis=1))


def _nsa_sample(q, kv, gates, pool_sk, pool_sv, page_table, kc, vc, win_k, win_v, w):
    bs, npages = page_table.shape
    page = pool_sk.shape[1]
    past_len = npages * page
    kvw = NSA_KV * NSA_HD
    nwin = win_k.shape[1]
    assert nwin == WINDOW and past_len % SEL_BLOCK == 0
    ncmp = kc.shape[1]
    nslot = 64
    assert past_len // SEL_BLOCK + 1 <= nslot
    cos128, sin128 = _rope_tables(jnp.full((1,), past_len, jnp.int32))
    overlap = _overlap_matrix(ncmp, nslot).T
    key_blk = np.arange(past_len) // SEL_BLOCK
    eblk = jnp.asarray((np.arange(nslot)[:, None] == key_blk[None, :]).astype(np.float32) * MASK_BIG, BF16)
    src = np.arange(NSA_HEADS * NSA_HD)
    dst = (src // NSA_HD // NSA_GQ) * NSA_HD + src % NSA_HD
    fold_np = np.zeros((NSA_HEADS * NSA_HD, kvw), np.float32)
    fold_np[src, dst] = 1.0
    fold = jnp.asarray(fold_np, BF16)
    foldt = jnp.asarray(fold_np.T, BF16)
    r3 = lambda t: t.reshape(bs, 1, -1)
    row = lambda wd: pl.BlockSpec((1, 1, wd), lambda b, pt: (b, 0, 0))
    full = lambda shape: pl.BlockSpec(shape, lambda b, pt: (0,) * len(shape))
    per_b = lambda r, c: pl.BlockSpec((1, r, c), lambda b, pt: (b, 0, 0))
    pages = [pl.BlockSpec((None, kvw, page), functools.partial(lambda b, pt, j: (pt[b, j], 0, 0), j=j))
             for j in range(npages)]
    pk3 = _pool_t(pool_sk)
    pv3 = _pool_t(pool_sv)
    gs = pltpu.PrefetchScalarGridSpec(
        num_scalar_prefetch=1, grid=(bs,),
        in_specs=[row(NSA_HEADS * NSA_HD), row(6 * kvw), row(3 * NSA_HEADS)] + pages + pages
        + [per_b(ncmp, kvw), per_b(ncmp, kvw), per_b(kvw, nwin), per_b(kvw, nwin),
           full((1, NSA_HEADS * NSA_HD)), full((3, kvw)), full((1, 128)), full((1, 128)), full((128, 128)),
           full((nslot, ncmp)), full((nslot, past_len)), full(fold.shape), full(foldt.shape)],
        out_specs=[row(NSA_HEADS * NSA_HD), row(kvw), row(kvw), per_b(kvw, nwin), per_b(kvw, nwin)])
    return pl.pallas_call(
        functools.partial(_nsa_sample_kernel, npages=npages, past_len=past_len),
        grid_spec=gs,
        out_shape=[jax.ShapeDtypeStruct((bs, 1, NSA_HEADS * NSA_HD), BF16),
                   jax.ShapeDtypeStruct((bs, 1, kvw), F32), jax.ShapeDtypeStruct((bs, 1, kvw), F32),
                   jax.ShapeDtypeStruct((bs, kvw, nwin), F32), jax.ShapeDtypeStruct((bs, kvw, nwin), F32)],
        compiler_params=_cparams("parallel"),
        name="nsa_sample",
    )(page_table, r3(q), r3(kv), r3(gates), *([pk3] * npages), *([pv3] * npages), kc, vc,
      _pool_t(win_k), _pool_t(win_v),
      jnp.tile(w['nsa_q_norm'], NSA_HEADS).reshape(1, -1), jnp.tile(w['nsa_k_norm'], (1, NSA_KV)),
      cos128, sin128, _segment_ones(), overlap, eblk, fold, foldt)


def _mem_sample_kernel(q_ref, mk_ref, mv_ref, g_ref, o_ref):
    q = q_ref[0]
    mtok = mk_ref.shape[1]
    rows = []
    for h in range(MEM_HEADS):
        qh = q[:, h * MEM_HD:(h + 1) * MEM_HD]
        ms = jnp.mean(qh * qh, axis=-1, keepdims=True)
        rows.append(qh * lax.rsqrt(ms + EPS) * g_ref[...] * (MEM_HD ** -0.5))
    q8 = jnp.concatenate(rows + rows, axis=0).astype(BF16)
    mk = mk_ref[0].reshape(mtok * MEM_HEADS, MEM_HD).astype(BF16)
    mv = mv_ref[0].reshape(mtok * MEM_HEADS, MEM_HD).astype(BF16)
    own = ((lax.broadcasted_iota(jnp.int32, (8, mtok * MEM_HEADS), 1) & (MEM_HEADS - 1))
           == (lax.broadcasted_iota(jnp.int32, (8, mtok * MEM_HEADS), 0) & (MEM_HEADS - 1)))
    s = jnp.where(own, _dot_nt(q8, mk), NEG)
    m = jnp.max(s, axis=-1, keepdims=True)
    p = jnp.exp(s - m)
    p = p / jnp.sum(p, axis=-1, keepdims=True)
    o8 = _dot(p.astype(BF16), mv)
    for h in range(MEM_HEADS):
        o_ref[0, :, h * MEM_HD:(h + 1) * MEM_HD] = o8[h:h + 1].astype(o_ref.dtype)


def _mem_sample(q_mem, mem_k, mem_v, qnorm):
    bs = q_mem.shape[0]
    width = MEM_HEADS * MEM_HD
    mtok = mem_k.shape[1]
    row = pl.BlockSpec((1, 1, width), lambda i: (i, 0, 0))
    mem = pl.BlockSpec((1, mtok, MEM_HEADS, MEM_HD), lambda i: (i, 0, 0, 0))
    return pl.pallas_call(
        _mem_sample_kernel,
        grid=(bs,),
        in_specs=[row, mem, mem, pl.BlockSpec((1, MEM_HD), lambda i: (0, 0))],
        out_specs=row,
        out_shape=jax.ShapeDtypeStruct((bs, 1, width), BF16),
        compiler_params=_cparams("parallel"),
        name="mem_sample",
    )(q_mem.reshape(bs, 1, width), mem_k, mem_v, qnorm.reshape(1, -1)).reshape(bs, width)


def _split_w_in(w_in):
    offs = np.cumsum((0,) + IN_SIZES)
    piece = lambda i: w_in[:, int(offs[i]):int(offs[i + 1])]
    small = jnp.concatenate([piece(2), piece(5)], axis=1)
    small = jnp.pad(small, ((0, 0), (0, 128 - small.shape[1])))
    return dict(z=piece(0), xbc=piece(1), small=small, q=piece(3), kv=piece(4), qmem=piece(6), gmerge=piece(7))


def _in_proj(x, norm_w, wi):
    proj = {name: _matmul(x, wmat, norm_w) for name, wmat in wi.items()}
    proj['dt'] = proj['small'][:, :SSD_HEADS]
    proj['gnsa'] = proj['small'][:, SSD_HEADS:SSD_HEADS + 3 * NSA_HEADS]
    return proj


def _prompt_layer(x_prompt, mem_prompt, w, wb, wi):
    b, length, _ = x_prompt.shape
    n = b * length
    kvw = NSA_KV * NSA_HD
    x = x_prompt.reshape(n, D_MODEL)
    proj = _in_proj(x, w['norm_mix'], wi)
    xbc = proj['xbc'].reshape(b, length, SSD_CONV_DIM)
    y_ssd, ssd_state = _ssd_prompt(xbc, proj['z'].reshape(b, length, D_INNER),
                                   proj['dt'].reshape(b, length, SSD_HEADS), w)
    kv = proj['kv']
    qh, skaug, skf, svh, wkh, wkf, wvh = _nsa_prep(proj['q'], kv, b, length, w)
    kc = _compress_prompt(kv, 0, b, length, w['cmp_k_pe'], w['cmp_k_w1'], w['cmp_k_w2'], w['nsa_k_norm'][0], True)
    vc = _compress_prompt(kv, 1, b, length, w['cmp_v_pe'], w['cmp_v_w1'], w['cmp_v_w2'], w['nsa_k_norm'][0], False)
    to_heads = lambda t: t.reshape(b, -1, NSA_KV, NSA_HD).transpose(0, 2, 1, 3).astype(BF16)
    gates = proj['gnsa'].reshape(b, length, NSA_KV, 3 * NSA_GQ).transpose(0, 2, 1, 3)
    o_nsa = _nsa_prompt(qh, to_heads(kc), to_heads(vc), skaug, svh, wkh, wvh, gates, b, length)
    mem = mem_prompt.reshape(-1, D_MODEL)
    mkv = _matmul(mem, wb['w_mem_kv'], w['mem_norm'], tm=256)
    mtok = mem_prompt.shape[1]
    mk = mkv[:, :MEM_HEADS * MEM_HD].reshape(b, mtok, MEM_HEADS, MEM_HD)
    mk = mk * lax.rsqrt(jnp.mean(mk * mk, axis=-1, keepdims=True) + EPS) * w['mem_k_norm']
    mv = mkv[:, MEM_HEADS * MEM_HD:].reshape(b, mtok, MEM_HEADS, MEM_HD)
    o_mem = _mem_prompt(proj['qmem'], mk.reshape(b, mtok, -1).astype(BF16), mv.reshape(b, mtok, -1).astype(BF16),
                        w['mem_q_norm'], b, length)
    x1 = _merge(x, y_ssd.reshape(n, D_INNER), o_nsa, o_mem, proj['gmerge'], wb)
    u = _matmul(x1, wb['w_up'], w['norm_ffn'])
    y = _ffn_down_prompt(u, x1, wb, b, length)
    r4 = lambda t: t.reshape(b, length, NSA_KV, NSA_HD)
    keep = min(WINDOW, length)
    state = (r4(kv[:, 0:kvw]), r4(kv[:, kvw:2 * kvw]), r4(skf), r4(kv[:, 3 * kvw:4 * kvw]),
             r4(wkf)[:, -keep:], r4(kv[:, 5 * kvw:6 * kvw])[:, -keep:],
             xbc[:, -(SSD_CONV - 1):], ssd_state, u.reshape(b, length, 2 * D_FF)[:, -(FFN_CONV - 1):], mk, mv)
    return y.reshape(b, length, D_MODEL), state


def _sample_layer(x_sample, caches, page_table, w, wb, wi):
    (pool_ck, pool_cv, pool_sk, pool_sv, win_k, win_v, conv_prev, ssd_h0, mem_k, mem_v, ffn_prev) = caches
    bs = x_sample.shape[0]
    kvw = NSA_KV * NSA_HD
    x = x_sample.reshape(bs, D_MODEL)
    proj = _in_proj(x, w['norm_mix'], wi)
    y_ssd, ssd_state = _ssd_step(proj['xbc'], conv_prev, proj['z'], proj['dt'], ssd_h0, w)
    kv = proj['kv']
    kc = _compress_sample(pool_ck, page_table, w['cmp_k_pe'], w['cmp_k_w1'], w['cmp_k_w2'], w['nsa_k_norm'][0], True)
    vc = _compress_sample(pool_cv, page_table, w['cmp_v_pe'], w['cmp_v_w1'], w['cmp_v_w2'], w['nsa_k_norm'][0], False)
    o_nsa, sk_new, wk_new, win_k_new, win_v_new = _nsa_sample(
        proj['q'], kv, proj['gnsa'], pool_sk, pool_sv, page_table, kc, vc, win_k, win_v, w)
    o_mem = _mem_sample(proj['qmem'], mem_k, mem_v, w['mem_q_norm'])
    x1 = _merge(x, y_ssd, o_nsa.reshape(bs, -1), o_mem, proj['gmerge'], wb)
    u = _matmul(x1, wb['w_up'], w['norm_ffn'])
    y = _ffn_down_sample(u, ffn_prev, x1, wb)
    r4 = lambda t: t.reshape(bs, -1, NSA_KV, NSA_HD)
    from_t = lambda t: t.reshape(bs, NSA_KV, NSA_HD, -1).transpose(0, 3, 1, 2)
    state = (r4(kv[:, 0:kvw]), r4(kv[:, kvw:2 * kvw]), r4(sk_new), r4(kv[:, 3 * kvw:4 * kvw]),
             from_t(win_k_new), from_t(win_v_new),
             jnp.concatenate([conv_prev[:, 1:], proj['xbc'][:, None]], axis=1), ssd_state,
             jnp.concatenate([ffn_prev[:, 1:], u[:, None]], axis=1))
    return y.reshape(bs, 1, D_MODEL), state


def kernel(x_prompt, x_sample, cache_nsa_cmp_k, cache_nsa_cmp_v, cache_nsa_sel_k, cache_nsa_sel_v, state_nsa_win_k, state_nsa_win_v, state_ssd_conv, state_ssd, cache_mem_k, cache_mem_v, state_ffn_conv, page_table, mem_prompt, norm_mix, w_in, b_merge, ssd_conv_w, ssd_conv_b, ssd_dt_bias, ssd_a_log, ssd_d, ssd_norm, w_ssd_o, nsa_q_norm, nsa_k_norm, cmp_k_pe, cmp_k_w1, cmp_k_w2, cmp_v_pe, cmp_v_w1, cmp_v_w2, w_nsa_o, mem_norm, w_mem_kv, mem_q_norm, mem_k_norm, w_mem_o, w_out, norm_ffn, w_up, ffn_conv_w, ffn_conv_b, w_down):
    weights = dict(norm_mix=norm_mix, w_in=w_in, b_merge=b_merge, ssd_conv_w=ssd_conv_w,
                   ssd_conv_b=ssd_conv_b, ssd_dt_bias=ssd_dt_bias, ssd_a_log=ssd_a_log, ssd_d=ssd_d,
                   ssd_norm=ssd_norm, w_ssd_o=w_ssd_o, nsa_q_norm=nsa_q_norm, nsa_k_norm=nsa_k_norm,
                   cmp_k_pe=cmp_k_pe, cmp_k_w1=cmp_k_w1, cmp_k_w2=cmp_k_w2, cmp_v_pe=cmp_v_pe,
                   cmp_v_w1=cmp_v_w1, cmp_v_w2=cmp_v_w2, w_nsa_o=w_nsa_o, mem_norm=mem_norm,
                   w_mem_kv=w_mem_kv, mem_q_norm=mem_q_norm, mem_k_norm=mem_k_norm, w_mem_o=w_mem_o,
                   w_out=w_out, norm_ffn=norm_ffn, w_up=w_up, ffn_conv_w=ffn_conv_w,
                   ffn_conv_b=ffn_conv_b, w_down=w_down)
    w = {name: arr[0] for name, arr in weights.items()}
    wb = dict(w)
    for name in ('w_ssd_o', 'w_nsa_o', 'w_mem_o', 'w_out', 'w_up', 'w_down', 'w_mem_kv'):
        wb[name] = w[name].astype(BF16)
    wi = {name: piece.astype(BF16) for name, piece in _split_w_in(w['w_in']).items()}
    y_p, st_p = _prompt_layer(x_prompt, mem_prompt, w, wb, wi)
    caches = (cache_nsa_cmp_k[0], cache_nsa_cmp_v[0], cache_nsa_sel_k[0], cache_nsa_sel_v[0], state_nsa_win_k[0],
              state_nsa_win_v[0], state_ssd_conv[0], state_ssd[0], cache_mem_k[0], cache_mem_v[0], state_ffn_conv[0])
    y_s, st_s = _sample_layer(x_sample, caches, page_table, w, wb, wi)
    return (y_p, y_s) + tuple(s[None] for s in st_p) + tuple(s[None] for s in st_s)
```

```python
import functools
import math

import numpy as np
import jax
import jax.numpy as jnp
from jax import lax
from jax.experimental import pallas as pl
from jax.experimental.pallas import tpu as pltpu

D_MODEL = 1024
D_INNER = 2048
SSD_HEAD_DIM = 64
SSD_HEADS = 32
SSD_GROUPS = 4
SSD_HPG = 8
SSD_STATE = 128
SSD_CONV = 4
SSD_CONV_DIM = 3072
SSD_CHUNK = 128
NSA_HEADS = 16
NSA_KV = 4
NSA_GQ = 4
NSA_HD = 64
CMP_LEN = 32
CMP_STRIDE = 16
CMP_HIDDEN = 128
SEL_BLOCK = 64
SEL_TOPN = 16
SEL_FORCE = 1.0e4
WINDOW = 512
QUERY_BLOCK = 128
MEM_HEADS = 4
MEM_HD = 256
D_FF = 2816
FFN_CONV = 3
ROPE_THETA = 10000.0
EPS = 1e-6
IN_SIZES = (2048, 3072, 32, 1024, 1536, 48, 1024, 3072)

VMEM_LIMIT = 48 * 1024 * 1024
MASK_BIG = 2.0 ** 20
NEG = -1.0e30

F32 = jnp.float32
BF16 = jnp.bfloat16
HI = lax.Precision.HIGHEST


def _cparams(*sem):
    return pltpu.CompilerParams(dimension_semantics=sem, vmem_limit_bytes=VMEM_LIMIT)


def _sigmoid(x):
    return 1.0 / (1.0 + jnp.exp(-x))


def _silu(x):
    return x * _sigmoid(x)


def _dot(a, b):
    return jnp.dot(a, b, preferred_element_type=F32)


def _dot_nt(a, b):
    return lax.dot_general(a, b, (((1,), (1,)), ((), ())), preferred_element_type=F32)


def _dot_hi(a, b):
    return jnp.dot(a, b, preferred_element_type=F32, precision=HI)


def _split3(x):
    hi = x.astype(BF16)
    r1 = x - hi.astype(F32)
    mid = r1.astype(BF16)
    lo = (r1 - mid.astype(F32)).astype(BF16)
    return hi, mid, lo


def _dot_x3(a, b, split):
    if split == 0:
        parts = [_dot(t, b) for t in _split3(a)]
    else:
        parts = [_dot(a, t) for t in _split3(b)]
    return parts[0] + parts[1] + parts[2]


def _mm_kernel(x_ref, g_ref, w_ref, o_ref, xn_ref, *, norm):
    @pl.when(pl.program_id(1) == 0)
    def _():
        x = x_ref[...].astype(F32)
        if norm:
            ms = jnp.mean(x * x, axis=-1, keepdims=True)
            x = x * lax.rsqrt(ms + EPS) * g_ref[...]
        xn_ref[...] = x.astype(BF16)

    o_ref[...] = _dot(xn_ref[...], w_ref[...]).astype(o_ref.dtype)


def _matmul(x, w, g=None, out_dtype=F32, tm=512, tn=512):
    m, k = x.shape
    n = w.shape[1]
    tm = min(tm, m)
    tn = min(tn, n)
    assert m % tm == 0 and n % tn == 0, (m, n, tm, tn)
    norm = g is not None
    if g is None:
        g = jnp.ones((1, k), F32)
    return pl.pallas_call(
        functools.partial(_mm_kernel, norm=norm),
        grid=(m // tm, n // tn),
        in_specs=[pl.BlockSpec((tm, k), lambda i, j: (i, 0)),
                  pl.BlockSpec((1, k), lambda i, j: (0, 0)),
                  pl.BlockSpec((k, tn), lambda i, j: (0, j))],
        out_specs=pl.BlockSpec((tm, tn), lambda i, j: (i, j)),
        out_shape=jax.ShapeDtypeStruct((m, n), out_dtype),
        scratch_shapes=[pltpu.VMEM((tm, k), BF16)],
        compiler_params=_cparams("parallel", "arbitrary"),
        name="rms_matmul" if norm else "matmul",
    )(x, g.reshape(1, k).astype(F32), w)


def _shift_rows(x, k, prev, row):
    r = pltpu.roll(x, k, axis=0)
    head = r[:8]
    for i in range(k):
        head = jnp.where(row == i, prev[8 - k + i:8 - k + i + 1, :], head)
    return jnp.concatenate([head, r[8:]], axis=0)


def _ssd_prompt_kernel(xbc_ref, z_ref, dt_ref, dtT_ref, cw_ref, cb_ref, dtb_ref, dtbT_ref, a_ref, aT_ref,
                       dskip_ref, nw_ref, e_ref, y_ref, hT_ref, carry_ref, h_ref, yacc_ref):
    c = pl.program_id(1)
    q = SSD_CHUNK

    @pl.when(c == 0)
    def _():
        carry_ref[...] = jnp.zeros_like(carry_ref)
        h_ref[...] = jnp.zeros_like(h_ref)

    xbc = xbc_ref[0]
    row = lax.broadcasted_iota(jnp.int32, (8, 1), 0)
    prev = carry_ref[...]
    conv = cb_ref[...] + cw_ref[3:4, :] * xbc
    for k in range(1, SSD_CONV):
        conv = conv + cw_ref[3 - k:4 - k, :] * _shift_rows(xbc, k, prev, row)
    carry_ref[...] = xbc[q - 8:, :]
    xc = _silu(conv)
    xs = xc[:, :D_INNER]
    bm = xc[:, D_INNER:D_INNER + SSD_GROUPS * SSD_STATE].astype(BF16)
    cm = xc[:, D_INNER + SSD_GROUPS * SSD_STATE:].astype(BF16)

    def softplus(v):
        return jnp.maximum(v, 0.0) + jnp.log(1.0 + jnp.exp(-jnp.abs(v)))

    dt = softplus(dt_ref[0] + dtb_ref[...])
    dtT = softplus(dtT_ref[0] + dtbT_ref[...])
    ii = lax.broadcasted_iota(jnp.int32, (q, q), 0)
    jj = lax.broadcasted_iota(jnp.int32, (q, q), 1)
    causal = ii >= jj
    cum = _dot_x3(jnp.where(causal, 1.0, 0.0).astype(BF16), dt * a_ref[...], 1)
    cumT = _dot_x3(dtT * aT_ref[...], jnp.where(jj >= ii, 1.0, 0.0).astype(BF16), 0)
    ecum = jnp.exp(cum)
    dend = jnp.exp(cum[q - 1:q, :] - cum)
    spread = _dot_x3(jnp.concatenate([dt, ecum, dend], axis=0), e_ref[...], 0)
    dt_x, ecum_x, dend_x = spread[:q], spread[q:2 * q], spread[2 * q:]
    xdt = xs * dt_x
    xdt_b = xdt.astype(BF16)
    xw_b = (xdt * dend_x).astype(BF16)
    gw = SSD_HPG * SSD_HEAD_DIM
    for g in range(SSD_GROUPS):
        bg = bm[:, g * SSD_STATE:(g + 1) * SSD_STATE]
        cg = cm[:, g * SSD_STATE:(g + 1) * SSD_STATE]
        cb = _dot_nt(cg, bg)
        h_prev = h_ref[g]
        yoff = _dot(cg, h_prev.astype(BF16)) * ecum_x[:, g * gw:(g + 1) * gw]
        st = _dot(bg.astype(F32).T.astype(BF16), xw_b[:, g * gw:(g + 1) * gw])
        h_ref[g] = h_prev * ecum_x[q - 1:q, g * gw:(g + 1) * gw] + st
        for eh in range(SSD_HPG):
            hh = g * SSD_HPG + eh
            seg = cum[:, hh:hh + 1] - cumT[hh:hh + 1, :]
            decay = jnp.exp(jnp.where(causal, seg, NEG))
            mm = (cb * decay).astype(BF16)
            lo = hh * SSD_HEAD_DIM
            yd = _dot(mm, xdt_b[:, lo:lo + SSD_HEAD_DIM])
            yacc_ref[:, lo:lo + SSD_HEAD_DIM] = yd + yoff[:, eh * SSD_HEAD_DIM:(eh + 1) * SSD_HEAD_DIM]
    y = yacc_ref[...] + dskip_ref[...] * xs
    yz = y * _silu(z_ref[0])
    ms = jnp.mean(yz * yz, axis=-1, keepdims=True)
    y_ref[0] = (yz * lax.rsqrt(ms + EPS) * nw_ref[...]).astype(y_ref.dtype)

    @pl.when(c == pl.num_programs(1) - 1)
    def _():
        hT_ref[0] = h_ref[...]


def _head_expand(width=SSD_HEAD_DIM):
    e = np.zeros((SSD_HEADS, SSD_HEADS * width), np.float32)
    for h in range(SSD_HEADS):
        e[h, h * width:(h + 1) * width] = 1.0
    return jnp.asarray(e, BF16)


def _ssd_prompt(xbc, z, dt_raw, w):
    b, length, _ = xbc.shape
    q = SSD_CHUNK
    nc = length // q
    dtT = jnp.swapaxes(dt_raw, 1, 2)
    a = -jnp.exp(w['ssd_a_log'].astype(F32))
    full = lambda shape: pl.BlockSpec(shape, lambda i, j: (0,) * len(shape))
    y, hT = pl.pallas_call(
        _ssd_prompt_kernel,
        grid=(b, nc),
        in_specs=[pl.BlockSpec((1, q, SSD_CONV_DIM), lambda i, j: (i, j, 0)),
                  pl.BlockSpec((1, q, D_INNER), lambda i, j: (i, j, 0)),
                  pl.BlockSpec((1, q, SSD_HEADS), lambda i, j: (i, j, 0)),
                  pl.BlockSpec((1, SSD_HEADS, q), lambda i, j: (i, 0, j)),
                  full((SSD_CONV, SSD_CONV_DIM)), full((1, SSD_CONV_DIM)),
                  full((1, SSD_HEADS)), full((SSD_HEADS, 1)), full((1, SSD_HEADS)), full((SSD_HEADS, 1)),
                  full((1, D_INNER)), full((1, D_INNER)), full((SSD_HEADS, D_INNER))],
        out_specs=[pl.BlockSpec((1, q, D_INNER), lambda i, j: (i, j, 0)),
                   pl.BlockSpec((1, SSD_GROUPS, SSD_STATE, SSD_HPG * SSD_HEAD_DIM), lambda i, j: (i, 0, 0, 0))],
        out_shape=[jax.ShapeDtypeStruct((b, length, D_INNER), BF16),
                   jax.ShapeDtypeStruct((b, SSD_GROUPS, SSD_STATE, SSD_HPG * SSD_HEAD_DIM), F32)],
        scratch_shapes=[pltpu.VMEM((8, SSD_CONV_DIM), F32),
                        pltpu.VMEM((SSD_GROUPS, SSD_STATE, SSD_HPG * SSD_HEAD_DIM), F32),
                        pltpu.VMEM((q, D_INNER), F32)],
        compiler_params=_cparams("parallel", "arbitrary"),
        name="ssd_prompt",
    )(xbc, z, dt_raw, dtT, w['ssd_conv_w'], w['ssd_conv_b'].reshape(1, -1),
      w['ssd_dt_bias'].reshape(1, -1), w['ssd_dt_bias'].reshape(-1, 1), a.reshape(1, -1), a.reshape(-1, 1),
      jnp.repeat(w['ssd_d'].astype(F32), SSD_HEAD_DIM).reshape(1, -1), w['ssd_norm'].reshape(1, -1),
      _head_expand())
    state = hT.reshape(b, SSD_GROUPS, SSD_STATE, SSD_HPG, SSD_HEAD_DIM).transpose(0, 1, 3, 4, 2)
    return y, state.reshape(b, SSD_HEADS, SSD_HEAD_DIM, SSD_STATE)


def _segment_ones():
    i = np.arange(128)
    return jnp.asarray((i[:, None] // NSA_HD == i[None, :] // NSA_HD).astype(np.float32), BF16)


def _head_rmsnorm(x, g_row, segm):
    sq = x * x
    hi = sq.astype(BF16)
    lo = (sq - hi.astype(F32)).astype(BF16)
    parts = []
    for c in range(x.shape[1] // 128):
        sl = slice(c * 128, (c + 1) * 128)
        parts.append(_dot(hi[:, sl], segm) + _dot(lo[:, sl], segm))
    ss = parts[0] if len(parts) == 1 else jnp.concatenate(parts, axis=1)
    return x * lax.rsqrt(ss * (1.0 / NSA_HD) + EPS) * g_row


def _tile_lanes(t, width):
    reps = width // t.shape[1]
    return t if reps == 1 else jnp.concatenate([t] * reps, axis=1)


def _rope(x, cos128, sin128):
    width = x.shape[1]
    lane = lax.broadcasted_iota(jnp.int32, x.shape, 1)
    first = (lane & (NSA_HD // 2)) == 0
    rot = jnp.where(first, pltpu.roll(x, width - NSA_HD // 2, axis=1), pltpu.roll(x, NSA_HD // 2, axis=1))
    return x * _tile_lanes(cos128, width) + rot * _tile_lanes(sin128, width)


def _rope_tables(pos):
    half = NSA_HD // 2
    inv = ROPE_THETA ** (-jnp.arange(half, dtype=F32) / half)
    ang = pos.astype(F32)[:, None] * inv[None, :]
    cos, sin = jnp.cos(ang), jnp.sin(ang)
    cos128 = jnp.concatenate([cos, cos, cos, cos], axis=1)
    sin128 = jnp.concatenate([-sin, sin, -sin, sin], axis=1)
    return cos128, sin128


def _nsa_prep_kernel(q_ref, kv_ref, cos_ref, sin_ref, qg_ref, kg_ref, segm_ref,
                     qh_ref, skaug_ref, skf_ref, svh_ref, wkh_ref, wkf_ref, wvh_ref):
    tr = q_ref.shape[0]
    i = pl.program_id(1)
    cos, sin, segm = cos_ref[...], sin_ref[...], segm_ref[...]
    kvw = NSA_KV * NSA_HD
    q = _rope(_head_rmsnorm(q_ref[...], qg_ref[...], segm), cos, sin) * (NSA_HD ** -0.5)
    for h in range(NSA_HEADS):
        qh_ref[0, h] = q[:, h * NSA_HD:(h + 1) * NSA_HD].astype(BF16)
    sk = _rope(_head_rmsnorm(kv_ref[:, 2 * kvw:3 * kvw], kg_ref[1:2, :], segm), cos, sin)
    wk = _rope(_head_rmsnorm(kv_ref[:, 4 * kvw:5 * kvw], kg_ref[2:3, :], segm), cos, sin)
    skf_ref[...] = sk
    wkf_ref[...] = wk
    sv = kv_ref[:, 3 * kvw:4 * kvw]
    wv = kv_ref[:, 5 * kvw:6 * kvw]
    pos = i * tr + lax.broadcasted_iota(jnp.int32, (tr, NSA_HD), 0)
    blk = lax.broadcasted_iota(jnp.int32, (tr, NSA_HD), 1)
    onehot = jnp.where((pos >> 6) == blk, MASK_BIG, 0.0).astype(BF16)
    for k in range(NSA_KV):
        sl = slice(k * NSA_HD, (k + 1) * NSA_HD)
        skaug_ref[0, k] = jnp.concatenate([sk[:, sl].astype(BF16), onehot], axis=1)
        svh_ref[0, k] = sv[:, sl].astype(BF16)
        wkh_ref[0, k] = wk[:, sl].astype(BF16)
        wvh_ref[0, k] = wv[:, sl].astype(BF16)


def _nsa_prep(q, kv, b, length, w, tr=256):
    n = b * length
    nl = length // tr
    cos128, sin128 = _rope_tables(jnp.arange(length, dtype=jnp.int32))
    kvw = NSA_KV * NSA_HD
    row = lambda wd: pl.BlockSpec((tr, wd), lambda bi, i: (bi * nl + i, 0))
    tab = pl.BlockSpec((tr, 128), lambda bi, i: (i, 0))
    full = lambda shape: pl.BlockSpec(shape, lambda bi, i: (0,) * len(shape))
    hm = lambda nh, wd: pl.BlockSpec((1, nh, tr, wd), lambda bi, i: (bi, 0, i, 0))
    return pl.pallas_call(
        _nsa_prep_kernel,
        grid=(b, nl),
        in_specs=[row(NSA_HEADS * NSA_HD), row(6 * kvw), tab, tab, full((1, NSA_HEADS * NSA_HD)), full((3, kvw)),
                  full((128, 128))],
        out_specs=[hm(NSA_HEADS, NSA_HD), hm(NSA_KV, 2 * NSA_HD), row(kvw), hm(NSA_KV, NSA_HD), hm(NSA_KV, NSA_HD),
                   row(kvw), hm(NSA_KV, NSA_HD)],
        out_shape=[jax.ShapeDtypeStruct((b, NSA_HEADS, length, NSA_HD), BF16),
                   jax.ShapeDtypeStruct((b, NSA_KV, length, 2 * NSA_HD), BF16),
                   jax.ShapeDtypeStruct((n, kvw), F32),
                   jax.ShapeDtypeStruct((b, NSA_KV, length, NSA_HD), BF16),
                   jax.ShapeDtypeStruct((b, NSA_KV, length, NSA_HD), BF16),
                   jax.ShapeDtypeStruct((n, kvw), F32),
                   jax.ShapeDtypeStruct((b, NSA_KV, length, NSA_HD), BF16)],
        compiler_params=_cparams("parallel", "parallel"),
        name="nsa_prep",
    )(q, kv, cos128, sin128, jnp.tile(w['nsa_q_norm'], NSA_HEADS).reshape(1, -1),
      jnp.tile(w['nsa_k_norm'], (1, NSA_KV)), _segment_ones())


def _compress_kernel(*refs, n_x, is_k, n_prefetch, from_pages):
    refs = refs[n_prefetch:]
    if from_pages:
        page_refs = refs[:n_x]
        wbd_ref, pe_ref, w2_ref, g_ref, cos_ref, sin_ref, segm_ref, o_ref, x0_ref, x1_ref = refs[n_x:]
        plen = page_refs[0].shape[-1]
        for j, r in enumerate(page_refs):
            rows = r[...].T
            x0_ref[j * plen:(j + 1) * plen, :] = rows[:, :128]
            x1_ref[j * plen:(j + 1) * plen, :] = rows[:, 128:]
        x_refs = ((x0_ref,), (x1_ref,))
        n_x = 1
    else:
        x_refs = (refs[:n_x], refs[n_x:2 * n_x])
        wbd_ref, pe_ref, w2_ref, g_ref, cos_ref, sin_ref, segm_ref, o_ref = refs[2 * n_x:]
    per = x_refs[0][0].shape[-2] // CMP_STRIDE
    nchunk = per * n_x
    acc = [jnp.zeros((nchunk, 4 * CMP_HIDDEN), F32) for _ in range(2)]
    for s in range(CMP_STRIDE):
        for p in range(2):
            xs = [r[pl.ds(s, per, stride=CMP_STRIDE), :] for r in x_refs[p]]
            x = (xs[0] if n_x == 1 else jnp.concatenate(xs, axis=0)).astype(BF16)
            acc[p] = acc[p] + _dot(x, wbd_ref[s])
    row = lax.broadcasted_iota(jnp.int32, (nchunk, 1), 0)
    outs = []
    for p in range(2):
        hid = []
        for kl in range(2):
            lo = kl * 2 * CMP_HIDDEN
            first = acc[p][:, lo:lo + CMP_HIDDEN]
            second = pltpu.roll(acc[p][:, lo + CMP_HIDDEN:lo + 2 * CMP_HIDDEN], nchunk - 1, axis=0)
            hid.append(_silu(first + second + pe_ref[...]))
        outs.append(_dot(jnp.concatenate(hid, axis=1).astype(BF16), w2_ref[...]))
    out = jnp.concatenate(outs, axis=1)
    if is_k:
        out = _rope(_head_rmsnorm(out, g_ref[...], segm_ref[...]), cos_ref[...], sin_ref[...])
    o_ref[0] = jnp.where(row < nchunk - 1, out, 0.0)


def _compress_weights(pe, w1, w2):
    w1r = w1.reshape(CMP_LEN, NSA_HD, CMP_HIDDEN)
    pe_term = jnp.einsum('ld,ldm->m', pe, w1r, precision=HI).reshape(1, CMP_HIDDEN)
    both = jnp.concatenate([w1r[:CMP_STRIDE], w1r[CMP_STRIDE:]], axis=-1)
    zero = jnp.zeros_like(both)
    wbd = jnp.concatenate([jnp.concatenate([both, zero], axis=-1), jnp.concatenate([zero, both], axis=-1)], axis=1)
    zero2 = jnp.zeros_like(w2)
    w2bd = jnp.concatenate([jnp.concatenate([w2, zero2], axis=-1), jnp.concatenate([zero2, w2], axis=-1)], axis=0)
    return wbd.astype(BF16), pe_term, w2bd.astype(BF16)


def _compress_call(x_args, x_specs, grid, nchunk, batch, pe, w1, w2, knorm, is_k, n_prefetch=0, prefetch=(),
                   from_pages=False):
    wbd, pe_term, w2bd = _compress_weights(pe, w1, w2)
    cos128, sin128 = _rope_tables(jnp.arange(nchunk, dtype=jnp.int32) * CMP_STRIDE + (CMP_LEN - 1))
    kvw = NSA_KV * NSA_HD
    full = lambda shape: pl.BlockSpec(shape, lambda *a: (0,) * len(shape))
    gs = pltpu.PrefetchScalarGridSpec(
        num_scalar_prefetch=n_prefetch, grid=grid,
        in_specs=list(x_specs) + [full(wbd.shape), full((1, CMP_HIDDEN)), full(w2bd.shape), full((1, kvw)),
                                  full((nchunk, 128)), full((nchunk, 128)), full((128, 128))],
        out_specs=pl.BlockSpec((1, nchunk, kvw), lambda i, *a: (i, 0, 0)),
        scratch_shapes=[pltpu.VMEM((nchunk * CMP_STRIDE, 128), F32)] * 2 if from_pages else [])
    return pl.pallas_call(
        functools.partial(_compress_kernel, n_x=len(x_specs) if from_pages else len(x_specs) // 2, is_k=is_k,
                          n_prefetch=n_prefetch, from_pages=from_pages),
        grid_spec=gs,
        out_shape=jax.ShapeDtypeStruct((batch, nchunk, kvw), F32),
        compiler_params=_cparams("parallel"),
        name="compress_k" if is_k else "compress_v",
    )(*prefetch, *x_args, wbd, pe_term, w2bd, jnp.tile(knorm, NSA_KV).reshape(1, -1), cos128, sin128,
      _segment_ones())


def _compress_prompt(kv, col, b, length, pe, w1, w2, knorm, is_k):
    specs = [pl.BlockSpec((length, 128), functools.partial(lambda i, c: (i, c), c=2 * col + p)) for p in range(2)]
    return _compress_call([kv, kv], specs, (b,), length // CMP_STRIDE, b, pe, w1, w2, knorm, is_k)


SEL_TILE = 512
WIN_TILE = 128


def _flash_step(carry, s, v):
    m, l, acc = carry
    m_new = jnp.maximum(m, jnp.max(s, axis=-1, keepdims=True))
    alpha = jnp.exp(m - m_new)
    p = jnp.exp(s - m_new)
    l = alpha * l + jnp.sum(p, axis=-1, keepdims=True)
    acc = alpha * acc + _dot(p.astype(BF16), v)
    return m_new, l, acc


def _topn_bias(score, n):
    nblk, nq = score.shape
    groups = [score[8 * v:8 * v + 8] for v in range(nblk // 8)]
    sub = lax.broadcasted_iota(jnp.int32, (8, nq), 0)
    cnt = [jnp.zeros((8, nq), F32) for _ in groups]
    for jp in range(nblk):
        row = score[jp:jp + 1]
        for v, grp in enumerate(groups):
            if v < jp // 8:
                inc = jnp.where(row > grp, 1.0, 0.0)
            elif v > jp // 8:
                inc = jnp.where(row >= grp, 1.0, 0.0)
            else:
                inc = jnp.where(sub > jp % 8, jnp.where(row >= grp, 1.0, 0.0), jnp.where(row > grp, 1.0, 0.0))
            cnt[v] = cnt[v] + inc
    return jnp.concatenate([jnp.where(c < n, 0.0, -1.0) for c in cnt], axis=0)


def _nsa_prompt_kernel(q_ref, kc_ref, vc_ref, sk_ref, sv_ref, wk_ref, wv_ref, g_ref, ov_ref, o_ref):
    qb = pl.program_id(2)
    qlen = QUERY_BLOCK
    rows = NSA_GQ * qlen
    t0 = qb * qlen
    q4 = q_ref[0].reshape(rows, NSA_HD)
    trow = t0 + (lax.broadcasted_iota(jnp.int32, (rows, 1), 0) & (qlen - 1))
    ncmp = kc_ref.shape[2]
    s = _dot_nt(q4, kc_ref[0, 0])
    kend = lax.broadcasted_iota(jnp.int32, (1, ncmp), 1) * CMP_STRIDE + (CMP_LEN - 1)
    valid = kend <= trow
    s = jnp.where(valid, s, NEG)
    m = jnp.max(s, axis=-1, keepdims=True)
    p = jnp.where(valid, jnp.exp(s - m), 0.0)
    den = jnp.sum(p, axis=-1, keepdims=True)
    p = p / jnp.where(den > 0, den, 1.0)
    o_c = _dot(p.astype(BF16), vc_ref[0, 0])
    psum = p[0:qlen] + p[qlen:2 * qlen] + p[2 * qlen:3 * qlen] + p[3 * qlen:4 * qlen]
    nblk = ov_ref.shape[0]
    imp = lax.dot_general(ov_ref[...], psum, (((1,), (1,)), ((), ())), preferred_element_type=F32,
                          precision=HI)
    jb = lax.broadcasted_iota(jnp.int32, (nblk, 1), 0)
    t = t0 + lax.broadcasted_iota(jnp.int32, (1, qlen), 1)
    cur = t >> 6
    forced = (jb == 0) | (jb == cur) | (jb == cur - 1)
    score = jnp.where(jb * SEL_BLOCK <= t, imp + jnp.where(forced, SEL_FORCE, 0.0), NEG)
    bias = _topn_bias(score, min(SEL_TOPN, nblk)).T.astype(BF16)
    qaug = jnp.concatenate([q4, jnp.concatenate([bias] * NSA_GQ, axis=0)], axis=1)
    init = (jnp.full((rows, 1), NEG, F32), jnp.zeros((rows, 1), F32), jnp.zeros((rows, NSA_HD), F32))

    def sel_body(kt, carry):
        k0 = pl.multiple_of(kt * SEL_TILE, SEL_TILE)
        sc = _dot_nt(qaug, sk_ref[0, 0, pl.ds(k0, SEL_TILE), :])
        return _flash_step(carry, sc, sv_ref[0, 0, pl.ds(k0, SEL_TILE), :])

    n_full = t0 // SEL_TILE
    carry = lax.fori_loop(0, n_full, sel_body, init)
    k0 = pl.multiple_of(n_full * SEL_TILE, SEL_TILE)
    kpos = k0 + lax.broadcasted_iota(jnp.int32, (1, SEL_TILE), 1)
    sc = jnp.where(kpos <= trow, _dot_nt(qaug, sk_ref[0, 0, pl.ds(k0, SEL_TILE), :]), NEG)
    _, l_s, acc_s = _flash_step(carry, sc, sv_ref[0, 0, pl.ds(k0, SEL_TILE), :])
    o_s = acc_s / l_s

    span = min(WINDOW + qlen, wk_ref.shape[2])
    kw0 = pl.multiple_of(jnp.maximum(t0 - WINDOW, 0), qlen)
    kpos = kw0 + lax.broadcasted_iota(jnp.int32, (1, span), 1)
    s_w = _dot_nt(q4, wk_ref[0, 0, pl.ds(kw0, span), :])
    s_w = jnp.where(kpos <= trow, jnp.where(kpos >= trow - WINDOW, s_w, NEG), NEG)
    p_w = jnp.exp(s_w - jnp.max(s_w, axis=-1, keepdims=True))
    o_w = _dot(p_w.astype(BF16), wv_ref[0, 0, pl.ds(kw0, span), :]) / jnp.sum(p_w, axis=-1, keepdims=True)

    gate = _sigmoid(g_ref[0, 0])
    for g in range(NSA_GQ):
        r = slice(g * qlen, (g + 1) * qlen)
        og = (gate[:, 3 * g:3 * g + 1] * o_c[r] + gate[:, 3 * g + 1:3 * g + 2] * o_s[r]
              + gate[:, 3 * g + 2:3 * g + 3] * o_w[r])
        o_ref[:, g * NSA_HD:(g + 1) * NSA_HD] = og.astype(o_ref.dtype)


def _overlap_matrix(ncmp, nblk):
    ci = np.arange(ncmp)[:, None] * CMP_STRIDE
    sj = np.arange(nblk)[None, :] * SEL_BLOCK
    return jnp.asarray(((ci <= sj + SEL_BLOCK - 1) & (ci + CMP_LEN - 1 >= sj)).astype(np.float32))


def _nsa_prompt(qh, kc, vc, skaug, svh, wkh, wvh, gates, b, length):
    nb = length // QUERY_BLOCK
    ncmp = kc.shape[2]
    nblk = NSA_HD
    assert length // SEL_BLOCK <= nblk
    seq = lambda wd: pl.BlockSpec((1, 1, length, wd), lambda bi, k, i: (bi, k, 0, 0))
    cmp_spec = pl.BlockSpec((1, 1, ncmp, NSA_HD), lambda bi, k, i: (bi, k, 0, 0))
    return pl.pallas_call(
        _nsa_prompt_kernel,
        grid=(b, NSA_KV, nb),
        in_specs=[pl.BlockSpec((1, NSA_GQ, QUERY_BLOCK, NSA_HD), lambda bi, k, i: (bi, k, i, 0)),
                  cmp_spec, cmp_spec, seq(2 * NSA_HD), seq(NSA_HD), seq(NSA_HD), seq(NSA_HD),
                  pl.BlockSpec((1, 1, QUERY_BLOCK, 3 * NSA_GQ), lambda bi, k, i: (bi, k, i, 0)),
                  pl.BlockSpec((nblk, ncmp), lambda bi, k, i: (0, 0))],
        out_specs=pl.BlockSpec((QUERY_BLOCK, NSA_GQ * NSA_HD), lambda bi, k, i: (bi * nb + i, k)),
        out_shape=jax.ShapeDtypeStruct((b * length, NSA_HEADS * NSA_HD), BF16),
        compiler_params=_cparams("parallel", "parallel", "arbitrary"),
        name="nsa_prompt",
    )(qh, kc, vc, skaug, svh, wkh, wvh, gates, _overlap_matrix(ncmp, nblk).T)


def _mem_prompt_kernel(q_ref, mk_ref, mv_ref, g_ref, o_ref):
    for h in range(MEM_HEADS):
        sl = slice(h * MEM_HD, (h + 1) * MEM_HD)
        q = q_ref[:, sl]
        ms = jnp.mean(q * q, axis=-1, keepdims=True)
        qn = (q * lax.rsqrt(ms + EPS) * g_ref[...] * (MEM_HD ** -0.5)).astype(BF16)
        s = _dot_nt(qn, mk_ref[0, :, sl])
        m = jnp.max(s, axis=-1, keepdims=True)
        p = jnp.exp(s - m)
        p = p / jnp.sum(p, axis=-1, keepdims=True)
        o_ref[:, sl] = _dot(p.astype(BF16), mv_ref[0, :, sl]).astype(o_ref.dtype)


def _mem_prompt(q_mem, mk, mv, qnorm, b, length, tq=256):
    nl = length // tq
    width = MEM_HEADS * MEM_HD
    mem = pl.BlockSpec((1, mk.shape[1], width), lambda bi, i: (bi, 0, 0))
    return pl.pallas_call(
        _mem_prompt_kernel,
        grid=(b, nl),
        in_specs=[pl.BlockSpec((tq, width), lambda bi, i: (bi * nl + i, 0)), mem, mem,
                  pl.BlockSpec((1, MEM_HD), lambda bi, i: (0, 0))],
        out_specs=pl.BlockSpec((tq, width), lambda bi, i: (bi * nl + i, 0)),
        out_shape=jax.ShapeDtypeStruct((b * length, width), BF16),
        compiler_params=_cparams("parallel", "parallel"),
        name="mem_prompt",
    )(q_mem, mk, mv, qnorm.reshape(1, -1))


def _merge_kernel(x_ref, ys_ref, yn_ref, ym_ref, gm_ref, bm_ref, ws_ref, wn_ref, wm_ref, wo_ref, o_ref):
    gate = _sigmoid(gm_ref[...] + bm_ref[...])
    mixed = (gate[:, :D_MODEL] * _dot(ys_ref[...], ws_ref[...])
             + gate[:, D_MODEL:2 * D_MODEL] * _dot(yn_ref[...], wn_ref[...])
             + gate[:, 2 * D_MODEL:] * _dot(ym_ref[...], wm_ref[...]))
    o_ref[...] = x_ref[...] + _dot(mixed.astype(BF16), wo_ref[...])


def _merge(x, ys, yn, ym, gm, wb, tm=256):
    m = x.shape[0]
    tm = min(tm, m)
    row = lambda wd: pl.BlockSpec((tm, wd), lambda i: (i, 0))
    full = lambda shape: pl.BlockSpec(shape, lambda i: (0,) * len(shape))
    return pl.pallas_call(
        _merge_kernel,
        grid=(m // tm,),
        in_specs=[row(D_MODEL), row(D_INNER), row(D_MODEL), row(D_MODEL), row(3 * D_MODEL), full((1, 3 * D_MODEL)),
                  full((D_INNER, D_MODEL)), full((D_MODEL, D_MODEL)), full((D_MODEL, D_MODEL)),
                  full((D_MODEL, D_MODEL))],
        out_specs=row(D_MODEL),
        out_shape=jax.ShapeDtypeStruct((m, D_MODEL), F32),
        compiler_params=_cparams("parallel"),
        name="merge",
    )(x, ys, yn, ym, gm, wb['b_merge'].reshape(1, -1), wb['w_ssd_o'], wb['w_nsa_o'], wb['w_mem_o'], wb['w_out'])


FFN_COLS = 2816


def _ffn_kernel(*refs, seq):
    if seq:
        x_ref, g_ref, wu_ref, cw_ref, cb_ref, wd_ref, o_ref, ul_ref, carry_ref = refs
    else:
        x_ref, g_ref, wu_ref, cw_ref, cb_ref, wd_ref, p0_ref, p1_ref, o_ref, ul_ref = refs
    x = x_ref[...]
    tm = x.shape[0]
    ms = jnp.mean(x * x, axis=-1, keepdims=True)
    xn = (x * lax.rsqrt(ms + EPS) * g_ref[...]).astype(BF16)
    if seq:
        @pl.when(pl.program_id(1) == 0)
        def _():
            carry_ref[...] = jnp.zeros_like(carry_ref)

        row = lax.broadcasted_iota(jnp.int32, (8, 1), 0)

    def conv_cols(lo):
        sl = slice(lo, lo + FFN_COLS)
        u = _dot(xn, wu_ref[:, sl])
        if seq:
            prev = carry_ref[:, sl]
            u1 = _shift_rows(u, 1, prev, row)
            u2 = _shift_rows(u, 2, prev, row)
            carry_ref[:, sl] = u[tm - 8:, :]
        else:
            u2, u1 = p0_ref[:, sl], p1_ref[:, sl]
            ul_ref[:, sl] = u
        return cb_ref[:, sl] + cw_ref[0:1, sl] * u2 + cw_ref[1:2, sl] * u1 + cw_ref[2:3, sl] * u

    acc = x
    for j in range(D_FF // FFN_COLS):
        act = (_silu(conv_cols(j * FFN_COLS)) * conv_cols(D_FF + j * FFN_COLS)).astype(BF16)
        acc = acc + _dot(act, wd_ref[j * FFN_COLS:(j + 1) * FFN_COLS, :])
    o_ref[...] = acc
    if seq:
        @pl.when(pl.program_id(1) == pl.num_programs(1) - 1)
        def _():
            ul_ref[0] = carry_ref[...]


def _ffn_prompt(x1, w, wb, b, length, tm=256):
    nl = length // tm
    row = lambda wd: pl.BlockSpec((tm, wd), lambda bi, i: (bi * nl + i, 0))
    const = lambda shape: pl.BlockSpec(shape, lambda bi, i: (0,) * len(shape), pipeline_mode=pl.Buffered(1))
    return pl.pallas_call(
        functools.partial(_ffn_kernel, seq=True),
        grid=(b, nl),
        in_specs=[row(D_MODEL), const((1, D_MODEL)), const((D_MODEL, 2 * D_FF)), const((FFN_CONV, 2 * D_FF)),
                  const((1, 2 * D_FF)), const((D_FF, D_MODEL))],
        out_specs=[row(D_MODEL), pl.BlockSpec((1, 8, 2 * D_FF), lambda bi, i: (bi, 0, 0))],
        out_shape=[jax.ShapeDtypeStruct((b * length, D_MODEL), F32), jax.ShapeDtypeStruct((b, 8, 2 * D_FF), F32)],
        scratch_shapes=[pltpu.VMEM((8, 2 * D_FF), F32)],
        compiler_params=_cparams("parallel", "arbitrary"),
        name="ffn_prompt",
    )(x1, w['norm_ffn'].reshape(1, -1), wb['w_up'], wb['ffn_conv_w'], wb['ffn_conv_b'].reshape(1, -1), wb['w_down'])


def _ffn_sample(x1, prev, w, wb):
    m = x1.shape[0]
    full = lambda shape: pl.BlockSpec(shape, lambda i: (0,) * len(shape))
    return pl.pallas_call(
        functools.partial(_ffn_kernel, seq=False),
        grid=(1,),
        in_specs=[full((m, D_MODEL)), full((1, D_MODEL)), full((D_MODEL, 2 * D_FF)), full((FFN_CONV, 2 * D_FF)),
                  full((1, 2 * D_FF)), full((D_FF, D_MODEL)), full((m, 2 * D_FF)), full((m, 2 * D_FF))],
        out_specs=[full((m, D_MODEL)), full((m, 2 * D_FF))],
        out_shape=[jax.ShapeDtypeStruct((m, D_MODEL), F32), jax.ShapeDtypeStruct((m, 2 * D_FF), F32)],
        compiler_params=_cparams("arbitrary"),
        name="ffn_sample",
    )(x1, w['norm_ffn'].reshape(1, -1), wb['w_up'], wb['ffn_conv_w'], wb['ffn_conv_b'].reshape(1, -1), wb['w_down'],
      prev[:, 0], prev[:, 1])


def _compress_sample(pool, page_table, pe, w1, w2, knorm, is_k):
    bs, npages = page_table.shape
    page = pool.shape[1]
    pool_t = _pool_t(pool)
    specs = [pl.BlockSpec((None, NSA_KV * NSA_HD, page), functools.partial(lambda b, pt, j: (pt[b, j], 0, 0), j=j))
             for j in range(npages)]
    return _compress_call([pool_t] * npages, specs, (bs,), npages * page // CMP_STRIDE, bs, pe, w1, w2, knorm,
                          is_k, n_prefetch=1, prefetch=(page_table,), from_pages=True)


def _pool_t(pool):
    n, page = pool.shape[:2]
    return pool.transpose(0, 2, 3, 1).reshape(n, NSA_KV * NSA_HD, page)


def _rows8(x):
    return jnp.broadcast_to(x, (8, x.shape[1]))


def _ssd_step_kernel(xbc_ref, prev_ref, z_ref, dt_ref, h_ref, cw_ref, cb_ref, dtb_ref, a_ref, dskip_ref, nw_ref,
                     e64_ref, e128_ref, y_ref, hn_ref):
    conv = cb_ref[...] + cw_ref[3:4, :] * xbc_ref[0]
    for k in range(SSD_CONV - 1):
        conv = conv + cw_ref[k:k + 1, :] * prev_ref[0, k:k + 1, :]
    xc = _silu(conv)
    xs = xc[:, :D_INNER]
    nb = SSD_GROUPS * SSD_STATE
    bm = xc[:, D_INNER:D_INNER + nb]
    cm = xc[:, D_INNER + nb:]
    v = dt_ref[0] + dtb_ref[...]
    dt = jnp.maximum(v, 0.0) + jnp.log(1.0 + jnp.exp(-jnp.abs(v)))
    dec = jnp.exp(dt * a_ref[...])
    xdt = xs * _dot_x3(_rows8(dt), e64_ref[...], 0)[0:1]
    dec128 = _dot_x3(_rows8(dec), e128_ref[...], 0)
    pieces = []
    for c in range(D_INNER // 128):
        g = c // (SSD_HPG // 2)
        xcol = jnp.broadcast_to(xdt[:, c * 128:(c + 1) * 128], (128, 128)).T
        bg = jnp.broadcast_to(bm[:, g * SSD_STATE:(g + 1) * SSD_STATE], (128, SSD_STATE))
        decv = jnp.concatenate(
            [jnp.concatenate([dec128[:, hh * 128:(hh + 1) * 128]] * (SSD_HEAD_DIM // 8), axis=0)
             for hh in (2 * c, 2 * c + 1)], axis=0)
        hnew = decv * h_ref[0, c * 128:(c + 1) * 128, :] + xcol * bg
        hn_ref[0, c * 128:(c + 1) * 128, :] = hnew
        cg = _rows8(cm[:, g * SSD_STATE:(g + 1) * SSD_STATE]).astype(BF16)
        pieces.append(_dot_nt(cg, hnew.astype(BF16))[0:1])
    y = jnp.concatenate(pieces, axis=1) + dskip_ref[...] * xs
    yz = y * _silu(z_ref[0])
    ms = jnp.mean(yz * yz, axis=-1, keepdims=True)
    y_ref[0] = (yz * lax.rsqrt(ms + EPS) * nw_ref[...]).astype(y_ref.dtype)


def _ssd_step(xbc, prev, z, dt_raw, h0, w):
    bs = xbc.shape[0]
    a = -jnp.exp(w['ssd_a_log'].astype(F32))
    row = lambda wd: pl.BlockSpec((1, 1, wd), lambda i: (i, 0, 0))
    full = lambda shape: pl.BlockSpec(shape, lambda i: (0,) * len(shape))
    st = pl.BlockSpec((1, D_INNER, SSD_STATE), lambda i: (i, 0, 0))
    y, hn = pl.pallas_call(
        _ssd_step_kernel,
        grid=(bs,),
        in_specs=[row(SSD_CONV_DIM), pl.BlockSpec((1, SSD_CONV - 1, SSD_CONV_DIM), lambda i: (i, 0, 0)),
                  row(D_INNER), row(SSD_HEADS), st,
                  full((SSD_CONV, SSD_CONV_DIM)), full((1, SSD_CONV_DIM)), full((1, SSD_HEADS)), full((1, SSD_HEADS)),
                  full((1, D_INNER)), full((1, D_INNER)), full((SSD_HEADS, D_INNER)),
                  full((SSD_HEADS, SSD_HEADS * 128))],
        out_specs=[row(D_INNER), st],
        out_shape=[jax.ShapeDtypeStruct((bs, 1, D_INNER), BF16),
                   jax.ShapeDtypeStruct((bs, D_INNER, SSD_STATE), F32)],
        compiler_params=_cparams("parallel"),
        name="ssd_step",
    )(xbc.reshape(bs, 1, -1), prev, z.reshape(bs, 1, -1), dt_raw.reshape(bs, 1, -1),
      h0.reshape(bs, D_INNER, SSD_STATE), w['ssd_conv_w'], w['ssd_conv_b'].reshape(1, -1),
      w['ssd_dt_bias'].reshape(1, -1), a.reshape(1, -1),
      jnp.repeat(w['ssd_d'].astype(F32), SSD_HEAD_DIM).reshape(1, -1), w['ssd_norm'].reshape(1, -1),
      _head_expand(), _head_expand(128))
    return y.reshape(bs, D_INNER), hn.reshape(bs, SSD_HEADS, SSD_HEAD_DIM, SSD_STATE)


def _softmax_with_extra(s, s_new, valid=None):
    if valid is not None:
        s = jnp.where(valid, s, NEG)
    m = jnp.maximum(jnp.max(s, axis=-1, keepdims=True), s_new)
    p = jnp.exp(s - m)
    if valid is not None:
        p = jnp.where(valid, p, 0.0)
    p_new = jnp.exp(s_new - m)
    inv = 1.0 / (jnp.sum(p, axis=-1, keepdims=True) + p_new)
    return p * inv, p_new * inv


def _nsa_sample_kernel(*refs, npages, past_len):
    pt_ref = refs[0]
    del pt_ref
    q_ref, kv_ref, gate_ref = refs[1:4]
    skp = refs[4:4 + npages]
    svp = refs[4 + npages:4 + 2 * npages]
    (kc_ref, vc_ref, wk_ref, wv_ref, qg_ref, kg_ref, cos_ref, sin_ref, segm_ref, ov_ref, eblk_ref, fold_ref,
     foldt_ref, o_ref, skn_ref, wkn_ref, wko_ref, wvo_ref) = refs[4 + 2 * npages:]
    kvw = NSA_KV * NSA_HD
    cos, sin, segm = cos_ref[...], sin_ref[...], segm_ref[...]
    q = _rope(_head_rmsnorm(_rows8(q_ref[0]), qg_ref[...], segm), _rows8(cos), _rows8(sin)) * (NSA_HD ** -0.5)
    kv = kv_ref[0]
    sk_new = _rope(_head_rmsnorm(_rows8(kv[:, 2 * kvw:3 * kvw]), kg_ref[1:2, :], segm), _rows8(cos), _rows8(sin))[0:1]
    wk_new = _rope(_head_rmsnorm(_rows8(kv[:, 4 * kvw:5 * kvw]), kg_ref[2:3, :], segm), _rows8(cos), _rows8(sin))[0:1]
    sv_new = kv[:, 3 * kvw:4 * kvw]
    wv_new = kv[:, 5 * kvw:6 * kvw]
    skn_ref[0] = sk_new
    wkn_ref[0] = wk_new
    hrow = lax.broadcasted_iota(jnp.int32, (NSA_HEADS, NSA_HEADS * NSA_HD), 0)
    hcol = lax.broadcasted_iota(jnp.int32, (NSA_HEADS, NSA_HEADS * NSA_HD), 1)
    own = (hcol >> 6) == hrow
    q16 = jnp.where(own, jnp.concatenate([q, q], axis=0), 0.0).astype(BF16)
    qbd = _dot(q16, fold_ref[...])
    qbd_b = qbd.astype(BF16)
    ncmp = kc_ref.shape[1]
    kend = lax.broadcasted_iota(jnp.int32, (1, ncmp), 1) * CMP_STRIDE + (CMP_LEN - 1)
    valid_c = kend <= past_len
    s_c = jnp.where(valid_c, _dot_nt(qbd_b, kc_ref[0].astype(BF16)), NEG)
    m_c = jnp.max(s_c, axis=-1, keepdims=True)
    p_c = jnp.where(valid_c, jnp.exp(s_c - m_c), 0.0)
    den = jnp.sum(p_c, axis=-1, keepdims=True)
    p_c = p_c / jnp.where(den > 0, den, 1.0)
    o_c = _dot(p_c.astype(BF16), vc_ref[0].astype(BF16))
    gi = lax.broadcasted_iota(jnp.int32, (NSA_HEADS, NSA_HEADS), 0) // NSA_GQ
    gj = lax.broadcasted_iota(jnp.int32, (NSA_HEADS, NSA_HEADS), 1) // NSA_GQ
    psum = _dot_hi(jnp.where(gi == gj, 1.0, 0.0), p_c)
    psum = jnp.concatenate([psum, jnp.zeros((128 - NSA_HEADS, ncmp), F32)], axis=0)
    imp = lax.dot_general(ov_ref[...], psum, (((1,), (1,)), ((), ())), preferred_element_type=F32,
                          precision=HI)
    nslot = ov_ref.shape[0]
    jb = lax.broadcasted_iota(jnp.int32, (nslot, 1), 0)
    cur = past_len // SEL_BLOCK
    forced = (jb == 0) | (jb == cur) | (jb == cur - 1)
    score = jnp.where(jb * SEL_BLOCK <= past_len, imp + jnp.where(forced, SEL_FORCE, 0.0), NEG)
    bias = _topn_bias(score, SEL_TOPN).T[:NSA_HEADS]
    kmask = _dot(bias.astype(BF16), eblk_ref[...])
    sk_all = jnp.concatenate([r[...] for r in skp], axis=1).astype(BF16)
    sv_all = jnp.concatenate([r[...] for r in svp], axis=1).astype(BF16)
    s_s = _dot(qbd_b, sk_all) + kmask
    s_new = jnp.sum(qbd * sk_new, axis=-1, keepdims=True)
    p_s, p_new = _softmax_with_extra(s_s, s_new)
    o_s = _dot_nt(p_s.astype(BF16), sv_all) + p_new * sv_new
    wk = wk_ref[0]
    wv = wv_ref[0]
    s_w = _dot(qbd_b, wk.astype(BF16))
    s_wn = jnp.sum(qbd * wk_new, axis=-1, keepdims=True)
    p_w, p_wn = _softmax_with_extra(s_w, s_wn)
    o_w = _dot_nt(p_w.astype(BF16), wv.astype(BF16)) + p_wn * wv_new
    g16 = jnp.where((lax.broadcasted_iota(jnp.int32, (NSA_HEADS, 3 * NSA_HEADS), 1) // 3)
                    == lax.broadcasted_iota(jnp.int32, (NSA_HEADS, 3 * NSA_HEADS), 0),
                    jnp.broadcast_to(_sigmoid(gate_ref[0]), (NSA_HEADS, 3 * NSA_HEADS)), 0.0)
    br = lax.broadcasted_iota(jnp.int32, (NSA_HEADS, 3 * NSA_HEADS), 1) % 3
    gsel = lambda r: jnp.sum(jnp.where(br == r, g16, 0.0), axis=-1, keepdims=True)
    o16 = gsel(0) * o_c + gsel(1) * o_s + gsel(2) * o_w
    ox = _dot(o16.astype(BF16), foldt_ref[...])
    o_ref[0] = jnp.sum(jnp.where(own, ox, 0.0), axis=0, keepdims=True).astype(o_ref.dtype)
    nwin = wk.shape[1]
    lane = lax.broadcasted_iota(jnp.int32, (kvw, nwin), 1)

    def column(rowvec):
        col = jnp.broadcast_to(rowvec, (128, kvw)).T
        return jnp.concatenate([col] * (nwin // 128), axis=1)

    wko_ref[0] = jnp.where(lane == nwin - 1, column(wk_new), pltpu.roll(wk, nwin - 1, axis=1))
    wvo_ref[0] = jnp.where(lane == nwin - 1, column(wv_new), pltpu.roll(wv, nwin - 1, axis=1))


def _nsa_sample(q, kv, gates, pool_sk, pool_sv, page_table, kc, vc, win_k, win_v, w):
    bs, npages = page_table.shape
    page = pool_sk.shape[1]
    past_len = npages * page
    kvw = NSA_KV * NSA_HD
    nwin = win_k.shape[1]
    assert nwin == WINDOW and past_len % SEL_BLOCK == 0
    ncmp = kc.shape[1]
    nslot = 64
    assert past_len // SEL_BLOCK + 1 <= nslot
    cos128, sin128 = _rope_tables(jnp.full((1,), past_len, jnp.int32))
    overlap = _overlap_matrix(ncmp, nslot).T
    key_blk = np.arange(past_len) // SEL_BLOCK
    eblk = jnp.asarray((np.arange(nslot)[:, None] == key_blk[None, :]).astype(np.float32) * MASK_BIG, BF16)
    src = np.arange(NSA_HEADS * NSA_HD)
    dst = (src // NSA_HD // NSA_GQ) * NSA_HD + src % NSA_HD
    fold_np = np.zeros((NSA_HEADS * NSA_HD, kvw), np.float32)
    fold_np[src, dst] = 1.0
    fold = jnp.asarray(fold_np, BF16)
    foldt = jnp.asarray(fold_np.T, BF16)
    r3 = lambda t: t.reshape(bs, 1, -1)
    row = lambda wd: pl.BlockSpec((1, 1, wd), lambda b, pt: (b, 0, 0))
    full = lambda shape: pl.BlockSpec(shape, lambda b, pt: (0,) * len(shape))
    per_b = lambda r, c: pl.BlockSpec((1, r, c), lambda b, pt: (b, 0, 0))
    pages = [pl.BlockSpec((None, kvw, page), functools.partial(lambda b, pt, j: (pt[b, j], 0, 0), j=j))
             for j in range(npages)]
    pk3 = _pool_t(pool_sk)
    pv3 = _pool_t(pool_sv)
    gs = pltpu.PrefetchScalarGridSpec(
        num_scalar_prefetch=1, grid=(bs,),
        in_specs=[row(NSA_HEADS * NSA_HD), row(6 * kvw), row(3 * NSA_HEADS)] + pages + pages
        + [per_b(ncmp, kvw), per_b(ncmp, kvw), per_b(kvw, nwin), per_b(kvw, nwin),
           full((1, NSA_HEADS * NSA_HD)), full((3, kvw)), full((1, 128)), full((1, 128)), full((128, 128)),
           full((nslot, ncmp)), full((nslot, past_len)), full(fold.shape), full(foldt.shape)],
        out_specs=[row(NSA_HEADS * NSA_HD), row(kvw), row(kvw), per_b(kvw, nwin), per_b(kvw, nwin)])
    return pl.pallas_call(
        functools.partial(_nsa_sample_kernel, npages=npages, past_len=past_len),
        grid_spec=gs,
        out_shape=[jax.ShapeDtypeStruct((bs, 1, NSA_HEADS * NSA_HD), BF16),
                   jax.ShapeDtypeStruct((bs, 1, kvw), F32), jax.ShapeDtypeStruct((bs, 1, kvw), F32),
                   jax.ShapeDtypeStruct((bs, kvw, nwin), F32), jax.ShapeDtypeStruct((bs, kvw, nwin), F32)],
        compiler_params=_cparams("parallel"),
        name="nsa_sample",
    )(page_table, r3(q), r3(kv), r3(gates), *([pk3] * npages), *([pv3] * npages), kc, vc,
      _pool_t(win_k), _pool_t(win_v),
      jnp.tile(w['nsa_q_norm'], NSA_HEADS).reshape(1, -1), jnp.tile(w['nsa_k_norm'], (1, NSA_KV)),
      cos128, sin128, _segment_ones(), overlap, eblk, fold, foldt)


def _mem_sample_kernel(q_ref, mk_ref, mv_ref, g_ref, o_ref):
    q = q_ref[0]
    mtok = mk_ref.shape[1]
    rows = []
    for h in range(MEM_HEADS):
        qh = q[:, h * MEM_HD:(h + 1) * MEM_HD]
        ms = jnp.mean(qh * qh, axis=-1, keepdims=True)
        rows.append(qh * lax.rsqrt(ms + EPS) * g_ref[...] * (MEM_HD ** -0.5))
    q8 = jnp.concatenate(rows + rows, axis=0).astype(BF16)
    mk = mk_ref[0].reshape(mtok * MEM_HEADS, MEM_HD).astype(BF16)
    mv = mv_ref[0].reshape(mtok * MEM_HEADS, MEM_HD).astype(BF16)
    own = ((lax.broadcasted_iota(jnp.int32, (8, mtok * MEM_HEADS), 1) & (MEM_HEADS - 1))
           == (lax.broadcasted_iota(jnp.int32, (8, mtok * MEM_HEADS), 0) & (MEM_HEADS - 1)))
    s = jnp.where(own, _dot_nt(q8, mk), NEG)
    m = jnp.max(s, axis=-1, keepdims=True)
    p = jnp.exp(s - m)
    p = p / jnp.sum(p, axis=-1, keepdims=True)
    o8 = _dot(p.astype(BF16), mv)
    for h in range(MEM_HEADS):
        o_ref[0, :, h * MEM_HD:(h + 1) * MEM_HD] = o8[h:h + 1].astype(o_ref.dtype)


def _mem_sample(q_mem, mem_k, mem_v, qnorm):
    bs = q_mem.shape[0]
    width = MEM_HEADS * MEM_HD
    mtok = mem_k.shape[1]
    row = pl.BlockSpec((1, 1, width), lambda i: (i, 0, 0))
    mem = pl.BlockSpec((1, mtok, MEM_HEADS, MEM_HD), lambda i: (i, 0, 0, 0))
    return pl.pallas_call(
        _mem_sample_kernel,
        grid=(bs,),
        in_specs=[row, mem, mem, pl.BlockSpec((1, MEM_HD), lambda i: (0, 0))],
        out_specs=row,
        out_shape=jax.ShapeDtypeStruct((bs, 1, width), BF16),
        compiler_params=_cparams("parallel"),
        name="mem_sample",
    )(q_mem.reshape(bs, 1, width), mem_k, mem_v, qnorm.reshape(1, -1)).reshape(bs, width)


def _split_w_in(w_in):
    offs = np.cumsum((0,) + IN_SIZES)
    piece = lambda i: w_in[:, int(offs[i]):int(offs[i + 1])]
    small = jnp.concatenate([piece(2), piece(5)], axis=1)
    small = jnp.pad(small, ((0, 0), (0, 128 - small.shape[1])))
    return dict(z=piece(0), xbc=piece(1), small=small, q=piece(3), kv=piece(4), qmem=piece(6), gmerge=piece(7))


def _multi_mm_kernel(*refs, n_out):
    x_ref, g_ref = refs[:2]
    w_refs = refs[2:2 + n_out]
    o_refs = refs[2 + n_out:]
    x = x_ref[...]
    ms = jnp.mean(x * x, axis=-1, keepdims=True)
    xn = (x * lax.rsqrt(ms + EPS) * g_ref[...]).astype(BF16)
    for w_ref, o_ref in zip(w_refs, o_refs):
        o_ref[...] = _dot(xn, w_ref[...]).astype(o_ref.dtype)


def _multi_matmul(x, g, weights, tm=256):
    m, k = x.shape
    tm = min(tm, m)
    assert m % tm == 0
    const = lambda shape: pl.BlockSpec(shape, lambda i: (0, 0), pipeline_mode=pl.Buffered(1))
    outs = pl.pallas_call(
        functools.partial(_multi_mm_kernel, n_out=len(weights)),
        grid=(m // tm,),
        in_specs=[pl.BlockSpec((tm, k), lambda i: (i, 0)), const((1, k))] + [const(wm.shape) for wm in weights],
        out_specs=[pl.BlockSpec((tm, wm.shape[1]), lambda i: (i, 0)) for wm in weights],
        out_shape=[jax.ShapeDtypeStruct((m, wm.shape[1]), F32) for wm in weights],
        compiler_params=_cparams("parallel"),
        name="in_proj",
    )(x, g.reshape(1, k).astype(F32), *weights)
    return outs


def _in_proj(x, norm_w, wi):
    names = list(wi)
    proj = dict(zip(names, _multi_matmul(x, norm_w, [wi[n] for n in names])))
    proj['dt'] = proj['small'][:, :SSD_HEADS]
    proj['gnsa'] = proj['small'][:, SSD_HEADS:SSD_HEADS + 3 * NSA_HEADS]
    return proj


def _prompt_layer(x_prompt, mem_prompt, w, wb, wi):
    b, length, _ = x_prompt.shape
    n = b * length
    kvw = NSA_KV * NSA_HD
    x = x_prompt.reshape(n, D_MODEL)
    proj = _in_proj(x, w['norm_mix'], wi)
    xbc = proj['xbc'].reshape(b, length, SSD_CONV_DIM)
    y_ssd, ssd_state = _ssd_prompt(xbc, proj['z'].reshape(b, length, D_INNER),
                                   proj['dt'].reshape(b, length, SSD_HEADS), w)
    kv = proj['kv']
    qh, skaug, skf, svh, wkh, wkf, wvh = _nsa_prep(proj['q'], kv, b, length, w)
    kc = _compress_prompt(kv, 0, b, length, w['cmp_k_pe'], w['cmp_k_w1'], w['cmp_k_w2'], w['nsa_k_norm'][0], True)
    vc = _compress_prompt(kv, 1, b, length, w['cmp_v_pe'], w['cmp_v_w1'], w['cmp_v_w2'], w['nsa_k_norm'][0], False)
    to_heads = lambda t: t.reshape(b, -1, NSA_KV, NSA_HD).transpose(0, 2, 1, 3).astype(BF16)
    gates = proj['gnsa'].reshape(b, length, NSA_KV, 3 * NSA_GQ).transpose(0, 2, 1, 3)
    o_nsa = _nsa_prompt(qh, to_heads(kc), to_heads(vc), skaug, svh, wkh, wvh, gates, b, length)
    mem = mem_prompt.reshape(-1, D_MODEL)
    mkv = _matmul(mem, wb['w_mem_kv'], w['mem_norm'], tm=256)
    mtok = mem_prompt.shape[1]
    mk = mkv[:, :MEM_HEADS * MEM_HD].reshape(b, mtok, MEM_HEADS, MEM_HD)
    mk = mk * lax.rsqrt(jnp.mean(mk * mk, axis=-1, keepdims=True) + EPS) * w['mem_k_norm']
    mv = mkv[:, MEM_HEADS * MEM_HD:].reshape(b, mtok, MEM_HEADS, MEM_HD)
    o_mem = _mem_prompt(proj['qmem'], mk.reshape(b, mtok, -1).astype(BF16), mv.reshape(b, mtok, -1).astype(BF16),
                        w['mem_q_norm'], b, length)
    x1 = _merge(x, y_ssd.reshape(n, D_INNER), o_nsa, o_mem, proj['gmerge'], wb)
    y, u_last = _ffn_prompt(x1, w, wb, b, length)
    r4 = lambda t: t.reshape(b, length, NSA_KV, NSA_HD)
    keep = min(WINDOW, length)
    state = (r4(kv[:, 0:kvw]), r4(kv[:, kvw:2 * kvw]), r4(skf), r4(kv[:, 3 * kvw:4 * kvw]),
             r4(wkf)[:, -keep:], r4(kv[:, 5 * kvw:6 * kvw])[:, -keep:],
             xbc[:, -(SSD_CONV - 1):], ssd_state, u_last[:, -(FFN_CONV - 1):], mk, mv)
    return y.reshape(b, length, D_MODEL), state


def _sample_layer(x_sample, caches, page_table, w, wb, wi):
    (pool_ck, pool_cv, pool_sk, pool_sv, win_k, win_v, conv_prev, ssd_h0, mem_k, mem_v, ffn_prev) = caches
    bs = x_sample.shape[0]
    kvw = NSA_KV * NSA_HD
    x = x_sample.reshape(bs, D_MODEL)
    proj = _in_proj(x, w['norm_mix'], wi)
    y_ssd, ssd_state = _ssd_step(proj['xbc'], conv_prev, proj['z'], proj['dt'], ssd_h0, w)
    kv = proj['kv']
    kc = _compress_sample(pool_ck, page_table, w['cmp_k_pe'], w['cmp_k_w1'], w['cmp_k_w2'], w['nsa_k_norm'][0], True)
    vc = _compress_sample(pool_cv, page_table, w['cmp_v_pe'], w['cmp_v_w1'], w['cmp_v_w2'], w['nsa_k_norm'][0], False)
    o_nsa, sk_new, wk_new, win_k_new, win_v_new = _nsa_sample(
        proj['q'], kv, proj['gnsa'], pool_sk, pool_sv, page_table, kc, vc, win_k, win_v, w)
    o_mem = _mem_sample(proj['qmem'], mem_k, mem_v, w['mem_q_norm'])
    x1 = _merge(x, y_ssd, o_nsa.reshape(bs, -1), o_mem, proj['gmerge'], wb)
    y, u = _ffn_sample(x1, ffn_prev, w, wb)
    r4 = lambda t: t.reshape(bs, -1, NSA_KV, NSA_HD)
    from_t = lambda t: t.reshape(bs, NSA_KV, NSA_HD, -1).transpose(0, 3, 1, 2)
    state = (r4(kv[:, 0:kvw]), r4(kv[:, kvw:2 * kvw]), r4(sk_new), r4(kv[:, 3 * kvw:4 * kvw]),
             from_t(win_k_new), from_t(win_v_new),
             jnp.concatenate([conv_prev[:, 1:], proj['xbc'][:, None]], axis=1), ssd_state,
             jnp.concatenate([ffn_prev[:, 1:], u[:, None]], axis=1))
    return y.reshape(bs, 1, D_MODEL), state


def kernel(x_prompt, x_sample, cache_nsa_cmp_k, cache_nsa_cmp_v, cache_nsa_sel_k, cache_nsa_sel_v, state_nsa_win_k, state_nsa_win_v, state_ssd_conv, state_ssd, cache_mem_k, cache_mem_v, state_ffn_conv, page_table, mem_prompt, norm_mix, w_in, b_merge, ssd_conv_w, ssd_conv_b, ssd_dt_bias, ssd_a_log, ssd_d, ssd_norm, w_ssd_o, nsa_q_norm, nsa_k_norm, cmp_k_pe, cmp_k_w1, cmp_k_w2, cmp_v_pe, cmp_v_w1, cmp_v_w2, w_nsa_o, mem_norm, w_mem_kv, mem_q_norm, mem_k_norm, w_mem_o, w_out, norm_ffn, w_up, ffn_conv_w, ffn_conv_b, w_down):
    weights = dict(norm_mix=norm_mix, w_in=w_in, b_merge=b_merge, ssd_conv_w=ssd_conv_w,
                   ssd_conv_b=ssd_conv_b, ssd_dt_bias=ssd_dt_bias, ssd_a_log=ssd_a_log, ssd_d=ssd_d,
                   ssd_norm=ssd_norm, w_ssd_o=w_ssd_o, nsa_q_norm=nsa_q_norm, nsa_k_norm=nsa_k_norm,
                   cmp_k_pe=cmp_k_pe, cmp_k_w1=cmp_k_w1, cmp_k_w2=cmp_k_w2, cmp_v_pe=cmp_v_pe,
                   cmp_v_w1=cmp_v_w1, cmp_v_w2=cmp_v_w2, w_nsa_o=w_nsa_o, mem_norm=mem_norm,
                   w_mem_kv=w_mem_kv, mem_q_norm=mem_q_norm, mem_k_norm=mem_k_norm, w_mem_o=w_mem_o,
                   w_out=w_out, norm_ffn=norm_ffn, w_up=w_up, ffn_conv_w=ffn_conv_w,
                   ffn_conv_b=ffn_conv_b, w_down=w_down)
    w = {name: arr[0] for name, arr in weights.items()}
    wb = dict(w)
    for name in ('w_ssd_o', 'w_nsa_o', 'w_mem_o', 'w_out', 'w_up', 'w_down', 'w_mem_kv'):
        wb[name] = w[name].astype(BF16)
    wi = {name: piece.astype(BF16) for name, piece in _split_w_in(w['w_in']).items()}
    y_p, st_p = _prompt_layer(x_prompt, mem_prompt, w, wb, wi)
    caches = (cache_nsa_cmp_k[0], cache_nsa_cmp_v[0], cache_nsa_sel_k[0], cache_nsa_sel_v[0], state_nsa_win_k[0],
              state_nsa_win_v[0], state_ssd_conv[0], state_ssd[0], cache_mem_k[0], cache_mem_v[0], state_ffn_conv[0])
    y_s, st_s = _sample_layer(x_sample, caches, page_table, w, wb, wi)
    return (y_p, y_s) + tuple(s[None] for s in st_p) + tuple(s[None] for s in st_s)
```

```python
import functools
import math

import numpy as np
import jax
import jax.numpy as jnp
from jax import lax
from jax.experimental import pallas as pl
from jax.experimental.pallas import tpu as pltpu

D_MODEL = 1024
D_INNER = 2048
SSD_HEAD_DIM = 64
SSD_HEADS = 32
SSD_GROUPS = 4
SSD_HPG = 8
SSD_STATE = 128
SSD_CONV = 4
SSD_CONV_DIM = 3072
SSD_CHUNK = 128
NSA_HEADS = 16
NSA_KV = 4
NSA_GQ = 4
NSA_HD = 64
CMP_LEN = 32
CMP_STRIDE = 16
CMP_HIDDEN = 128
SEL_BLOCK = 64
SEL_TOPN = 16
SEL_FORCE = 1.0e4
WINDOW = 512
QUERY_BLOCK = 128
MEM_HEADS = 4
MEM_HD = 256
D_FF = 2816
FFN_CONV = 3
ROPE_THETA = 10000.0
EPS = 1e-6
IN_SIZES = (2048, 3072, 32, 1024, 1536, 48, 1024, 3072)

VMEM_LIMIT = 48 * 1024 * 1024
MASK_BIG = 2.0 ** 20
NEG = -1.0e30

F32 = jnp.float32
BF16 = jnp.bfloat16
HI = lax.Precision.HIGHEST


def _cparams(*sem):
    return pltpu.CompilerParams(dimension_semantics=sem, vmem_limit_bytes=VMEM_LIMIT)


def _sigmoid(x):
    return 1.0 / (1.0 + jnp.exp(-x))


def _silu(x):
    return x * _sigmoid(x)


def _dot(a, b):
    return jnp.dot(a, b, preferred_element_type=F32)


def _dot_nt(a, b):
    return lax.dot_general(a, b, (((1,), (1,)), ((), ())), preferred_element_type=F32)


def _dot_hi(a, b):
    return jnp.dot(a, b, preferred_element_type=F32, precision=HI)


def _split3(x):
    hi = x.astype(BF16)
    r1 = x - hi.astype(F32)
    mid = r1.astype(BF16)
    lo = (r1 - mid.astype(F32)).astype(BF16)
    return hi, mid, lo


def _dot_x3(a, b, split):
    if split == 0:
        parts = [_dot(t, b) for t in _split3(a)]
    else:
        parts = [_dot(a, t) for t in _split3(b)]
    return parts[0] + parts[1] + parts[2]


def _mm_kernel(x_ref, g_ref, w_ref, o_ref, xn_ref, *, norm):
    @pl.when(pl.program_id(1) == 0)
    def _():
        x = x_ref[...].astype(F32)
        if norm:
            ms = jnp.mean(x * x, axis=-1, keepdims=True)
            x = x * lax.rsqrt(ms + EPS) * g_ref[...]
        xn_ref[...] = x.astype(BF16)

    o_ref[...] = _dot(xn_ref[...], w_ref[...]).astype(o_ref.dtype)


def _matmul(x, w, g=None, out_dtype=F32, tm=512, tn=512):
    m, k = x.shape
    n = w.shape[1]
    tm = min(tm, m)
    tn = min(tn, n)
    assert m % tm == 0 and n % tn == 0, (m, n, tm, tn)
    norm = g is not None
    if g is None:
        g = jnp.ones((1, k), F32)
    return pl.pallas_call(
        functools.partial(_mm_kernel, norm=norm),
        grid=(m // tm, n // tn),
        in_specs=[pl.BlockSpec((tm, k), lambda i, j: (i, 0)),
                  pl.BlockSpec((1, k), lambda i, j: (0, 0)),
                  pl.BlockSpec((k, tn), lambda i, j: (0, j))],
        out_specs=pl.BlockSpec((tm, tn), lambda i, j: (i, j)),
        out_shape=jax.ShapeDtypeStruct((m, n), out_dtype),
        scratch_shapes=[pltpu.VMEM((tm, k), BF16)],
        compiler_params=_cparams("parallel", "arbitrary"),
        name="rms_matmul" if norm else "matmul",
    )(x, g.reshape(1, k).astype(F32), w)


def _shift_rows(x, k, prev, row):
    r = pltpu.roll(x, k, axis=0)
    head = r[:8]
    for i in range(k):
        head = jnp.where(row == i, prev[8 - k + i:8 - k + i + 1, :], head)
    return jnp.concatenate([head, r[8:]], axis=0)


def _ssd_prompt_kernel(xbc_ref, z_ref, dt_ref, dtT_ref, cw_ref, cb_ref, dtb_ref, dtbT_ref, a_ref, aT_ref,
                       dskip_ref, nw_ref, e_ref, y_ref, hT_ref, carry_ref, h_ref, yacc_ref):
    c = pl.program_id(1)
    q = SSD_CHUNK

    @pl.when(c == 0)
    def _():
        carry_ref[...] = jnp.zeros_like(carry_ref)
        h_ref[...] = jnp.zeros_like(h_ref)

    xbc = xbc_ref[0]
    row = lax.broadcasted_iota(jnp.int32, (8, 1), 0)
    prev = carry_ref[...]
    conv = cb_ref[...] + cw_ref[3:4, :] * xbc
    for k in range(1, SSD_CONV):
        conv = conv + cw_ref[3 - k:4 - k, :] * _shift_rows(xbc, k, prev, row)
    carry_ref[...] = xbc[q - 8:, :]
    xc = _silu(conv)
    xs = xc[:, :D_INNER]
    bm = xc[:, D_INNER:D_INNER + SSD_GROUPS * SSD_STATE].astype(BF16)
    cm = xc[:, D_INNER + SSD_GROUPS * SSD_STATE:].astype(BF16)

    def softplus(v):
        return jnp.maximum(v, 0.0) + jnp.log(1.0 + jnp.exp(-jnp.abs(v)))

    dt = softplus(dt_ref[0] + dtb_ref[...])
    dtT = softplus(dtT_ref[0] + dtbT_ref[...])
    ii = lax.broadcasted_iota(jnp.int32, (q, q), 0)
    jj = lax.broadcasted_iota(jnp.int32, (q, q), 1)
    causal = ii >= jj
    cum = _dot_x3(jnp.where(causal, 1.0, 0.0).astype(BF16), dt * a_ref[...], 1)
    cumT = _dot_x3(dtT * aT_ref[...], jnp.where(jj >= ii, 1.0, 0.0).astype(BF16), 0)
    ecum = jnp.exp(cum)
    dend = jnp.exp(cum[q - 1:q, :] - cum)
    spread = _dot_x3(jnp.concatenate([dt, ecum, dend], axis=0), e_ref[...], 0)
    dt_x, ecum_x, dend_x = spread[:q], spread[q:2 * q], spread[2 * q:]
    xdt = xs * dt_x
    xdt_b = xdt.astype(BF16)
    xw_b = (xdt * dend_x).astype(BF16)
    gw = SSD_HPG * SSD_HEAD_DIM
    for g in range(SSD_GROUPS):
        bg = bm[:, g * SSD_STATE:(g + 1) * SSD_STATE]
        cg = cm[:, g * SSD_STATE:(g + 1) * SSD_STATE]
        cb = _dot_nt(cg, bg)
        h_prev = h_ref[g]
        yoff = _dot(cg, h_prev.astype(BF16)) * ecum_x[:, g * gw:(g + 1) * gw]
        st = _dot(bg.astype(F32).T.astype(BF16), xw_b[:, g * gw:(g + 1) * gw])
        h_ref[g] = h_prev * ecum_x[q - 1:q, g * gw:(g + 1) * gw] + st
        for eh in range(SSD_HPG):
            hh = g * SSD_HPG + eh
            seg = cum[:, hh:hh + 1] - cumT[hh:hh + 1, :]
            decay = jnp.exp(jnp.where(causal, seg, NEG))
            mm = (cb * decay).astype(BF16)
            lo = hh * SSD_HEAD_DIM
            yd = _dot(mm, xdt_b[:, lo:lo + SSD_HEAD_DIM])
            yacc_ref[:, lo:lo + SSD_HEAD_DIM] = yd + yoff[:, eh * SSD_HEAD_DIM:(eh + 1) * SSD_HEAD_DIM]
    y = yacc_ref[...] + dskip_ref[...] * xs
    yz = y * _silu(z_ref[0])
    ms = jnp.mean(yz * yz, axis=-1, keepdims=True)
    y_ref[0] = (yz * lax.rsqrt(ms + EPS) * nw_ref[...]).astype(y_ref.dtype)

    @pl.when(c == pl.num_programs(1) - 1)
    def _():
        hT_ref[0] = h_ref[...]


def _head_expand(width=SSD_HEAD_DIM):
    e = np.zeros((SSD_HEADS, SSD_HEADS * width), np.float32)
    for h in range(SSD_HEADS):
        e[h, h * width:(h + 1) * width] = 1.0
    return jnp.asarray(e, BF16)


def _ssd_prompt(xbc, z, dt_raw, w):
    b, length, _ = xbc.shape
    q = SSD_CHUNK
    nc = length // q
    dtT = jnp.swapaxes(dt_raw, 1, 2)
    a = -jnp.exp(w['ssd_a_log'].astype(F32))
    full = lambda shape: pl.BlockSpec(shape, lambda i, j: (0,) * len(shape))
    y, hT = pl.pallas_call(
        _ssd_prompt_kernel,
        grid=(b, nc),
        in_specs=[pl.BlockSpec((1, q, SSD_CONV_DIM), lambda i, j: (i, j, 0)),
                  pl.BlockSpec((1, q, D_INNER), lambda i, j: (i, j, 0)),
                  pl.BlockSpec((1, q, SSD_HEADS), lambda i, j: (i, j, 0)),
                  pl.BlockSpec((1, SSD_HEADS, q), lambda i, j: (i, 0, j)),
                  full((SSD_CONV, SSD_CONV_DIM)), full((1, SSD_CONV_DIM)),
                  full((1, SSD_HEADS)), full((SSD_HEADS, 1)), full((1, SSD_HEADS)), full((SSD_HEADS, 1)),
                  full((1, D_INNER)), full((1, D_INNER)), full((SSD_HEADS, D_INNER))],
        out_specs=[pl.BlockSpec((1, q, D_INNER), lambda i, j: (i, j, 0)),
                   pl.BlockSpec((1, SSD_GROUPS, SSD_STATE, SSD_HPG * SSD_HEAD_DIM), lambda i, j: (i, 0, 0, 0))],
        out_shape=[jax.ShapeDtypeStruct((b, length, D_INNER), BF16),
                   jax.ShapeDtypeStruct((b, SSD_GROUPS, SSD_STATE, SSD_HPG * SSD_HEAD_DIM), F32)],
        scratch_shapes=[pltpu.VMEM((8, SSD_CONV_DIM), F32),
                        pltpu.VMEM((SSD_GROUPS, SSD_STATE, SSD_HPG * SSD_HEAD_DIM), F32),
                        pltpu.VMEM((q, D_INNER), F32)],
        compiler_params=_cparams("parallel", "arbitrary"),
        name="ssd_prompt",
    )(xbc, z, dt_raw, dtT, w['ssd_conv_w'], w['ssd_conv_b'].reshape(1, -1),
      w['ssd_dt_bias'].reshape(1, -1), w['ssd_dt_bias'].reshape(-1, 1), a.reshape(1, -1), a.reshape(-1, 1),
      jnp.repeat(w['ssd_d'].astype(F32), SSD_HEAD_DIM).reshape(1, -1), w['ssd_norm'].reshape(1, -1),
      _head_expand())
    state = hT.reshape(b, SSD_GROUPS, SSD_STATE, SSD_HPG, SSD_HEAD_DIM).transpose(0, 1, 3, 4, 2)
    return y, state.reshape(b, SSD_HEADS, SSD_HEAD_DIM, SSD_STATE)


def _segment_ones():
    i = np.arange(128)
    return jnp.asarray((i[:, None] // NSA_HD == i[None, :] // NSA_HD).astype(np.float32), BF16)


def _head_rmsnorm(x, g_row, segm):
    sq = x * x
    hi = sq.astype(BF16)
    lo = (sq - hi.astype(F32)).astype(BF16)
    parts = []
    for c in range(x.shape[1] // 128):
        sl = slice(c * 128, (c + 1) * 128)
        parts.append(_dot(hi[:, sl], segm) + _dot(lo[:, sl], segm))
    ss = parts[0] if len(parts) == 1 else jnp.concatenate(parts, axis=1)
    return x * lax.rsqrt(ss * (1.0 / NSA_HD) + EPS) * g_row


def _tile_lanes(t, width):
    reps = width // t.shape[1]
    return t if reps == 1 else jnp.concatenate([t] * reps, axis=1)


def _rope(x, cos128, sin128):
    width = x.shape[1]
    lane = lax.broadcasted_iota(jnp.int32, x.shape, 1)
    first = (lane & (NSA_HD // 2)) == 0
    rot = jnp.where(first, pltpu.roll(x, width - NSA_HD // 2, axis=1), pltpu.roll(x, NSA_HD // 2, axis=1))
    return x * _tile_lanes(cos128, width) + rot * _tile_lanes(sin128, width)


def _rope_tables(pos):
    half = NSA_HD // 2
    inv = ROPE_THETA ** (-jnp.arange(half, dtype=F32) / half)
    ang = pos.astype(F32)[:, None] * inv[None, :]
    cos, sin = jnp.cos(ang), jnp.sin(ang)
    cos128 = jnp.concatenate([cos, cos, cos, cos], axis=1)
    sin128 = jnp.concatenate([-sin, sin, -sin, sin], axis=1)
    return cos128, sin128


def _norm_rope_t(x, g_col, cos, sin):
    nh = x.shape[0] // NSA_HD
    x3 = x.reshape(nh, NSA_HD, x.shape[1])
    ms = jnp.sum(x3 * x3, axis=1, keepdims=True) * (1.0 / NSA_HD)
    x3 = x3 * lax.rsqrt(ms + EPS) * g_col[None]
    half = NSA_HD // 2
    x1, x2 = x3[:, :half], x3[:, half:]
    out = jnp.concatenate([x1 * cos[None] - x2 * sin[None], x2 * cos[None] + x1 * sin[None]], axis=1)
    return out.reshape(x.shape)


def _nsa_prep_kernel(q_ref, kv_ref, cos_ref, sin_ref, qg_ref, kg_ref,
                     qt_ref, skaug_ref, skf_ref, svt_ref, wkr_ref, wkf_ref, wvt_ref):
    tl = q_ref.shape[2]
    i = pl.program_id(1)
    cos, sin = cos_ref[...], sin_ref[...]
    kvw = NSA_KV * NSA_HD
    qt_ref[0] = (_norm_rope_t(q_ref[0], qg_ref[...], cos, sin) * (NSA_HD ** -0.5 * math.log2(math.e))).astype(BF16)
    sk = _norm_rope_t(kv_ref[0, 2 * kvw:3 * kvw, :], kg_ref[:, 1:2], cos, sin)
    wk = _norm_rope_t(kv_ref[0, 4 * kvw:5 * kvw, :], kg_ref[:, 2:3], cos, sin)
    skf_ref[0] = sk
    wkf_ref[0] = wk
    svt_ref[0] = kv_ref[0, 3 * kvw:4 * kvw, :].astype(BF16)
    wvt_ref[0] = kv_ref[0, 5 * kvw:6 * kvw, :].astype(BF16)
    pos = i * tl + lax.broadcasted_iota(jnp.int32, (tl, NSA_HD), 0)
    blk = lax.broadcasted_iota(jnp.int32, (tl, NSA_HD), 1)
    onehot = jnp.where((pos >> 6) == blk, MASK_BIG, 0.0).astype(BF16)
    for k in range(NSA_KV):
        sl = slice(k * NSA_HD, (k + 1) * NSA_HD)
        skaug_ref[0, k] = jnp.concatenate([sk[sl].T.astype(BF16), onehot], axis=1)
        wkr_ref[0, k] = wk[sl].T.astype(BF16)


def _nsa_prep(qt, kvt, w, tl=256):
    b, _, length = qt.shape
    nl = length // tl
    half = NSA_HD // 2
    inv = ROPE_THETA ** (-jnp.arange(half, dtype=F32) / half)
    ang = inv[:, None] * jnp.arange(length, dtype=F32)[None, :]
    kvw = NSA_KV * NSA_HD
    fm = lambda rows: pl.BlockSpec((1, rows, tl), lambda bi, i: (bi, 0, i))
    tab = pl.BlockSpec((half, tl), lambda bi, i: (0, i))
    full = lambda shape: pl.BlockSpec(shape, lambda bi, i: (0,) * len(shape))
    rows = lambda wd: pl.BlockSpec((1, NSA_KV, tl, wd), lambda bi, i: (bi, 0, i, 0))
    return pl.pallas_call(
        _nsa_prep_kernel,
        grid=(b, nl),
        in_specs=[fm(NSA_HEADS * NSA_HD), fm(6 * kvw), tab, tab, full((NSA_HD, 1)), full((NSA_HD, 3))],
        out_specs=[fm(NSA_HEADS * NSA_HD), rows(2 * NSA_HD), fm(kvw), fm(kvw), rows(NSA_HD), fm(kvw), fm(kvw)],
        out_shape=[jax.ShapeDtypeStruct((b, NSA_HEADS * NSA_HD, length), BF16),
                   jax.ShapeDtypeStruct((b, NSA_KV, length, 2 * NSA_HD), BF16),
                   jax.ShapeDtypeStruct((b, kvw, length), F32),
                   jax.ShapeDtypeStruct((b, kvw, length), BF16),
                   jax.ShapeDtypeStruct((b, NSA_KV, length, NSA_HD), BF16),
                   jax.ShapeDtypeStruct((b, kvw, length), F32),
                   jax.ShapeDtypeStruct((b, kvw, length), BF16)],
        compiler_params=_cparams("parallel", "parallel"),
        name="nsa_prep",
    )(qt, kvt, jnp.cos(ang), jnp.sin(ang), w['nsa_q_norm'].reshape(-1, 1), w['nsa_k_norm'].T)


def _compress_kernel(*refs, n_x, is_k, n_prefetch):
    refs = refs[n_prefetch:]
    page_refs = refs[:n_x]
    wbd_ref, pe_ref, w2_ref, g_ref, cos_ref, sin_ref, segm_ref, o_ref, x0_ref, x1_ref = refs[n_x:]
    plen = page_refs[0].shape[-1]
    for j, r in enumerate(page_refs):
        rows = r[...].T
        x0_ref[j * plen:(j + 1) * plen, :] = rows[:, :128]
        x1_ref[j * plen:(j + 1) * plen, :] = rows[:, 128:]
    nchunk = x0_ref.shape[0] // CMP_STRIDE
    row = lax.broadcasted_iota(jnp.int32, (nchunk, 1), 0)
    outs = []
    for x_ref in (x0_ref, x1_ref):
        x = jnp.concatenate([x_ref[pl.ds(s, nchunk, stride=CMP_STRIDE), :].astype(BF16)
                             for s in range(CMP_STRIDE)], axis=1)
        acc = _dot(x, wbd_ref[...])
        hid = []
        for kl in range(2):
            lo = kl * 2 * CMP_HIDDEN
            first = acc[:, lo:lo + CMP_HIDDEN]
            second = pltpu.roll(acc[:, lo + CMP_HIDDEN:lo + 2 * CMP_HIDDEN], nchunk - 1, axis=0)
            hid.append(_silu(first + second + pe_ref[...]))
        outs.append(_dot(jnp.concatenate(hid, axis=1).astype(BF16), w2_ref[...]))
    out = jnp.concatenate(outs, axis=1)
    if is_k:
        out = _rope(_head_rmsnorm(out, g_ref[...], segm_ref[...]), cos_ref[...], sin_ref[...])
    o_ref[0] = jnp.where(row < nchunk - 1, out, 0.0)


def _compress_weights(pe, w1, w2):
    w1r = w1.reshape(CMP_LEN, NSA_HD, CMP_HIDDEN)
    pe_term = jnp.einsum('ld,ldm->m', pe, w1r, precision=HI).reshape(1, CMP_HIDDEN)
    both = jnp.concatenate([w1r[:CMP_STRIDE], w1r[CMP_STRIDE:]], axis=-1)
    zero = jnp.zeros_like(both)
    wbd = jnp.concatenate([jnp.concatenate([both, zero], axis=-1), jnp.concatenate([zero, both], axis=-1)], axis=1)
    zero2 = jnp.zeros_like(w2)
    w2bd = jnp.concatenate([jnp.concatenate([w2, zero2], axis=-1), jnp.concatenate([zero2, w2], axis=-1)], axis=0)
    wbd = wbd.reshape(CMP_STRIDE * 2 * NSA_HD, 4 * CMP_HIDDEN)
    return wbd.astype(BF16), pe_term, w2bd.astype(BF16)


def _compress_call(x_args, x_specs, grid, nchunk, batch, pe, w1, w2, knorm, is_k, n_prefetch=0, prefetch=()):
    wbd, pe_term, w2bd = _compress_weights(pe, w1, w2)
    cos128, sin128 = _rope_tables(jnp.arange(nchunk, dtype=jnp.int32) * CMP_STRIDE + (CMP_LEN - 1))
    kvw = NSA_KV * NSA_HD
    full = lambda shape: pl.BlockSpec(shape, lambda *a: (0,) * len(shape))
    gs = pltpu.PrefetchScalarGridSpec(
        num_scalar_prefetch=n_prefetch, grid=grid,
        in_specs=list(x_specs) + [full(wbd.shape), full((1, CMP_HIDDEN)), full(w2bd.shape), full((1, kvw)),
                                  full((nchunk, 128)), full((nchunk, 128)), full((128, 128))],
        out_specs=pl.BlockSpec((1, nchunk, kvw), lambda i, *a: (i, 0, 0)),
        scratch_shapes=[pltpu.VMEM((nchunk * CMP_STRIDE, 128), F32)] * 2)
    return pl.pallas_call(
        functools.partial(_compress_kernel, n_x=len(x_specs), is_k=is_k, n_prefetch=n_prefetch),
        grid_spec=gs,
        out_shape=jax.ShapeDtypeStruct((batch, nchunk, kvw), F32),
        compiler_params=_cparams("parallel"),
        name="compress_k" if is_k else "compress_v",
    )(*prefetch, *x_args, wbd, pe_term, w2bd, jnp.tile(knorm, NSA_KV).reshape(1, -1), cos128, sin128,
      _segment_ones())


def _compress_prompt(kvt, col, pe, w1, w2, knorm, is_k, page=128):
    b, _, length = kvt.shape
    npages = length // page
    specs = [pl.BlockSpec((None, NSA_KV * NSA_HD, page), functools.partial(lambda i, j: (i, col, j), j=j))
             for j in range(npages)]
    return _compress_call([kvt] * npages, specs, (b,), length // CMP_STRIDE, b, pe, w1, w2, knorm, is_k)


SEL_TILE = 512


def _flash_step(carry, s, vt):
    m, l, acc = carry
    m_new = jnp.maximum(m, jnp.max(s, axis=0, keepdims=True))
    alpha = jnp.exp2(m - m_new)
    p = jnp.exp2(s - m_new)
    l = alpha * l + jnp.sum(p, axis=0, keepdims=True)
    acc = alpha * acc + _dot(vt, p.astype(BF16))
    return m_new, l, acc


def _topn_bias(score, n):
    nblk, nq = score.shape
    groups = [score[8 * v:8 * v + 8] for v in range(nblk // 8)]
    sub = lax.broadcasted_iota(jnp.int32, (8, nq), 0)
    cnt = [jnp.zeros((8, nq), F32) for _ in groups]
    for jp in range(nblk):
        row = score[jp:jp + 1]
        for v, grp in enumerate(groups):
            if v < jp // 8:
                inc = jnp.where(row > grp, 1.0, 0.0)
            elif v > jp // 8:
                inc = jnp.where(row >= grp, 1.0, 0.0)
            else:
                inc = jnp.where(sub > jp % 8, jnp.where(row >= grp, 1.0, 0.0), jnp.where(row > grp, 1.0, 0.0))
            cnt[v] = cnt[v] + inc
    return jnp.concatenate([jnp.where(c < n, 0.0, -1.0) for c in cnt], axis=0)


def _nsa_prompt_kernel(q_ref, kc_ref, vc_ref, sk_ref, sv_ref, wk_ref, wv_ref, g_ref, ov_ref, o_ref):
    qb = pl.program_id(2)
    qlen = QUERY_BLOCK
    cols = NSA_GQ * qlen
    t0 = qb * qlen
    qt = jnp.concatenate([q_ref[0, g * NSA_HD:(g + 1) * NSA_HD, :] for g in range(NSA_GQ)], axis=1)
    tcol = t0 + (lax.broadcasted_iota(jnp.int32, (1, cols), 1) & (qlen - 1))
    ncmp = kc_ref.shape[2]
    s = _dot(kc_ref[0, 0], qt)
    kend = lax.broadcasted_iota(jnp.int32, (ncmp, 1), 0) * CMP_STRIDE + (CMP_LEN - 1)
    valid = kend <= tcol
    s = jnp.where(valid, s, NEG)
    m = jnp.max(s, axis=0, keepdims=True)
    p = jnp.where(valid, jnp.exp2(s - m), 0.0)
    den = jnp.sum(p, axis=0, keepdims=True)
    p = p / jnp.where(den > 0, den, 1.0)
    o_c = _dot(vc_ref[0, 0], p.astype(BF16))
    psum = p[:, 0:qlen] + p[:, qlen:2 * qlen] + p[:, 2 * qlen:3 * qlen] + p[:, 3 * qlen:4 * qlen]
    nblk = ov_ref.shape[0]
    imp = _dot_hi(ov_ref[...], psum)
    jb = lax.broadcasted_iota(jnp.int32, (nblk, 1), 0)
    t = t0 + lax.broadcasted_iota(jnp.int32, (1, qlen), 1)
    cur = t >> 6
    forced = (jb == 0) | (jb == cur) | (jb == cur - 1)
    score = jnp.where(jb * SEL_BLOCK <= t, imp + jnp.where(forced, SEL_FORCE, 0.0), NEG)
    bias = _topn_bias(score, min(SEL_TOPN, nblk)).astype(BF16)
    qaug = jnp.concatenate([qt, jnp.concatenate([bias] * NSA_GQ, axis=1)], axis=0)
    init = (jnp.full((1, cols), NEG, F32), jnp.zeros((1, cols), F32), jnp.zeros((NSA_HD, cols), F32))

    def sel_scores(k0):
        return _dot(sk_ref[0, 0, pl.ds(k0, SEL_TILE), :], qaug)

    def sel_pair(kp, carry):
        m, l, acc = carry
        k0 = pl.multiple_of(kp * (2 * SEL_TILE), 2 * SEL_TILE)
        k1 = pl.multiple_of(k0 + SEL_TILE, SEL_TILE)
        s0, s1 = sel_scores(k0), sel_scores(k1)
        m_new = jnp.maximum(m, jnp.maximum(jnp.max(s0, axis=0, keepdims=True), jnp.max(s1, axis=0, keepdims=True)))
        alpha = jnp.exp2(m - m_new)
        p0, p1 = jnp.exp2(s0 - m_new), jnp.exp2(s1 - m_new)
        l = alpha * l + jnp.sum(p0, axis=0, keepdims=True) + jnp.sum(p1, axis=0, keepdims=True)
        acc = (alpha * acc + _dot(sv_ref[0, 0, :, pl.ds(k0, SEL_TILE)], p0.astype(BF16))
               + _dot(sv_ref[0, 0, :, pl.ds(k1, SEL_TILE)], p1.astype(BF16)))
        return m_new, l, acc

    def sel_single(kt, carry):
        k0 = pl.multiple_of(kt * SEL_TILE, SEL_TILE)
        return _flash_step(carry, sel_scores(k0), sv_ref[0, 0, :, pl.ds(k0, SEL_TILE)])

    n_full = t0 // SEL_TILE
    carry = lax.fori_loop(0, n_full // 2, sel_pair, init)
    carry = lax.fori_loop(n_full - n_full % 2, n_full, sel_single, carry)
    k0 = pl.multiple_of(n_full * SEL_TILE, SEL_TILE)
    kpos = k0 + lax.broadcasted_iota(jnp.int32, (SEL_TILE, 1), 0)
    sc = jnp.where(kpos <= tcol, _dot(sk_ref[0, 0, pl.ds(k0, SEL_TILE), :], qaug), NEG)
    _, l_s, acc_s = _flash_step(carry, sc, sv_ref[0, 0, :, pl.ds(k0, SEL_TILE)])
    o_s = acc_s / l_s

    span = min(WINDOW + qlen, wk_ref.shape[2])
    kw0 = pl.multiple_of(jnp.maximum(t0 - WINDOW, 0), qlen)
    kpos = kw0 + lax.broadcasted_iota(jnp.int32, (span, 1), 0)
    s_w = _dot(wk_ref[0, 0, pl.ds(kw0, span), :], qt)
    s_w = jnp.where(kpos <= tcol, jnp.where(kpos >= tcol - WINDOW, s_w, NEG), NEG)
    p_w = jnp.exp2(s_w - jnp.max(s_w, axis=0, keepdims=True))
    o_w = _dot(wv_ref[0, 0, :, pl.ds(kw0, span)], p_w.astype(BF16)) / jnp.sum(p_w, axis=0, keepdims=True)

    gate = _sigmoid(g_ref[0, 0])
    outs = []
    for g in range(NSA_GQ):
        c = slice(g * qlen, (g + 1) * qlen)
        outs.append(gate[3 * g:3 * g + 1] * o_c[:, c] + gate[3 * g + 1:3 * g + 2] * o_s[:, c]
                    + gate[3 * g + 2:3 * g + 3] * o_w[:, c])
    o_ref[...] = jnp.concatenate(outs, axis=0).T.astype(o_ref.dtype)


def _overlap_matrix(ncmp, nblk):
    ci = np.arange(ncmp)[:, None] * CMP_STRIDE
    sj = np.arange(nblk)[None, :] * SEL_BLOCK
    return jnp.asarray(((ci <= sj + SEL_BLOCK - 1) & (ci + CMP_LEN - 1 >= sj)).astype(np.float32))


def _nsa_prompt(qt, kc, vct, skaug, svt, wkr, wvt, gates_t):
    b, _, length = qt.shape
    nb = length // QUERY_BLOCK
    ncmp = kc.shape[2]
    nblk = NSA_HD
    assert length // SEL_BLOCK <= nblk
    per_kv = lambda r, c: pl.BlockSpec((1, 1, r, c), lambda bi, k, i: (bi, k, 0, 0))
    return pl.pallas_call(
        _nsa_prompt_kernel,
        grid=(b, NSA_KV, nb),
        in_specs=[pl.BlockSpec((1, NSA_GQ * NSA_HD, QUERY_BLOCK), lambda bi, k, i: (bi, k, i)),
                  per_kv(ncmp, NSA_HD), per_kv(NSA_HD, ncmp), per_kv(length, 2 * NSA_HD), per_kv(NSA_HD, length),
                  per_kv(length, NSA_HD), per_kv(NSA_HD, length),
                  pl.BlockSpec((1, 1, 3 * NSA_GQ, QUERY_BLOCK), lambda bi, k, i: (bi, k, 0, i)),
                  pl.BlockSpec((nblk, ncmp), lambda bi, k, i: (0, 0))],
        out_specs=pl.BlockSpec((QUERY_BLOCK, NSA_GQ * NSA_HD), lambda bi, k, i: (bi * nb + i, k)),
        out_shape=jax.ShapeDtypeStruct((b * length, NSA_HEADS * NSA_HD), BF16),
        compiler_params=_cparams("parallel", "parallel", "arbitrary"),
        name="nsa_prompt",
    )(qt, kc, vct, skaug, svt, wkr, wvt, gates_t, _overlap_matrix(ncmp, nblk).T)


def _mem_prompt_kernel(q_ref, mk_ref, mv_ref, g_ref, o_ref):
    for h in range(MEM_HEADS):
        sl = slice(h * MEM_HD, (h + 1) * MEM_HD)
        q = q_ref[:, sl]
        ms = jnp.mean(q * q, axis=-1, keepdims=True)
        qn = (q * lax.rsqrt(ms + EPS) * g_ref[...] * (MEM_HD ** -0.5)).astype(BF16)
        s = _dot_nt(qn, mk_ref[0, :, sl])
        m = jnp.max(s, axis=-1, keepdims=True)
        p = jnp.exp(s - m)
        p = p / jnp.sum(p, axis=-1, keepdims=True)
        o_ref[:, sl] = _dot(p.astype(BF16), mv_ref[0, :, sl]).astype(o_ref.dtype)


def _mem_prompt(q_mem, mk, mv, qnorm, b, length, tq=256):
    nl = length // tq
    width = MEM_HEADS * MEM_HD
    mem = pl.BlockSpec((1, mk.shape[1], width), lambda bi, i: (bi, 0, 0))
    return pl.pallas_call(
        _mem_prompt_kernel,
        grid=(b, nl),
        in_specs=[pl.BlockSpec((tq, width), lambda bi, i: (bi * nl + i, 0)), mem, mem,
                  pl.BlockSpec((1, MEM_HD), lambda bi, i: (0, 0))],
        out_specs=pl.BlockSpec((tq, width), lambda bi, i: (bi * nl + i, 0)),
        out_shape=jax.ShapeDtypeStruct((b * length, width), BF16),
        compiler_params=_cparams("parallel", "parallel"),
        name="mem_prompt",
    )(q_mem, mk, mv, qnorm.reshape(1, -1))


def _merge_kernel(x_ref, ys_ref, yn_ref, ym_ref, gm_ref, bm_ref, ws_ref, wn_ref, wm_ref, wo_ref, o_ref):
    gate = _sigmoid(gm_ref[...] + bm_ref[...])
    mixed = (gate[:, :D_MODEL] * _dot(ys_ref[...], ws_ref[...])
             + gate[:, D_MODEL:2 * D_MODEL] * _dot(yn_ref[...], wn_ref[...])
             + gate[:, 2 * D_MODEL:] * _dot(ym_ref[...], wm_ref[...]))
    o_ref[...] = x_ref[...] + _dot(mixed.astype(BF16), wo_ref[...])


def _merge(x, ys, yn, ym, gm, wb, tm=256):
    m = x.shape[0]
    tm = min(tm, m)
    row = lambda wd: pl.BlockSpec((tm, wd), lambda i: (i, 0))
    full = lambda shape: pl.BlockSpec(shape, lambda i: (0,) * len(shape))
    return pl.pallas_call(
        _merge_kernel,
        grid=(m // tm,),
        in_specs=[row(D_MODEL), row(D_INNER), row(D_MODEL), row(D_MODEL), row(3 * D_MODEL), full((1, 3 * D_MODEL)),
                  full((D_INNER, D_MODEL)), full((D_MODEL, D_MODEL)), full((D_MODEL, D_MODEL)),
                  full((D_MODEL, D_MODEL))],
        out_specs=row(D_MODEL),
        out_shape=jax.ShapeDtypeStruct((m, D_MODEL), F32),
        compiler_params=_cparams("parallel"),
        name="merge",
    )(x, ys, yn, ym, gm, wb['b_merge'].reshape(1, -1), wb['w_ssd_o'], wb['w_nsa_o'], wb['w_mem_o'], wb['w_out'])


FFN_COLS = 2816


def _ffn_kernel(*refs, seq):
    if seq:
        x_ref, g_ref, wu_ref, cw_ref, cb_ref, wd_ref, o_ref, ul_ref, carry_ref = refs
    else:
        x_ref, g_ref, wu_ref, cw_ref, cb_ref, wd_ref, p0_ref, p1_ref, o_ref, ul_ref = refs
    x = x_ref[...]
    tm = x.shape[0]
    ms = jnp.mean(x * x, axis=-1, keepdims=True)
    xn = (x * lax.rsqrt(ms + EPS) * g_ref[...]).astype(BF16)
    if seq:
        @pl.when(pl.program_id(1) == 0)
        def _():
            carry_ref[...] = jnp.zeros_like(carry_ref)

        row = lax.broadcasted_iota(jnp.int32, (8, 1), 0)

    def conv_cols(lo):
        sl = slice(lo, lo + FFN_COLS)
        u = _dot(xn, wu_ref[:, sl])
        if seq:
            prev = carry_ref[:, sl]
            u1 = _shift_rows(u, 1, prev, row)
            u2 = _shift_rows(u, 2, prev, row)
            carry_ref[:, sl] = u[tm - 8:, :]
        else:
            u2, u1 = p0_ref[:, sl], p1_ref[:, sl]
            ul_ref[:, sl] = u
        return cb_ref[:, sl] + cw_ref[0:1, sl] * u2 + cw_ref[1:2, sl] * u1 + cw_ref[2:3, sl] * u

    acc = x
    for j in range(D_FF // FFN_COLS):
        act = (_silu(conv_cols(j * FFN_COLS)) * conv_cols(D_FF + j * FFN_COLS)).astype(BF16)
        acc = acc + _dot(act, wd_ref[j * FFN_COLS:(j + 1) * FFN_COLS, :])
    o_ref[...] = acc
    if seq:
        @pl.when(pl.program_id(1) == pl.num_programs(1) - 1)
        def _():
            ul_ref[0] = carry_ref[...]


def _ffn_prompt(x1, w, wb, b, length, tm=256):
    nl = length // tm
    row = lambda wd: pl.BlockSpec((tm, wd), lambda bi, i: (bi * nl + i, 0))
    const = lambda shape: pl.BlockSpec(shape, lambda bi, i: (0,) * len(shape), pipeline_mode=pl.Buffered(1))
    return pl.pallas_call(
        functools.partial(_ffn_kernel, seq=True),
        grid=(b, nl),
        in_specs=[row(D_MODEL), const((1, D_MODEL)), const((D_MODEL, 2 * D_FF)), const((FFN_CONV, 2 * D_FF)),
                  const((1, 2 * D_FF)), const((D_FF, D_MODEL))],
        out_specs=[row(D_MODEL), pl.BlockSpec((1, 8, 2 * D_FF), lambda bi, i: (bi, 0, 0))],
        out_shape=[jax.ShapeDtypeStruct((b * length, D_MODEL), F32), jax.ShapeDtypeStruct((b, 8, 2 * D_FF), F32)],
        scratch_shapes=[pltpu.VMEM((8, 2 * D_FF), F32)],
        compiler_params=_cparams("parallel", "arbitrary"),
        name="ffn_prompt",
    )(x1, w['norm_ffn'].reshape(1, -1), wb['w_up'], wb['ffn_conv_w'], wb['ffn_conv_b'].reshape(1, -1), wb['w_down'])


def _ffn_sample(x1, prev, w, wb):
    m = x1.shape[0]
    full = lambda shape: pl.BlockSpec(shape, lambda i: (0,) * len(shape))
    return pl.pallas_call(
        functools.partial(_ffn_kernel, seq=False),
        grid=(1,),
        in_specs=[full((m, D_MODEL)), full((1, D_MODEL)), full((D_MODEL, 2 * D_FF)), full((FFN_CONV, 2 * D_FF)),
                  full((1, 2 * D_FF)), full((D_FF, D_MODEL)), full((m, 2 * D_FF)), full((m, 2 * D_FF))],
        out_specs=[full((m, D_MODEL)), full((m, 2 * D_FF))],
        out_shape=[jax.ShapeDtypeStruct((m, D_MODEL), F32), jax.ShapeDtypeStruct((m, 2 * D_FF), F32)],
        compiler_params=_cparams("arbitrary"),
        name="ffn_sample",
    )(x1, w['norm_ffn'].reshape(1, -1), wb['w_up'], wb['ffn_conv_w'], wb['ffn_conv_b'].reshape(1, -1), wb['w_down'],
      prev[:, 0], prev[:, 1])


def _compress_sample(pool, page_table, pe, w1, w2, knorm, is_k):
    bs, npages = page_table.shape
    page = pool.shape[1]
    pool_t = _pool_t(pool)
    specs = [pl.BlockSpec((None, NSA_KV * NSA_HD, page), functools.partial(lambda b, pt, j: (pt[b, j], 0, 0), j=j))
             for j in range(npages)]
    return _compress_call([pool_t] * npages, specs, (bs,), npages * page // CMP_STRIDE, bs, pe, w1, w2, knorm,
                          is_k, n_prefetch=1, prefetch=(page_table,))


def _pool_t(pool):
    n, page = pool.shape[:2]
    return pool.transpose(0, 2, 3, 1).reshape(n, NSA_KV * NSA_HD, page)


def _rows8(x):
    return jnp.broadcast_to(x, (8, x.shape[1]))


def _ssd_step_kernel(xbc_ref, prev_ref, z_ref, dt_ref, h_ref, cw_ref, cb_ref, dtb_ref, a_ref, dskip_ref, nw_ref,
                     e64_ref, e128_ref, y_ref, hn_ref):
    conv = cb_ref[...] + cw_ref[3:4, :] * xbc_ref[0]
    for k in range(SSD_CONV - 1):
        conv = conv + cw_ref[k:k + 1, :] * prev_ref[0, k:k + 1, :]
    xc = _silu(conv)
    xs = xc[:, :D_INNER]
    nb = SSD_GROUPS * SSD_STATE
    bm = xc[:, D_INNER:D_INNER + nb]
    cm = xc[:, D_INNER + nb:]
    v = dt_ref[0] + dtb_ref[...]
    dt = jnp.maximum(v, 0.0) + jnp.log(1.0 + jnp.exp(-jnp.abs(v)))
    dec = jnp.exp(dt * a_ref[...])
    xdt = xs * _dot_x3(_rows8(dt), e64_ref[...], 0)[0:1]
    dec128 = _dot_x3(_rows8(dec), e128_ref[...], 0)
    pieces = []
    for c in range(D_INNER // 128):
        g = c // (SSD_HPG // 2)
        xcol = jnp.broadcast_to(xdt[:, c * 128:(c + 1) * 128], (128, 128)).T
        bg = jnp.broadcast_to(bm[:, g * SSD_STATE:(g + 1) * SSD_STATE], (128, SSD_STATE))
        decv = jnp.concatenate(
            [jnp.concatenate([dec128[:, hh * 128:(hh + 1) * 128]] * (SSD_HEAD_DIM // 8), axis=0)
             for hh in (2 * c, 2 * c + 1)], axis=0)
        hnew = decv * h_ref[0, c * 128:(c + 1) * 128, :] + xcol * bg
        hn_ref[0, c * 128:(c + 1) * 128, :] = hnew
        cg = _rows8(cm[:, g * SSD_STATE:(g + 1) * SSD_STATE]).astype(BF16)
        pieces.append(_dot_nt(cg, hnew.astype(BF16))[0:1])
    y = jnp.concatenate(pieces, axis=1) + dskip_ref[...] * xs
    yz = y * _silu(z_ref[0])
    ms = jnp.mean(yz * yz, axis=-1, keepdims=True)
    y_ref[0] = (yz * lax.rsqrt(ms + EPS) * nw_ref[...]).astype(y_ref.dtype)


def _ssd_step(xbc, prev, z, dt_raw, h0, w):
    bs = xbc.shape[0]
    a = -jnp.exp(w['ssd_a_log'].astype(F32))
    row = lambda wd: pl.BlockSpec((1, 1, wd), lambda i: (i, 0, 0))
    full = lambda shape: pl.BlockSpec(shape, lambda i: (0,) * len(shape))
    st = pl.BlockSpec((1, D_INNER, SSD_STATE), lambda i: (i, 0, 0))
    y, hn = pl.pallas_call(
        _ssd_step_kernel,
        grid=(bs,),
        in_specs=[row(SSD_CONV_DIM), pl.BlockSpec((1, SSD_CONV - 1, SSD_CONV_DIM), lambda i: (i, 0, 0)),
                  row(D_INNER), row(SSD_HEADS), st,
                  full((SSD_CONV, SSD_CONV_DIM)), full((1, SSD_CONV_DIM)), full((1, SSD_HEADS)), full((1, SSD_HEADS)),
                  full((1, D_INNER)), full((1, D_INNER)), full((SSD_HEADS, D_INNER)),
                  full((SSD_HEADS, SSD_HEADS * 128))],
        out_specs=[row(D_INNER), st],
        out_shape=[jax.ShapeDtypeStruct((bs, 1, D_INNER), BF16),
                   jax.ShapeDtypeStruct((bs, D_INNER, SSD_STATE), F32)],
        compiler_params=_cparams("parallel"),
        name="ssd_step",
    )(xbc.reshape(bs, 1, -1), prev, z.reshape(bs, 1, -1), dt_raw.reshape(bs, 1, -1),
      h0.reshape(bs, D_INNER, SSD_STATE), w['ssd_conv_w'], w['ssd_conv_b'].reshape(1, -1),
      w['ssd_dt_bias'].reshape(1, -1), a.reshape(1, -1),
      jnp.repeat(w['ssd_d'].astype(F32), SSD_HEAD_DIM).reshape(1, -1), w['ssd_norm'].reshape(1, -1),
      _head_expand(), _head_expand(128))
    return y.reshape(bs, D_INNER), hn.reshape(bs, SSD_HEADS, SSD_HEAD_DIM, SSD_STATE)


def _softmax_with_extra(s, s_new, valid=None):
    if valid is not None:
        s = jnp.where(valid, s, NEG)
    m = jnp.maximum(jnp.max(s, axis=-1, keepdims=True), s_new)
    p = jnp.exp(s - m)
    if valid is not None:
        p = jnp.where(valid, p, 0.0)
    p_new = jnp.exp(s_new - m)
    inv = 1.0 / (jnp.sum(p, axis=-1, keepdims=True) + p_new)
    return p * inv, p_new * inv


def _nsa_sample_kernel(*refs, npages, past_len):
    pt_ref = refs[0]
    del pt_ref
    q_ref, kv_ref, gate_ref = refs[1:4]
    skp = refs[4:4 + npages]
    svp = refs[4 + npages:4 + 2 * npages]
    (kc_ref, vc_ref, wk_ref, wv_ref, qg_ref, kg_ref, cos_ref, sin_ref, segm_ref, ov_ref, eblk_ref, fold_ref,
     foldt_ref, o_ref, skn_ref, wkn_ref, wko_ref, wvo_ref) = refs[4 + 2 * npages:]
    kvw = NSA_KV * NSA_HD
    cos, sin, segm = cos_ref[...], sin_ref[...], segm_ref[...]
    q = _rope(_head_rmsnorm(_rows8(q_ref[0]), qg_ref[...], segm), _rows8(cos), _rows8(sin)) * (NSA_HD ** -0.5)
    kv = kv_ref[0]
    sk_new = _rope(_head_rmsnorm(_rows8(kv[:, 2 * kvw:3 * kvw]), kg_ref[1:2, :], segm), _rows8(cos), _rows8(sin))[0:1]
    wk_new = _rope(_head_rmsnorm(_rows8(kv[:, 4 * kvw:5 * kvw]), kg_ref[2:3, :], segm), _rows8(cos), _rows8(sin))[0:1]
    sv_new = kv[:, 3 * kvw:4 * kvw]
    wv_new = kv[:, 5 * kvw:6 * kvw]
    skn_ref[0] = sk_new
    wkn_ref[0] = wk_new
    hrow = lax.broadcasted_iota(jnp.int32, (NSA_HEADS, NSA_HEADS * NSA_HD), 0)
    hcol = lax.broadcasted_iota(jnp.int32, (NSA_HEADS, NSA_HEADS * NSA_HD), 1)
    own = (hcol >> 6) == hrow
    q16 = jnp.where(own, jnp.concatenate([q, q], axis=0), 0.0).astype(BF16)
    qbd = _dot(q16, fold_ref[...])
    qbd_b = qbd.astype(BF16)
    ncmp = kc_ref.shape[1]
    kend = lax.broadcasted_iota(jnp.int32, (1, ncmp), 1) * CMP_STRIDE + (CMP_LEN - 1)
    valid_c = kend <= past_len
    s_c = jnp.where(valid_c, _dot_nt(qbd_b, kc_ref[0].astype(BF16)), NEG)
    m_c = jnp.max(s_c, axis=-1, keepdims=True)
    p_c = jnp.where(valid_c, jnp.exp(s_c - m_c), 0.0)
    den = jnp.sum(p_c, axis=-1, keepdims=True)
    p_c = p_c / jnp.where(den > 0, den, 1.0)
    o_c = _dot(p_c.astype(BF16), vc_ref[0].astype(BF16))
    gi = lax.broadcasted_iota(jnp.int32, (NSA_HEADS, NSA_HEADS), 0) // NSA_GQ
    gj = lax.broadcasted_iota(jnp.int32, (NSA_HEADS, NSA_HEADS), 1) // NSA_GQ
    psum = _dot_hi(jnp.where(gi == gj, 1.0, 0.0), p_c)
    psum = jnp.concatenate([psum, jnp.zeros((128 - NSA_HEADS, ncmp), F32)], axis=0)
    imp = lax.dot_general(ov_ref[...], psum, (((1,), (1,)), ((), ())), preferred_element_type=F32,
                          precision=HI)
    nslot = ov_ref.shape[0]
    jb = lax.broadcasted_iota(jnp.int32, (nslot, 1), 0)
    cur = past_len // SEL_BLOCK
    forced = (jb == 0) | (jb == cur) | (jb == cur - 1)
    score = jnp.where(jb * SEL_BLOCK <= past_len, imp + jnp.where(forced, SEL_FORCE, 0.0), NEG)
    bias = _topn_bias(score, SEL_TOPN).T[:NSA_HEADS]
    kmask = _dot(bias.astype(BF16), eblk_ref[...])
    sk_all = jnp.concatenate([r[...] for r in skp], axis=1).astype(BF16)
    sv_all = jnp.concatenate([r[...] for r in svp], axis=1).astype(BF16)
    s_s = _dot(qbd_b, sk_all) + kmask
    s_new = jnp.sum(qbd * sk_new, axis=-1, keepdims=True)
    p_s, p_new = _softmax_with_extra(s_s, s_new)
    o_s = _dot_nt(p_s.astype(BF16), sv_all) + p_new * sv_new
    wk = wk_ref[0]
    wv = wv_ref[0]
    s_w = _dot(qbd_b, wk.astype(BF16))
    s_wn = jnp.sum(qbd * wk_new, axis=-1, keepdims=True)
    p_w, p_wn = _softmax_with_extra(s_w, s_wn)
    o_w = _dot_nt(p_w.astype(BF16), wv.astype(BF16)) + p_wn * wv_new
    g16 = jnp.where((lax.broadcasted_iota(jnp.int32, (NSA_HEADS, 3 * NSA_HEADS), 1) // 3)
                    == lax.broadcasted_iota(jnp.int32, (NSA_HEADS, 3 * NSA_HEADS), 0),
                    jnp.broadcast_to(_sigmoid(gate_ref[0]), (NSA_HEADS, 3 * NSA_HEADS)), 0.0)
    br = lax.broadcasted_iota(jnp.int32, (NSA_HEADS, 3 * NSA_HEADS), 1) % 3
    gsel = lambda r: jnp.sum(jnp.where(br == r, g16, 0.0), axis=-1, keepdims=True)
    o16 = gsel(0) * o_c + gsel(1) * o_s + gsel(2) * o_w
    ox = _dot(o16.astype(BF16), foldt_ref[...])
    o_ref[0] = jnp.sum(jnp.where(own, ox, 0.0), axis=0, keepdims=True).astype(o_ref.dtype)
    nwin = wk.shape[1]
    lane = lax.broadcasted_iota(jnp.int32, (kvw, nwin), 1)

    def column(rowvec):
        col = jnp.broadcast_to(rowvec, (128, kvw)).T
        return jnp.concatenate([col] * (nwin // 128), axis=1)

    wko_ref[0] = jnp.where(lane == nwin - 1, column(wk_new), pltpu.roll(wk, nwin - 1, axis=1))
    wvo_ref[0] = jnp.where(lane == nwin - 1, column(wv_new), pltpu.roll(wv, nwin - 1, axis=1))


def _nsa_sample(q, kv, gates, pool_sk, pool_sv, page_table, kc, vc, win_k, win_v, w):
    bs, npages = page_table.shape
    page = pool_sk.shape[1]
    past_len = npages * page
    kvw = NSA_KV * NSA_HD
    nwin = win_k.shape[1]
    assert nwin == WINDOW and past_len % SEL_BLOCK == 0
    ncmp = kc.shape[1]
    nslot = 64
    assert past_len // SEL_BLOCK + 1 <= nslot
    cos128, sin128 = _rope_tables(jnp.full((1,), past_len, jnp.int32))
    overlap = _overlap_matrix(ncmp, nslot).T
    key_blk = np.arange(past_len) // SEL_BLOCK
    eblk = jnp.asarray((np.arange(nslot)[:, None] == key_blk[None, :]).astype(np.float32) * MASK_BIG, BF16)
    src = np.arange(NSA_HEADS * NSA_HD)
    dst = (src // NSA_HD // NSA_GQ) * NSA_HD + src % NSA_HD
    fold_np = np.zeros((NSA_HEADS * NSA_HD, kvw), np.float32)
    fold_np[src, dst] = 1.0
    fold = jnp.asarray(fold_np, BF16)
    foldt = jnp.asarray(fold_np.T, BF16)
    r3 = lambda t: t.reshape(bs, 1, -1)
    row = lambda wd: pl.BlockSpec((1, 1, wd), lambda b, pt: (b, 0, 0))
    full = lambda shape: pl.BlockSpec(shape, lambda b, pt: (0,) * len(shape))
    per_b = lambda r, c: pl.BlockSpec((1, r, c), lambda b, pt: (b, 0, 0))
    pages = [pl.BlockSpec((None, kvw, page), functools.partial(lambda b, pt, j: (pt[b, j], 0, 0), j=j))
             for j in range(npages)]
    pk3 = _pool_t(pool_sk)
    pv3 = _pool_t(pool_sv)
    gs = pltpu.PrefetchScalarGridSpec(
        num_scalar_prefetch=1, grid=(bs,),
        in_specs=[row(NSA_HEADS * NSA_HD), row(6 * kvw), row(3 * NSA_HEADS)] + pages + pages
        + [per_b(ncmp, kvw), per_b(ncmp, kvw), per_b(kvw, nwin), per_b(kvw, nwin),
           full((1, NSA_HEADS * NSA_HD)), full((3, kvw)), full((1, 128)), full((1, 128)), full((128, 128)),
           full((nslot, ncmp)), full((nslot, past_len)), full(fold.shape), full(foldt.shape)],
        out_specs=[row(NSA_HEADS * NSA_HD), row(kvw), row(kvw), per_b(kvw, nwin), per_b(kvw, nwin)])
    return pl.pallas_call(
        functools.partial(_nsa_sample_kernel, npages=npages, past_len=past_len),
        grid_spec=gs,
        out_shape=[jax.ShapeDtypeStruct((bs, 1, NSA_HEADS * NSA_HD), BF16),
                   jax.ShapeDtypeStruct((bs, 1, kvw), F32), jax.ShapeDtypeStruct((bs, 1, kvw), F32),
                   jax.ShapeDtypeStruct((bs, kvw, nwin), F32), jax.ShapeDtypeStruct((bs, kvw, nwin), F32)],
        compiler_params=_cparams("parallel"),
        name="nsa_sample",
    )(page_table, r3(q), r3(kv), r3(gates), *([pk3] * npages), *([pv3] * npages), kc, vc,
      _pool_t(win_k), _pool_t(win_v),
      jnp.tile(w['nsa_q_norm'], NSA_HEADS).reshape(1, -1), jnp.tile(w['nsa_k_norm'], (1, NSA_KV)),
      cos128, sin128, _segment_ones(), overlap, eblk, fold, foldt)


def _mem_sample_kernel(q_ref, mk_ref, mv_ref, g_ref, o_ref):
    q = q_ref[0]
    mtok = mk_ref.shape[1]
    rows = []
    for h in range(MEM_HEADS):
        qh = q[:, h * MEM_HD:(h + 1) * MEM_HD]
        ms = jnp.mean(qh * qh, axis=-1, keepdims=True)
        rows.append(qh * lax.rsqrt(ms + EPS) * g_ref[...] * (MEM_HD ** -0.5))
    q8 = jnp.concatenate(rows + rows, axis=0).astype(BF16)
    mk = mk_ref[0].reshape(mtok * MEM_HEADS, MEM_HD).astype(BF16)
    mv = mv_ref[0].reshape(mtok * MEM_HEADS, MEM_HD).astype(BF16)
    own = ((lax.broadcasted_iota(jnp.int32, (8, mtok * MEM_HEADS), 1) & (MEM_HEADS - 1))
           == (lax.broadcasted_iota(jnp.int32, (8, mtok * MEM_HEADS), 0) & (MEM_HEADS - 1)))
    s = jnp.where(own, _dot_nt(q8, mk), NEG)
    m = jnp.max(s, axis=-1, keepdims=True)
    p = jnp.exp(s - m)
    p = p / jnp.sum(p, axis=-1, keepdims=True)
    o8 = _dot(p.astype(BF16), mv)
    for h in range(MEM_HEADS):
        o_ref[0, :, h * MEM_HD:(h + 1) * MEM_HD] = o8[h:h + 1].astype(o_ref.dtype)


def _mem_sample(q_mem, mem_k, mem_v, qnorm):
    bs = q_mem.shape[0]
    width = MEM_HEADS * MEM_HD
    mtok = mem_k.shape[1]
    row = pl.BlockSpec((1, 1, width), lambda i: (i, 0, 0))
    mem = pl.BlockSpec((1, mtok, MEM_HEADS, MEM_HD), lambda i: (i, 0, 0, 0))
    return pl.pallas_call(
        _mem_sample_kernel,
        grid=(bs,),
        in_specs=[row, mem, mem, pl.BlockSpec((1, MEM_HD), lambda i: (0, 0))],
        out_specs=row,
        out_shape=jax.ShapeDtypeStruct((bs, 1, width), BF16),
        compiler_params=_cparams("parallel"),
        name="mem_sample",
    )(q_mem.reshape(bs, 1, width), mem_k, mem_v, qnorm.reshape(1, -1)).reshape(bs, width)


def _split_w_in(w_in):
    offs = np.cumsum((0,) + IN_SIZES)
    piece = lambda i: w_in[:, int(offs[i]):int(offs[i + 1])]
    small = jnp.concatenate([piece(2), piece(5)], axis=1)
    small = jnp.pad(small, ((0, 0), (0, 128 - small.shape[1])))
    return dict(z=piece(0), xbc=piece(1), small=small, q=piece(3), kv=piece(4), qmem=piece(6), gmerge=piece(7))


def _multi_mm_kernel(*refs, transposed):
    n_out = len(transposed)
    x_ref, g_ref = refs[:2]
    w_refs = refs[2:2 + n_out]
    o_refs = refs[2 + n_out:]
    x = x_ref[...]
    ms = jnp.mean(x * x, axis=-1, keepdims=True)
    xn = (x * lax.rsqrt(ms + EPS) * g_ref[...]).astype(BF16)
    for w_ref, o_ref, tr in zip(w_refs, o_refs, transposed):
        if tr:
            o_ref[0] = _dot_nt(w_ref[...], xn).astype(o_ref.dtype)
        else:
            o_ref[...] = _dot(xn, w_ref[...]).astype(o_ref.dtype)


def _multi_matmul(x, g, weights, transposed, seq_len, tm=256):
    m, k = x.shape
    tm = min(tm, m)
    assert m % tm == 0 and (not any(transposed) or seq_len % tm == 0)
    nl = seq_len // tm if any(transposed) else 1
    const = lambda shape: pl.BlockSpec(shape, lambda i: (0, 0), pipeline_mode=pl.Buffered(1))
    out_specs, out_shape = [], []
    for wm, tr in zip(weights, transposed):
        if tr:
            out_specs.append(pl.BlockSpec((1, wm.shape[0], tm), lambda i: (i // nl, 0, i % nl)))
            out_shape.append(jax.ShapeDtypeStruct((m // seq_len, wm.shape[0], seq_len), F32))
        else:
            out_specs.append(pl.BlockSpec((tm, wm.shape[1]), lambda i: (i, 0)))
            out_shape.append(jax.ShapeDtypeStruct((m, wm.shape[1]), F32))
    return pl.pallas_call(
        functools.partial(_multi_mm_kernel, transposed=tuple(transposed)),
        grid=(m // tm,),
        in_specs=[pl.BlockSpec((tm, k), lambda i: (i, 0)), const((1, k))] + [const(wm.shape) for wm in weights],
        out_specs=out_specs,
        out_shape=out_shape,
        compiler_params=_cparams("parallel"),
        name="in_proj",
    )(x, g.reshape(1, k).astype(F32), *weights)


def _in_proj(x, norm_w, wi, feature_major=(), seq_len=1):
    names = list(wi)
    weights = [wi[n].T if n in feature_major else wi[n] for n in names]
    proj = dict(zip(names, _multi_matmul(x, norm_w, weights, [n in feature_major for n in names], seq_len)))
    proj['dt'] = proj['small'][:, :SSD_HEADS]
    proj['gnsa'] = proj['small'][:, SSD_HEADS:SSD_HEADS + 3 * NSA_HEADS]
    return proj


def _prompt_layer(x_prompt, mem_prompt, w, wb, wi):
    b, length, _ = x_prompt.shape
    n = b * length
    kvw = NSA_KV * NSA_HD
    x = x_prompt.reshape(n, D_MODEL)
    proj = _in_proj(x, w['norm_mix'], wi, feature_major=('q', 'kv'), seq_len=length)
    xbc = proj['xbc'].reshape(b, length, SSD_CONV_DIM)
    y_ssd, ssd_state = _ssd_prompt(xbc, proj['z'].reshape(b, length, D_INNER),
                                   proj['dt'].reshape(b, length, SSD_HEADS), w)
    kvt = proj['kv']
    qt, skaug, skf, svt, wkr, wkf, wvt = _nsa_prep(proj['q'], kvt, w)
    kc = _compress_prompt(kvt, 0, w['cmp_k_pe'], w['cmp_k_w1'], w['cmp_k_w2'], w['nsa_k_norm'][0], True)
    vc = _compress_prompt(kvt, 1, w['cmp_v_pe'], w['cmp_v_w1'], w['cmp_v_w2'], w['nsa_k_norm'][0], False)
    heads = lambda t: t.reshape(b, -1, NSA_KV, NSA_HD).astype(BF16)
    t4 = lambda t: t.reshape(b, NSA_KV, NSA_HD, -1)
    gates_t = proj['gnsa'].reshape(b, length, NSA_KV, 3 * NSA_GQ).transpose(0, 2, 3, 1)
    o_nsa = _nsa_prompt(qt, heads(kc).transpose(0, 2, 1, 3), heads(vc).transpose(0, 2, 3, 1), skaug, t4(svt), wkr,
                        t4(wvt), gates_t)
    mem = mem_prompt.reshape(-1, D_MODEL)
    mkv = _matmul(mem, wb['w_mem_kv'], w['mem_norm'], tm=256)
    mtok = mem_prompt.shape[1]
    mk = mkv[:, :MEM_HEADS * MEM_HD].reshape(b, mtok, MEM_HEADS, MEM_HD)
    mk = mk * lax.rsqrt(jnp.mean(mk * mk, axis=-1, keepdims=True) + EPS) * w['mem_k_norm']
    mv = mkv[:, MEM_HEADS * MEM_HD:].reshape(b, mtok, MEM_HEADS, MEM_HD)
    o_mem = _mem_prompt(proj['qmem'], mk.reshape(b, mtok, -1).astype(BF16), mv.reshape(b, mtok, -1).astype(BF16),
                        w['mem_q_norm'], b, length)
    x1 = _merge(x, y_ssd.reshape(n, D_INNER), o_nsa, o_mem, proj['gmerge'], wb)
    y, u_last = _ffn_prompt(x1, w, wb, b, length)
    from_t = lambda t: t4(t).transpose(0, 3, 1, 2)
    keep = min(WINDOW, length)
    state = (from_t(kvt[:, 0:kvw]), from_t(kvt[:, kvw:2 * kvw]), from_t(skf), from_t(kvt[:, 3 * kvw:4 * kvw]),
             from_t(wkf[:, :, -keep:]), from_t(kvt[:, 5 * kvw:6 * kvw, -keep:]),
             xbc[:, -(SSD_CONV - 1):], ssd_state, u_last[:, -(FFN_CONV - 1):], mk, mv)
    return y.reshape(b, length, D_MODEL), state


def _sample_layer(x_sample, caches, page_table, w, wb, wi):
    (pool_ck, pool_cv, pool_sk, pool_sv, win_k, win_v, conv_prev, ssd_h0, mem_k, mem_v, ffn_prev) = caches
    bs = x_sample.shape[0]
    kvw = NSA_KV * NSA_HD
    x = x_sample.reshape(bs, D_MODEL)
    proj = _in_proj(x, w['norm_mix'], wi)
    y_ssd, ssd_state = _ssd_step(proj['xbc'], conv_prev, proj['z'], proj['dt'], ssd_h0, w)
    kv = proj['kv']
    kc = _compress_sample(pool_ck, page_table, w['cmp_k_pe'], w['cmp_k_w1'], w['cmp_k_w2'], w['nsa_k_norm'][0], True)
    vc = _compress_sample(pool_cv, page_table, w['cmp_v_pe'], w['cmp_v_w1'], w['cmp_v_w2'], w['nsa_k_norm'][0], False)
    o_nsa, sk_new, wk_new, win_k_new, win_v_new = _nsa_sample(
        proj['q'], kv, proj['gnsa'], pool_sk, pool_sv, page_table, kc, vc, win_k, win_v, w)
    o_mem = _mem_sample(proj['qmem'], mem_k, mem_v, w['mem_q_norm'])
    x1 = _merge(x, y_ssd, o_nsa.reshape(bs, -1), o_mem, proj['gmerge'], wb)
    y, u = _ffn_sample(x1, ffn_prev, w, wb)
    r4 = lambda t: t.reshape(bs, -1, NSA_KV, NSA_HD)
    from_t = lambda t: t.reshape(bs, NSA_KV, NSA_HD, -1).transpose(0, 3, 1, 2)
    state = (r4(kv[:, 0:kvw]), r4(kv[:, kvw:2 * kvw]), r4(sk_new), r4(kv[:, 3 * kvw:4 * kvw]),
             from_t(win_k_new), from_t(win_v_new),
             jnp.concatenate([conv_prev[:, 1:], proj['xbc'][:, None]], axis=1), ssd_state,
             jnp.concatenate([ffn_prev[:, 1:], u[:, None]], axis=1))
    return y.reshape(bs, 1, D_MODEL), state


def kernel(x_prompt, x_sample, cache_nsa_cmp_k, cache_nsa_cmp_v, cache_nsa_sel_k, cache_nsa_sel_v, state_nsa_win_k, state_nsa_win_v, state_ssd_conv, state_ssd, cache_mem_k, cache_mem_v, state_ffn_conv, page_table, mem_prompt, norm_mix, w_in, b_merge, ssd_conv_w, ssd_conv_b, ssd_dt_bias, ssd_a_log, ssd_d, ssd_norm, w_ssd_o, nsa_q_norm, nsa_k_norm, cmp_k_pe, cmp_k_w1, cmp_k_w2, cmp_v_pe, cmp_v_w1, cmp_v_w2, w_nsa_o, mem_norm, w_mem_kv, mem_q_norm, mem_k_norm, w_mem_o, w_out, norm_ffn, w_up, ffn_conv_w, ffn_conv_b, w_down):
    weights = dict(norm_mix=norm_mix, w_in=w_in, b_merge=b_merge, ssd_conv_w=ssd_conv_w,
                   ssd_conv_b=ssd_conv_b, ssd_dt_bias=ssd_dt_bias, ssd_a_log=ssd_a_log, ssd_d=ssd_d,
                   ssd_norm=ssd_norm, w_ssd_o=w_ssd_o, nsa_q_norm=nsa_q_norm, nsa_k_norm=nsa_k_norm,
                   cmp_k_pe=cmp_k_pe, cmp_k_w1=cmp_k_w1, cmp_k_w2=cmp_k_w2, cmp_v_pe=cmp_v_pe,
                   cmp_v_w1=cmp_v_w1, cmp_v_w2=cmp_v_w2, w_nsa_o=w_nsa_o, mem_norm=mem_norm,
                   w_mem_kv=w_mem_kv, mem_q_norm=mem_q_norm, mem_k_norm=mem_k_norm, w_mem_o=w_mem_o,
                   w_out=w_out, norm_ffn=norm_ffn, w_up=w_up, ffn_conv_w=ffn_conv_w,
                   ffn_conv_b=ffn_conv_b, w_down=w_down)
    w = {name: arr[0] for name, arr in weights.items()}
    wb = dict(w)
    for name in ('w_ssd_o', 'w_nsa_o', 'w_mem_o', 'w_out', 'w_up', 'w_down', 'w_mem_kv'):
        wb[name] = w[name].astype(BF16)
    wi = {name: piece.astype(BF16) for name, piece in _split_w_in(w['w_in']).items()}
    y_p, st_p = _prompt_layer(x_prompt, mem_prompt, w, wb, wi)
    caches = (cache_nsa_cmp_k[0], cache_nsa_cmp_v[0], cache_nsa_sel_k[0], cache_nsa_sel_v[0], state_nsa_win_k[0],
              state_nsa_win_v[0], state_ssd_conv[0], state_ssd[0], cache_mem_k[0], cache_mem_v[0], state_ffn_conv[0])
    y_s, st_s = _sample_layer(x_sample, caches, page_table, w, wb, wi)
    return (y_p, y_s) + tuple(s[None] for s in st_p) + tuple(s[None] for s in st_s)
```

```python
import functools
import math

import numpy as np
import jax
import jax.numpy as jnp
from jax import lax
from jax.experimental import pallas as pl
from jax.experimental.pallas import tpu as pltpu

D_MODEL = 1024
D_INNER = 2048
SSD_HEAD_DIM = 64
SSD_HEADS = 32
SSD_GROUPS = 4
SSD_HPG = 8
SSD_STATE = 128
SSD_CONV = 4
SSD_CONV_DIM = 3072
SSD_CHUNK = 128
NSA_HEADS = 16
NSA_KV = 4
NSA_GQ = 4
NSA_HD = 64
CMP_LEN = 32
CMP_STRIDE = 16
CMP_HIDDEN = 128
SEL_BLOCK = 64
SEL_TOPN = 16
SEL_FORCE = 1.0e4
WINDOW = 512
QUERY_BLOCK = 256
MEM_HEADS = 4
MEM_HD = 256
D_FF = 2816
FFN_CONV = 3
ROPE_THETA = 10000.0
EPS = 1e-6
IN_SIZES = (2048, 3072, 32, 1024, 1536, 48, 1024, 3072)

VMEM_LIMIT = 48 * 1024 * 1024
MASK_BIG = 2.0 ** 20
NEG = -1.0e30

F32 = jnp.float32
BF16 = jnp.bfloat16
HI = lax.Precision.HIGHEST


def _cparams(*sem):
    return pltpu.CompilerParams(dimension_semantics=sem, vmem_limit_bytes=VMEM_LIMIT)


def _sigmoid(x):
    return 1.0 / (1.0 + jnp.exp(-x))


def _silu(x):
    return x * _sigmoid(x)


def _dot(a, b):
    return jnp.dot(a, b, preferred_element_type=F32)


def _dot_nt(a, b):
    return lax.dot_general(a, b, (((1,), (1,)), ((), ())), preferred_element_type=F32)


def _dot_hi(a, b):
    return jnp.dot(a, b, preferred_element_type=F32, precision=HI)


def _split3(x):
    hi = x.astype(BF16)
    r1 = x - hi.astype(F32)
    mid = r1.astype(BF16)
    lo = (r1 - mid.astype(F32)).astype(BF16)
    return hi, mid, lo


def _dot_x3(a, b, split):
    if split == 0:
        parts = [_dot(t, b) for t in _split3(a)]
    else:
        parts = [_dot(a, t) for t in _split3(b)]
    return parts[0] + parts[1] + parts[2]


def _mm_kernel(x_ref, g_ref, w_ref, o_ref, xn_ref, *, norm):
    @pl.when(pl.program_id(1) == 0)
    def _():
        x = x_ref[...].astype(F32)
        if norm:
            ms = jnp.mean(x * x, axis=-1, keepdims=True)
            x = x * lax.rsqrt(ms + EPS) * g_ref[...]
        xn_ref[...] = x.astype(BF16)

    o_ref[...] = _dot(xn_ref[...], w_ref[...]).astype(o_ref.dtype)


def _matmul(x, w, g=None, out_dtype=F32, tm=512, tn=512):
    m, k = x.shape
    n = w.shape[1]
    tm = min(tm, m)
    tn = min(tn, n)
    assert m % tm == 0 and n % tn == 0, (m, n, tm, tn)
    norm = g is not None
    if g is None:
        g = jnp.ones((1, k), F32)
    return pl.pallas_call(
        functools.partial(_mm_kernel, norm=norm),
        grid=(m // tm, n // tn),
        in_specs=[pl.BlockSpec((tm, k), lambda i, j: (i, 0)),
                  pl.BlockSpec((1, k), lambda i, j: (0, 0)),
                  pl.BlockSpec((k, tn), lambda i, j: (0, j))],
        out_specs=pl.BlockSpec((tm, tn), lambda i, j: (i, j)),
        out_shape=jax.ShapeDtypeStruct((m, n), out_dtype),
        scratch_shapes=[pltpu.VMEM((tm, k), BF16)],
        compiler_params=_cparams("parallel", "arbitrary"),
        name="rms_matmul" if norm else "matmul",
    )(x, g.reshape(1, k).astype(F32), w)


def _shift_rows(x, k, prev, row):
    r = pltpu.roll(x, k, axis=0)
    head = r[:8]
    for i in range(k):
        head = jnp.where(row == i, prev[8 - k + i:8 - k + i + 1, :], head)
    return jnp.concatenate([head, r[8:]], axis=0)


def _ssd_prompt_kernel(xbc_ref, z_ref, dt_ref, dtT_ref, cw_ref, cb_ref, dtb_ref, dtbT_ref, a_ref, aT_ref,
                       dskip_ref, nw_ref, e_ref, y_ref, hT_ref, carry_ref, h_ref, yacc_ref):
    c = pl.program_id(1)
    q = SSD_CHUNK

    @pl.when(c == 0)
    def _():
        carry_ref[...] = jnp.zeros_like(carry_ref)
        h_ref[...] = jnp.zeros_like(h_ref)

    xbc = xbc_ref[0]
    row = lax.broadcasted_iota(jnp.int32, (8, 1), 0)
    prev = carry_ref[...]
    conv = cb_ref[...] + cw_ref[3:4, :] * xbc
    for k in range(1, SSD_CONV):
        conv = conv + cw_ref[3 - k:4 - k, :] * _shift_rows(xbc, k, prev, row)
    carry_ref[...] = xbc[q - 8:, :]
    xc = _silu(conv)
    xs = xc[:, :D_INNER]
    bm = xc[:, D_INNER:D_INNER + SSD_GROUPS * SSD_STATE].astype(BF16)
    cm = xc[:, D_INNER + SSD_GROUPS * SSD_STATE:].astype(BF16)

    def softplus(v):
        return jnp.maximum(v, 0.0) + jnp.log(1.0 + jnp.exp(-jnp.abs(v)))

    dt = softplus(dt_ref[0] + dtb_ref[...])
    dtT = softplus(dtT_ref[0] + dtbT_ref[...])
    ii = lax.broadcasted_iota(jnp.int32, (q, q), 0)
    jj = lax.broadcasted_iota(jnp.int32, (q, q), 1)
    causal = ii >= jj
    cum = _dot_x3(jnp.where(causal, 1.0, 0.0).astype(BF16), dt * a_ref[...], 1)
    cumT = _dot_x3(dtT * aT_ref[...], jnp.where(jj >= ii, 1.0, 0.0).astype(BF16), 0)
    ecum = jnp.exp(cum)
    dend = jnp.exp(cum[q - 1:q, :] - cum)
    stacked = jnp.concatenate([dt, ecum, dend], axis=0)
    hi = stacked.astype(BF16)
    lo = (stacked - hi.astype(F32)).astype(BF16)
    spread = _dot(hi, e_ref[...]) + _dot(lo, e_ref[...])
    dt_x, ecum_x, dend_x = spread[:q], spread[q:2 * q], spread[2 * q:]
    xdt = xs * dt_x
    xdt_b = xdt.astype(BF16)
    xw_b = (xdt * dend_x).astype(BF16)
    gw = SSD_HPG * SSD_HEAD_DIM
    for g in range(SSD_GROUPS):
        bg = bm[:, g * SSD_STATE:(g + 1) * SSD_STATE]
        cg = cm[:, g * SSD_STATE:(g + 1) * SSD_STATE]
        cb = _dot_nt(cg, bg)
        h_prev = h_ref[g]
        yoff = _dot(cg, h_prev.astype(BF16)) * ecum_x[:, g * gw:(g + 1) * gw]
        st = _dot(bg.astype(F32).T.astype(BF16), xw_b[:, g * gw:(g + 1) * gw])
        h_ref[g] = h_prev * ecum_x[q - 1:q, g * gw:(g + 1) * gw] + st
        for eh in range(SSD_HPG):
            hh = g * SSD_HPG + eh
            seg = cum[:, hh:hh + 1] - cumT[hh:hh + 1, :]
            decay = jnp.exp(jnp.where(causal, seg, NEG))
            mm = (cb * decay).astype(BF16)
            lo = hh * SSD_HEAD_DIM
            yd = _dot(mm, xdt_b[:, lo:lo + SSD_HEAD_DIM])
            yacc_ref[:, lo:lo + SSD_HEAD_DIM] = yd + yoff[:, eh * SSD_HEAD_DIM:(eh + 1) * SSD_HEAD_DIM]
    y = yacc_ref[...] + dskip_ref[...] * xs
    yz = y * _silu(z_ref[0])
    ms = jnp.mean(yz * yz, axis=-1, keepdims=True)
    y_ref[0] = (yz * lax.rsqrt(ms + EPS) * nw_ref[...]).astype(y_ref.dtype)

    @pl.when(c == pl.num_programs(1) - 1)
    def _():
        hT_ref[0] = h_ref[...]


def _head_expand(width=SSD_HEAD_DIM):
    e = np.zeros((SSD_HEADS, SSD_HEADS * width), np.float32)
    for h in range(SSD_HEADS):
        e[h, h * width:(h + 1) * width] = 1.0
    return jnp.asarray(e, BF16)


def _ssd_prompt(xbc, z, dt_raw, w):
    b, length, _ = xbc.shape
    q = SSD_CHUNK
    nc = length // q
    dtT = jnp.swapaxes(dt_raw, 1, 2)
    a = -jnp.exp(w['ssd_a_log'].astype(F32))
    full = lambda shape: pl.BlockSpec(shape, lambda i, j: (0,) * len(shape))
    y, hT = pl.pallas_call(
        _ssd_prompt_kernel,
        grid=(b, nc),
        in_specs=[pl.BlockSpec((1, q, SSD_CONV_DIM), lambda i, j: (i, j, 0)),
                  pl.BlockSpec((1, q, D_INNER), lambda i, j: (i, j, 0)),
                  pl.BlockSpec((1, q, SSD_HEADS), lambda i, j: (i, j, 0)),
                  pl.BlockSpec((1, SSD_HEADS, q), lambda i, j: (i, 0, j)),
                  full((SSD_CONV, SSD_CONV_DIM)), full((1, SSD_CONV_DIM)),
                  full((1, SSD_HEADS)), full((SSD_HEADS, 1)), full((1, SSD_HEADS)), full((SSD_HEADS, 1)),
                  full((1, D_INNER)), full((1, D_INNER)), full((SSD_HEADS, D_INNER))],
        out_specs=[pl.BlockSpec((1, q, D_INNER), lambda i, j: (i, j, 0)),
                   pl.BlockSpec((1, SSD_GROUPS, SSD_STATE, SSD_HPG * SSD_HEAD_DIM), lambda i, j: (i, 0, 0, 0))],
        out_shape=[jax.ShapeDtypeStruct((b, length, D_INNER), BF16),
                   jax.ShapeDtypeStruct((b, SSD_GROUPS, SSD_STATE, SSD_HPG * SSD_HEAD_DIM), F32)],
        scratch_shapes=[pltpu.VMEM((8, SSD_CONV_DIM), F32),
                        pltpu.VMEM((SSD_GROUPS, SSD_STATE, SSD_HPG * SSD_HEAD_DIM), F32),
                        pltpu.VMEM((q, D_INNER), F32)],
        compiler_params=_cparams("parallel", "arbitrary"),
        name="ssd_prompt",
    )(xbc, z, dt_raw, dtT, w['ssd_conv_w'], w['ssd_conv_b'].reshape(1, -1),
      w['ssd_dt_bias'].reshape(1, -1), w['ssd_dt_bias'].reshape(-1, 1), a.reshape(1, -1), a.reshape(-1, 1),
      jnp.repeat(w['ssd_d'].astype(F32), SSD_HEAD_DIM).reshape(1, -1), w['ssd_norm'].reshape(1, -1),
      _head_expand())
    state = hT.reshape(b, SSD_GROUPS, SSD_STATE, SSD_HPG, SSD_HEAD_DIM).transpose(0, 1, 3, 4, 2)
    return y, state.reshape(b, SSD_HEADS, SSD_HEAD_DIM, SSD_STATE)


def _segment_ones():
    i = np.arange(128)
    return jnp.asarray((i[:, None] // NSA_HD == i[None, :] // NSA_HD).astype(np.float32), BF16)


def _head_rmsnorm(x, g_row, segm):
    sq = x * x
    hi = sq.astype(BF16)
    lo = (sq - hi.astype(F32)).astype(BF16)
    parts = []
    for c in range(x.shape[1] // 128):
        sl = slice(c * 128, (c + 1) * 128)
        parts.append(_dot(hi[:, sl], segm) + _dot(lo[:, sl], segm))
    ss = parts[0] if len(parts) == 1 else jnp.concatenate(parts, axis=1)
    return x * lax.rsqrt(ss * (1.0 / NSA_HD) + EPS) * g_row


def _tile_lanes(t, width):
    reps = width // t.shape[1]
    return t if reps == 1 else jnp.concatenate([t] * reps, axis=1)


def _rope(x, cos128, sin128):
    width = x.shape[1]
    lane = lax.broadcasted_iota(jnp.int32, x.shape, 1)
    first = (lane & (NSA_HD // 2)) == 0
    rot = jnp.where(first, pltpu.roll(x, width - NSA_HD // 2, axis=1), pltpu.roll(x, NSA_HD // 2, axis=1))
    return x * _tile_lanes(cos128, width) + rot * _tile_lanes(sin128, width)


def _rope_tables(pos):
    half = NSA_HD // 2
    inv = ROPE_THETA ** (-jnp.arange(half, dtype=F32) / half)
    ang = pos.astype(F32)[:, None] * inv[None, :]
    cos, sin = jnp.cos(ang), jnp.sin(ang)
    cos128 = jnp.concatenate([cos, cos, cos, cos], axis=1)
    sin128 = jnp.concatenate([-sin, sin, -sin, sin], axis=1)
    return cos128, sin128


def _norm_rope_t(x, g_col, cos, sin):
    nh = x.shape[0] // NSA_HD
    x3 = x.reshape(nh, NSA_HD, x.shape[1])
    ms = jnp.sum(x3 * x3, axis=1, keepdims=True) * (1.0 / NSA_HD)
    x3 = x3 * lax.rsqrt(ms + EPS) * g_col[None]
    half = NSA_HD // 2
    x1, x2 = x3[:, :half], x3[:, half:]
    out = jnp.concatenate([x1 * cos[None] - x2 * sin[None], x2 * cos[None] + x1 * sin[None]], axis=1)
    return out.reshape(x.shape)


def _nsa_prep_kernel(q_ref, kv_ref, cos_ref, sin_ref, qg_ref, kg_ref,
                     qt_ref, skaug_ref, skf_ref, svt_ref, wkr_ref, wkf_ref, wvt_ref):
    tl = q_ref.shape[2]
    i = pl.program_id(1)
    cos, sin = cos_ref[...], sin_ref[...]
    kvw = NSA_KV * NSA_HD
    qt_ref[0] = (_norm_rope_t(q_ref[0], qg_ref[...], cos, sin) * (NSA_HD ** -0.5 * math.log2(math.e))).astype(BF16)
    sk = _norm_rope_t(kv_ref[0, 2 * kvw:3 * kvw, :], kg_ref[:, 1:2], cos, sin)
    wk = _norm_rope_t(kv_ref[0, 4 * kvw:5 * kvw, :], kg_ref[:, 2:3], cos, sin)
    skf_ref[0] = sk
    wkf_ref[0] = wk
    svt_ref[0] = kv_ref[0, 3 * kvw:4 * kvw, :].astype(BF16)
    wvt_ref[0] = kv_ref[0, 5 * kvw:6 * kvw, :].astype(BF16)
    pos = i * tl + lax.broadcasted_iota(jnp.int32, (tl, NSA_HD), 0)
    blk = lax.broadcasted_iota(jnp.int32, (tl, NSA_HD), 1)
    onehot = jnp.where((pos >> 6) == blk, MASK_BIG, 0.0).astype(BF16)
    for k in range(NSA_KV):
        sl = slice(k * NSA_HD, (k + 1) * NSA_HD)
        skaug_ref[0, k] = jnp.concatenate([sk[sl].T.astype(BF16), onehot], axis=1)
        wkr_ref[0, k] = wk[sl].T.astype(BF16)


def _nsa_prep(qt, kvt, w, tl=256):
    b, _, length = qt.shape
    nl = length // tl
    half = NSA_HD // 2
    inv = ROPE_THETA ** (-jnp.arange(half, dtype=F32) / half)
    ang = inv[:, None] * jnp.arange(length, dtype=F32)[None, :]
    kvw = NSA_KV * NSA_HD
    fm = lambda rows: pl.BlockSpec((1, rows, tl), lambda bi, i: (bi, 0, i))
    tab = pl.BlockSpec((half, tl), lambda bi, i: (0, i))
    full = lambda shape: pl.BlockSpec(shape, lambda bi, i: (0,) * len(shape))
    rows = lambda wd: pl.BlockSpec((1, NSA_KV, tl, wd), lambda bi, i: (bi, 0, i, 0))
    return pl.pallas_call(
        _nsa_prep_kernel,
        grid=(b, nl),
        in_specs=[fm(NSA_HEADS * NSA_HD), fm(6 * kvw), tab, tab, full((NSA_HD, 1)), full((NSA_HD, 3))],
        out_specs=[fm(NSA_HEADS * NSA_HD), rows(2 * NSA_HD), fm(kvw), fm(kvw), rows(NSA_HD), fm(kvw), fm(kvw)],
        out_shape=[jax.ShapeDtypeStruct((b, NSA_HEADS * NSA_HD, length), BF16),
                   jax.ShapeDtypeStruct((b, NSA_KV, length, 2 * NSA_HD), BF16),
                   jax.ShapeDtypeStruct((b, kvw, length), F32),
                   jax.ShapeDtypeStruct((b, kvw, length), BF16),
                   jax.ShapeDtypeStruct((b, NSA_KV, length, NSA_HD), BF16),
                   jax.ShapeDtypeStruct((b, kvw, length), F32),
                   jax.ShapeDtypeStruct((b, kvw, length), BF16)],
        compiler_params=_cparams("parallel", "parallel"),
        name="nsa_prep",
    )(qt, kvt, jnp.cos(ang), jnp.sin(ang), w['nsa_q_norm'].reshape(-1, 1), w['nsa_k_norm'].T)


def _compress_kernel(*refs, n_x, n_prefetch):
    refs = refs[n_prefetch:]
    (wk_ref, pek_ref, w2k_ref, wv_ref, pev_ref, w2v_ref, g_ref, cos_ref, sin_ref, segm_ref, ok_ref, ov_ref,
     x0k_ref, x1k_ref, x0v_ref, x1v_ref) = refs[2 * n_x:]
    shared = (g_ref, cos_ref, sin_ref, segm_ref)
    _compress_one(refs[:n_x], wk_ref, pek_ref, w2k_ref, shared, ok_ref, x0k_ref, x1k_ref, True)
    _compress_one(refs[n_x:2 * n_x], wv_ref, pev_ref, w2v_ref, shared, ov_ref, x0v_ref, x1v_ref, False)


def _compress_one(page_refs, wbd_ref, pe_ref, w2_ref, shared, o_ref, x0_ref, x1_ref, is_k):
    g_ref, cos_ref, sin_ref, segm_ref = shared
    plen = page_refs[0].shape[-1]
    for j, r in enumerate(page_refs):
        rows = r[...].T
        x0_ref[j * plen:(j + 1) * plen, :] = rows[:, :128]
        x1_ref[j * plen:(j + 1) * plen, :] = rows[:, 128:]
    nchunk = x0_ref.shape[0] // CMP_STRIDE
    row = lax.broadcasted_iota(jnp.int32, (nchunk, 1), 0)
    outs = []
    for x_ref in (x0_ref, x1_ref):
        x = jnp.concatenate([x_ref[pl.ds(s, nchunk, stride=CMP_STRIDE), :].astype(BF16)
                             for s in range(CMP_STRIDE)], axis=1)
        acc = _dot(x, wbd_ref[...])
        hid = []
        for kl in range(2):
            lo = kl * 2 * CMP_HIDDEN
            first = acc[:, lo:lo + CMP_HIDDEN]
            second = pltpu.roll(acc[:, lo + CMP_HIDDEN:lo + 2 * CMP_HIDDEN], nchunk - 1, axis=0)
            hid.append(_silu(first + second + pe_ref[...]))
        outs.append(_dot(jnp.concatenate(hid, axis=1).astype(BF16), w2_ref[...]))
    out = jnp.concatenate(outs, axis=1)
    if is_k:
        out = _rope(_head_rmsnorm(out, g_ref[...], segm_ref[...]), cos_ref[...], sin_ref[...])
    o_ref[0] = jnp.where(row < nchunk - 1, out, 0.0)


def _compress_weights(pe, w1, w2):
    w1r = w1.reshape(CMP_LEN, NSA_HD, CMP_HIDDEN)
    pe_term = jnp.einsum('ld,ldm->m', pe, w1r, precision=HI).reshape(1, CMP_HIDDEN)
    both = jnp.concatenate([w1r[:CMP_STRIDE], w1r[CMP_STRIDE:]], axis=-1)
    zero = jnp.zeros_like(both)
    wbd = jnp.concatenate([jnp.concatenate([both, zero], axis=-1), jnp.concatenate([zero, both], axis=-1)], axis=1)
    zero2 = jnp.zeros_like(w2)
    w2bd = jnp.concatenate([jnp.concatenate([w2, zero2], axis=-1), jnp.concatenate([zero2, w2], axis=-1)], axis=0)
    wbd = wbd.reshape(CMP_STRIDE * 2 * NSA_HD, 4 * CMP_HIDDEN)
    return wbd.astype(BF16), pe_term, w2bd.astype(BF16)


def _compress_call(x_args, x_specs, grid, nchunk, batch, w, n_prefetch=0, prefetch=()):
    wk = _compress_weights(w['cmp_k_pe'], w['cmp_k_w1'], w['cmp_k_w2'])
    wv = _compress_weights(w['cmp_v_pe'], w['cmp_v_w1'], w['cmp_v_w2'])
    cos128, sin128 = _rope_tables(jnp.arange(nchunk, dtype=jnp.int32) * CMP_STRIDE + (CMP_LEN - 1))
    kvw = NSA_KV * NSA_HD
    full = lambda shape: pl.BlockSpec(shape, lambda *a: (0,) * len(shape))
    wspecs = [full(wk[0].shape), full((1, CMP_HIDDEN)), full(wk[2].shape)]
    out = pl.BlockSpec((1, nchunk, kvw), lambda i, *a: (i, 0, 0))
    gs = pltpu.PrefetchScalarGridSpec(
        num_scalar_prefetch=n_prefetch, grid=grid,
        in_specs=list(x_specs) + wspecs + wspecs + [full((1, kvw)), full((nchunk, 128)), full((nchunk, 128)),
                                                    full((128, 128))],
        out_specs=[out, out],
        scratch_shapes=[pltpu.VMEM((nchunk * CMP_STRIDE, 128), F32)] * 4)
    return pl.pallas_call(
        functools.partial(_compress_kernel, n_x=len(x_specs) // 2, n_prefetch=n_prefetch),
        grid_spec=gs,
        out_shape=[jax.ShapeDtypeStruct((batch, nchunk, kvw), F32)] * 2,
        compiler_params=_cparams("parallel"),
        name="compress",
    )(*prefetch, *x_args, *wk, *wv, jnp.tile(w['nsa_k_norm'][0], NSA_KV).reshape(1, -1), cos128, sin128,
      _segment_ones())


def _compress_prompt(kvt, w, page=128):
    b, _, length = kvt.shape
    npages = length // page
    specs = [pl.BlockSpec((None, NSA_KV * NSA_HD, page), functools.partial(lambda i, c, j: (i, c, j), c=c, j=j))
             for c in range(2) for j in range(npages)]
    return _compress_call([kvt] * len(specs), specs, (b,), length // CMP_STRIDE, b, w)


SEL_TILE = 512


def _flash_step(carry, s, vt):
    m, l, acc = carry
    m_new = jnp.maximum(m, jnp.max(s, axis=0, keepdims=True))
    alpha = jnp.exp2(m - m_new)
    p = jnp.exp2(s - m_new)
    l = alpha * l + jnp.sum(p, axis=0, keepdims=True)
    acc = alpha * acc + _dot(vt, p.astype(BF16))
    return m_new, l, acc


def _topn_bias(score, n):
    nblk, nq = score.shape
    groups = [score[8 * v:8 * v + 8] for v in range(nblk // 8)]
    sub = lax.broadcasted_iota(jnp.int32, (8, nq), 0)
    cnt = [jnp.zeros((8, nq), F32) for _ in groups]
    for jp in range(nblk):
        row = score[jp:jp + 1]
        for v, grp in enumerate(groups):
            if v < jp // 8:
                inc = jnp.where(row > grp, 1.0, 0.0)
            elif v > jp // 8:
                inc = jnp.where(row >= grp, 1.0, 0.0)
            else:
                inc = jnp.where(sub > jp % 8, jnp.where(row >= grp, 1.0, 0.0), jnp.where(row > grp, 1.0, 0.0))
            cnt[v] = cnt[v] + inc
    return jnp.concatenate([jnp.where(c < n, 0.0, -1.0) for c in cnt], axis=0)


def _nsa_prompt_kernel(q_ref, kc_ref, vc_ref, sk_ref, sv_ref, wk_ref, wv_ref, g_ref, ov_ref, o_ref):
    qb = pl.program_id(2)
    qlen = QUERY_BLOCK
    cols = NSA_GQ * qlen
    t0 = qb * qlen
    qt = jnp.concatenate([q_ref[0, g * NSA_HD:(g + 1) * NSA_HD, :] for g in range(NSA_GQ)], axis=1)
    tcol = t0 + (lax.broadcasted_iota(jnp.int32, (1, cols), 1) & (qlen - 1))
    ncmp = kc_ref.shape[2]
    s = _dot(kc_ref[0, 0], qt)
    kend = lax.broadcasted_iota(jnp.int32, (ncmp, 1), 0) * CMP_STRIDE + (CMP_LEN - 1)
    valid = kend <= tcol
    s = jnp.where(valid, s, NEG)
    m = jnp.max(s, axis=0, keepdims=True)
    p = jnp.where(valid, jnp.exp2(s - m), 0.0)
    den = jnp.sum(p, axis=0, keepdims=True)
    p = p * (1.0 / jnp.where(den > 0, den, 1.0))
    o_c = _dot(vc_ref[0, 0], p.astype(BF16))
    psum = p[:, 0:qlen] + p[:, qlen:2 * qlen] + p[:, 2 * qlen:3 * qlen] + p[:, 3 * qlen:4 * qlen]
    nblk = ov_ref.shape[0]
    imp = _dot_hi(ov_ref[...], psum)
    jb = lax.broadcasted_iota(jnp.int32, (nblk, 1), 0)
    t = t0 + lax.broadcasted_iota(jnp.int32, (1, qlen), 1)
    cur = t >> 6
    forced = (jb == 0) | (jb == cur) | (jb == cur - 1)
    score = jnp.where(jb * SEL_BLOCK <= t, imp + jnp.where(forced, SEL_FORCE, 0.0), NEG)
    bias = _topn_bias(score, min(SEL_TOPN, nblk)).astype(BF16)
    qaug = jnp.concatenate([qt, jnp.concatenate([bias] * NSA_GQ, axis=1)], axis=0)
    init = (jnp.full((1, cols), NEG, F32), jnp.zeros((1, cols), F32), jnp.zeros((NSA_HD, cols), F32))

    def sel_scores(k0):
        return _dot(sk_ref[0, 0, pl.ds(k0, SEL_TILE), :], qaug)

    def sel_pair(kp, carry):
        m, l, acc = carry
        k0 = pl.multiple_of(kp * (2 * SEL_TILE), 2 * SEL_TILE)
        k1 = pl.multiple_of(k0 + SEL_TILE, SEL_TILE)
        s0, s1 = sel_scores(k0), sel_scores(k1)
        m_new = jnp.maximum(m, jnp.maximum(jnp.max(s0, axis=0, keepdims=True), jnp.max(s1, axis=0, keepdims=True)))
        alpha = jnp.exp2(m - m_new)
        p0, p1 = jnp.exp2(s0 - m_new), jnp.exp2(s1 - m_new)
        l = alpha * l + jnp.sum(p0, axis=0, keepdims=True) + jnp.sum(p1, axis=0, keepdims=True)
        acc = (alpha * acc + _dot(sv_ref[0, 0, :, pl.ds(k0, SEL_TILE)], p0.astype(BF16))
               + _dot(sv_ref[0, 0, :, pl.ds(k1, SEL_TILE)], p1.astype(BF16)))
        return m_new, l, acc

    def sel_single(kt, carry):
        k0 = pl.multiple_of(kt * SEL_TILE, SEL_TILE)
        return _flash_step(carry, sel_scores(k0), sv_ref[0, 0, :, pl.ds(k0, SEL_TILE)])

    n_full = t0 // SEL_TILE
    carry = lax.fori_loop(0, n_full // 2, sel_pair, init)
    carry = lax.fori_loop(n_full - n_full % 2, n_full, sel_single, carry)
    k0 = pl.multiple_of(n_full * SEL_TILE, SEL_TILE)
    kpos = k0 + lax.broadcasted_iota(jnp.int32, (SEL_TILE, 1), 0)
    sc = jnp.where(kpos <= tcol, _dot(sk_ref[0, 0, pl.ds(k0, SEL_TILE), :], qaug), NEG)
    _, l_s, acc_s = _flash_step(carry, sc, sv_ref[0, 0, :, pl.ds(k0, SEL_TILE)])
    o_s = acc_s * (1.0 / l_s)

    span = min(WINDOW + qlen, wk_ref.shape[2])
    kw0 = pl.multiple_of(jnp.maximum(t0 - WINDOW, 0), qlen)
    kpos = kw0 + lax.broadcasted_iota(jnp.int32, (span, 1), 0)
    s_w = _dot(wk_ref[0, 0, pl.ds(kw0, span), :], qt)
    s_w = jnp.where(kpos <= tcol, jnp.where(kpos >= tcol - WINDOW, s_w, NEG), NEG)
    p_w = jnp.exp2(s_w - jnp.max(s_w, axis=0, keepdims=True))
    o_w = _dot(wv_ref[0, 0, :, pl.ds(kw0, span)], p_w.astype(BF16)) * (1.0 / jnp.sum(p_w, axis=0, keepdims=True))

    gate = _sigmoid(g_ref[0, 0])
    outs = []
    for g in range(NSA_GQ):
        c = slice(g * qlen, (g + 1) * qlen)
        outs.append(gate[3 * g:3 * g + 1] * o_c[:, c] + gate[3 * g + 1:3 * g + 2] * o_s[:, c]
                    + gate[3 * g + 2:3 * g + 3] * o_w[:, c])
    o_ref[...] = jnp.concatenate(outs, axis=0).T.astype(o_ref.dtype)


def _overlap_matrix(ncmp, nblk):
    ci = np.arange(ncmp)[:, None] * CMP_STRIDE
    sj = np.arange(nblk)[None, :] * SEL_BLOCK
    return jnp.asarray(((ci <= sj + SEL_BLOCK - 1) & (ci + CMP_LEN - 1 >= sj)).astype(np.float32))


def _nsa_prompt(qt, kc, vct, skaug, svt, wkr, wvt, gates_t):
    b, _, length = qt.shape
    nb = length // QUERY_BLOCK
    ncmp = kc.shape[2]
    nblk = NSA_HD
    assert length // SEL_BLOCK <= nblk
    per_kv = lambda r, c: pl.BlockSpec((1, 1, r, c), lambda bi, k, i: (bi, k, 0, 0))
    return pl.pallas_call(
        _nsa_prompt_kernel,
        grid=(b, NSA_KV, nb),
        in_specs=[pl.BlockSpec((1, NSA_GQ * NSA_HD, QUERY_BLOCK), lambda bi, k, i: (bi, k, i)),
                  per_kv(ncmp, NSA_HD), per_kv(NSA_HD, ncmp), per_kv(length, 2 * NSA_HD), per_kv(NSA_HD, length),
                  per_kv(length, NSA_HD), per_kv(NSA_HD, length),
                  pl.BlockSpec((1, 1, 3 * NSA_GQ, QUERY_BLOCK), lambda bi, k, i: (bi, k, 0, i)),
                  pl.BlockSpec((nblk, ncmp), lambda bi, k, i: (0, 0))],
        out_specs=pl.BlockSpec((QUERY_BLOCK, NSA_GQ * NSA_HD), lambda bi, k, i: (bi * nb + i, k)),
        out_shape=jax.ShapeDtypeStruct((b * length, NSA_HEADS * NSA_HD), BF16),
        compiler_params=_cparams("parallel", "parallel", "arbitrary"),
        name="nsa_prompt",
    )(qt, kc, vct, skaug, svt, wkr, wvt, gates_t, _overlap_matrix(ncmp, nblk).T)


def _mem_prompt_kernel(q_ref, mk_ref, mv_ref, g_ref, o_ref):
    for h in range(MEM_HEADS):
        sl = slice(h * MEM_HD, (h + 1) * MEM_HD)
        q = q_ref[:, sl]
        ms = jnp.mean(q * q, axis=-1, keepdims=True)
        qn = (q * lax.rsqrt(ms + EPS) * g_ref[...] * (MEM_HD ** -0.5)).astype(BF16)
        s = _dot_nt(qn, mk_ref[0, :, sl])
        m = jnp.max(s, axis=-1, keepdims=True)
        p = jnp.exp(s - m)
        p = p / jnp.sum(p, axis=-1, keepdims=True)
        o_ref[:, sl] = _dot(p.astype(BF16), mv_ref[0, :, sl]).astype(o_ref.dtype)


def _mem_prompt(q_mem, mk, mv, qnorm, b, length, tq=256):
    nl = length // tq
    width = MEM_HEADS * MEM_HD
    mem = pl.BlockSpec((1, mk.shape[1], width), lambda bi, i: (bi, 0, 0))
    return pl.pallas_call(
        _mem_prompt_kernel,
        grid=(b, nl),
        in_specs=[pl.BlockSpec((tq, width), lambda bi, i: (bi * nl + i, 0)), mem, mem,
                  pl.BlockSpec((1, MEM_HD), lambda bi, i: (0, 0))],
        out_specs=pl.BlockSpec((tq, width), lambda bi, i: (bi * nl + i, 0)),
        out_shape=jax.ShapeDtypeStruct((b * length, width), BF16),
        compiler_params=_cparams("parallel", "parallel"),
        name="mem_prompt",
    )(q_mem, mk, mv, qnorm.reshape(1, -1))


def _merge_kernel(x_ref, ys_ref, yn_ref, ym_ref, gm_ref, bm_ref, ws_ref, wn_ref, wm_ref, wo_ref, o_ref):
    gate = _sigmoid(gm_ref[...] + bm_ref[...])
    mixed = (gate[:, :D_MODEL] * _dot(ys_ref[...], ws_ref[...])
             + gate[:, D_MODEL:2 * D_MODEL] * _dot(yn_ref[...], wn_ref[...])
             + gate[:, 2 * D_MODEL:] * _dot(ym_ref[...], wm_ref[...]))
    o_ref[...] = x_ref[...] + _dot(mixed.astype(BF16), wo_ref[...])


def _merge(x, ys, yn, ym, gm, wb, tm=256):
    m = x.shape[0]
    tm = min(tm, m)
    row = lambda wd: pl.BlockSpec((tm, wd), lambda i: (i, 0))
    full = lambda shape: pl.BlockSpec(shape, lambda i: (0,) * len(shape))
    return pl.pallas_call(
        _merge_kernel,
        grid=(m // tm,),
        in_specs=[row(D_MODEL), row(D_INNER), row(D_MODEL), row(D_MODEL), row(3 * D_MODEL), full((1, 3 * D_MODEL)),
                  full((D_INNER, D_MODEL)), full((D_MODEL, D_MODEL)), full((D_MODEL, D_MODEL)),
                  full((D_MODEL, D_MODEL))],
        out_specs=row(D_MODEL),
        out_shape=jax.ShapeDtypeStruct((m, D_MODEL), F32),
        compiler_params=_cparams("parallel"),
        name="merge",
    )(x, ys, yn, ym, gm, wb['b_merge'].reshape(1, -1), wb['w_ssd_o'], wb['w_nsa_o'], wb['w_mem_o'], wb['w_out'])


FFN_COLS = 2816


def _ffn_kernel(*refs, seq):
    if seq:
        x_ref, g_ref, wu_ref, cw_ref, cb_ref, wd_ref, o_ref, ul_ref, carry_ref = refs
    else:
        x_ref, g_ref, wu_ref, cw_ref, cb_ref, wd_ref, p0_ref, p1_ref, o_ref, ul_ref = refs
    x = x_ref[...]
    tm = x.shape[0]
    ms = jnp.mean(x * x, axis=-1, keepdims=True)
    xn = (x * lax.rsqrt(ms + EPS) * g_ref[...]).astype(BF16)
    if seq:
        @pl.when(pl.program_id(1) == 0)
        def _():
            carry_ref[...] = jnp.zeros_like(carry_ref)

        row = lax.broadcasted_iota(jnp.int32, (8, 1), 0)

    def conv_cols(lo):
        sl = slice(lo, lo + FFN_COLS)
        u = _dot(xn, wu_ref[:, sl])
        if seq:
            prev = carry_ref[:, sl]
            u1 = _shift_rows(u, 1, prev, row)
            u2 = _shift_rows(u, 2, prev, row)
            carry_ref[:, sl] = u[tm - 8:, :]
        else:
            u2, u1 = p0_ref[:, sl], p1_ref[:, sl]
            ul_ref[:, sl] = u
        return cb_ref[:, sl] + cw_ref[0:1, sl] * u2 + cw_ref[1:2, sl] * u1 + cw_ref[2:3, sl] * u

    acc = x
    for j in range(D_FF // FFN_COLS):
        act = (_silu(conv_cols(j * FFN_COLS)) * conv_cols(D_FF + j * FFN_COLS)).astype(BF16)
        acc = acc + _dot(act, wd_ref[j * FFN_COLS:(j + 1) * FFN_COLS, :])
    o_ref[...] = acc
    if seq:
        @pl.when(pl.program_id(1) == pl.num_programs(1) - 1)
        def _():
            ul_ref[0] = carry_ref[...]


def _ffn_prompt(x1, w, wb, b, length, tm=256):
    nl = length // tm
    row = lambda wd: pl.BlockSpec((tm, wd), lambda bi, i: (bi * nl + i, 0))
    const = lambda shape: pl.BlockSpec(shape, lambda bi, i: (0,) * len(shape), pipeline_mode=pl.Buffered(1))
    return pl.pallas_call(
        functools.partial(_ffn_kernel, seq=True),
        grid=(b, nl),
        in_specs=[row(D_MODEL), const((1, D_MODEL)), const((D_MODEL, 2 * D_FF)), const((FFN_CONV, 2 * D_FF)),
                  const((1, 2 * D_FF)), const((D_FF, D_MODEL))],
        out_specs=[row(D_MODEL), pl.BlockSpec((1, 8, 2 * D_FF), lambda bi, i: (bi, 0, 0))],
        out_shape=[jax.ShapeDtypeStruct((b * length, D_MODEL), F32), jax.ShapeDtypeStruct((b, 8, 2 * D_FF), F32)],
        scratch_shapes=[pltpu.VMEM((8, 2 * D_FF), F32)],
        compiler_params=_cparams("parallel", "arbitrary"),
        name="ffn_prompt",
    )(x1, w['norm_ffn'].reshape(1, -1), wb['w_up'], wb['ffn_conv_w'], wb['ffn_conv_b'].reshape(1, -1), wb['w_down'])


def _ffn_sample(x1, prev, w, wb):
    m = x1.shape[0]
    full = lambda shape: pl.BlockSpec(shape, lambda i: (0,) * len(shape))
    return pl.pallas_call(
        functools.partial(_ffn_kernel, seq=False),
        grid=(1,),
        in_specs=[full((m, D_MODEL)), full((1, D_MODEL)), full((D_MODEL, 2 * D_FF)), full((FFN_CONV, 2 * D_FF)),
                  full((1, 2 * D_FF)), full((D_FF, D_MODEL)), full((m, 2 * D_FF)), full((m, 2 * D_FF))],
        out_specs=[full((m, D_MODEL)), full((m, 2 * D_FF))],
        out_shape=[jax.ShapeDtypeStruct((m, D_MODEL), F32), jax.ShapeDtypeStruct((m, 2 * D_FF), F32)],
        compiler_params=_cparams("arbitrary"),
        name="ffn_sample",
    )(x1, w['norm_ffn'].reshape(1, -1), wb['w_up'], wb['ffn_conv_w'], wb['ffn_conv_b'].reshape(1, -1), wb['w_down'],
      prev[:, 0], prev[:, 1])


def _compress_sample(pool_k, pool_v, page_table, w):
    bs, npages = page_table.shape
    page = pool_k.shape[1]
    specs = [pl.BlockSpec((None, NSA_KV * NSA_HD, page), functools.partial(lambda b, pt, j: (pt[b, j], 0, 0), j=j))
             for j in range(npages)]
    return _compress_call([_pool_t(pool_k)] * npages + [_pool_t(pool_v)] * npages, specs + specs, (bs,),
                          npages * page // CMP_STRIDE, bs, w, n_prefetch=1, prefetch=(page_table,))


def _pool_t(pool):
    n, page = pool.shape[:2]
    return pool.transpose(0, 2, 3, 1).reshape(n, NSA_KV * NSA_HD, page)


def _rows8(x):
    return jnp.broadcast_to(x, (8, x.shape[1]))


def _ssd_step_kernel(xbc_ref, prev_ref, z_ref, dt_ref, h_ref, cw_ref, cb_ref, dtb_ref, a_ref, dskip_ref, nw_ref,
                     e64_ref, e128_ref, y_ref, hn_ref):
    conv = cb_ref[...] + cw_ref[3:4, :] * xbc_ref[0]
    for k in range(SSD_CONV - 1):
        conv = conv + cw_ref[k:k + 1, :] * prev_ref[0, k:k + 1, :]
    xc = _silu(conv)
    xs = xc[:, :D_INNER]
    nb = SSD_GROUPS * SSD_STATE
    bm = xc[:, D_INNER:D_INNER + nb]
    cm = xc[:, D_INNER + nb:]
    v = dt_ref[0] + dtb_ref[...]
    dt = jnp.maximum(v, 0.0) + jnp.log(1.0 + jnp.exp(-jnp.abs(v)))
    dec = jnp.exp(dt * a_ref[...])
    xdt = xs * _dot_x3(_rows8(dt), e64_ref[...], 0)[0:1]
    dec128 = _dot_x3(_rows8(dec), e128_ref[...], 0)
    pieces = []
    for c in range(D_INNER // 128):
        g = c // (SSD_HPG // 2)
        xcol = jnp.broadcast_to(xdt[:, c * 128:(c + 1) * 128], (128, 128)).T
        bg = jnp.broadcast_to(bm[:, g * SSD_STATE:(g + 1) * SSD_STATE], (128, SSD_STATE))
        decv = jnp.concatenate(
            [jnp.concatenate([dec128[:, hh * 128:(hh + 1) * 128]] * (SSD_HEAD_DIM // 8), axis=0)
             for hh in (2 * c, 2 * c + 1)], axis=0)
        hnew = decv * h_ref[0, c * 128:(c + 1) * 128, :] + xcol * bg
        hn_ref[0, c * 128:(c + 1) * 128, :] = hnew
        cg = _rows8(cm[:, g * SSD_STATE:(g + 1) * SSD_STATE]).astype(BF16)
        pieces.append(_dot_nt(cg, hnew.astype(BF16))[0:1])
    y = jnp.concatenate(pieces, axis=1) + dskip_ref[...] * xs
    yz = y * _silu(z_ref[0])
    ms = jnp.mean(yz * yz, axis=-1, keepdims=True)
    y_ref[0] = (yz * lax.rsqrt(ms + EPS) * nw_ref[...]).astype(y_ref.dtype)


def _ssd_step(xbc, prev, z, dt_raw, h0, w):
    bs = xbc.shape[0]
    a = -jnp.exp(w['ssd_a_log'].astype(F32))
    row = lambda wd: pl.BlockSpec((1, 1, wd), lambda i: (i, 0, 0))
    full = lambda shape: pl.BlockSpec(shape, lambda i: (0,) * len(shape))
    st = pl.BlockSpec((1, D_INNER, SSD_STATE), lambda i: (i, 0, 0))
    y, hn = pl.pallas_call(
        _ssd_step_kernel,
        grid=(bs,),
        in_specs=[row(SSD_CONV_DIM), pl.BlockSpec((1, SSD_CONV - 1, SSD_CONV_DIM), lambda i: (i, 0, 0)),
                  row(D_INNER), row(SSD_HEADS), st,
                  full((SSD_CONV, SSD_CONV_DIM)), full((1, SSD_CONV_DIM)), full((1, SSD_HEADS)), full((1, SSD_HEADS)),
                  full((1, D_INNER)), full((1, D_INNER)), full((SSD_HEADS, D_INNER)),
                  full((SSD_HEADS, SSD_HEADS * 128))],
        out_specs=[row(D_INNER), st],
        out_shape=[jax.ShapeDtypeStruct((bs, 1, D_INNER), BF16),
                   jax.ShapeDtypeStruct((bs, D_INNER, SSD_STATE), F32)],
        compiler_params=_cparams("parallel"),
        name="ssd_step",
    )(xbc.reshape(bs, 1, -1), prev, z.reshape(bs, 1, -1), dt_raw.reshape(bs, 1, -1),
      h0.reshape(bs, D_INNER, SSD_STATE), w['ssd_conv_w'], w['ssd_conv_b'].reshape(1, -1),
      w['ssd_dt_bias'].reshape(1, -1), a.reshape(1, -1),
      jnp.repeat(w['ssd_d'].astype(F32), SSD_HEAD_DIM).reshape(1, -1), w['ssd_norm'].reshape(1, -1),
      _head_expand(), _head_expand(128))
    return y.reshape(bs, D_INNER), hn.reshape(bs, SSD_HEADS, SSD_HEAD_DIM, SSD_STATE)


def _softmax_with_extra(s, s_new, valid=None):
    if valid is not None:
        s = jnp.where(valid, s, NEG)
    m = jnp.maximum(jnp.max(s, axis=-1, keepdims=True), s_new)
    p = jnp.exp(s - m)
    if valid is not None:
        p = jnp.where(valid, p, 0.0)
    p_new = jnp.exp(s_new - m)
    inv = 1.0 / (jnp.sum(p, axis=-1, keepdims=True) + p_new)
    return p * inv, p_new * inv


def _nsa_sample_kernel(*refs, npages, past_len, spb):
    q_ref, kv_ref, gate_ref = refs[1:4]
    skp = refs[4:4 + spb * npages]
    svp = refs[4 + spb * npages:4 + 2 * spb * npages]
    (kc_ref, vc_ref, wk_ref, wv_ref, qg_ref, kg_ref, cos_ref, sin_ref, segm_ref, ov_ref, eblk_ref, fold_ref,
     foldt_ref, o_ref, skn_ref, wkn_ref, wko_ref, wvo_ref) = refs[4 + 2 * spb * npages:]
    kvw = NSA_KV * NSA_HD
    segm = segm_ref[...]
    cos, sin = _rows8(cos_ref[...]), _rows8(sin_ref[...])
    hrow = lax.broadcasted_iota(jnp.int32, (NSA_HEADS, NSA_HEADS * NSA_HD), 0)
    hcol = lax.broadcasted_iota(jnp.int32, (NSA_HEADS, NSA_HEADS * NSA_HD), 1)
    own = (hcol >> 6) == hrow
    ncmp = kc_ref.shape[1]
    kend = lax.broadcasted_iota(jnp.int32, (1, ncmp), 1) * CMP_STRIDE + (CMP_LEN - 1)
    valid_c = kend <= past_len
    gi = lax.broadcasted_iota(jnp.int32, (NSA_HEADS, NSA_HEADS), 0) // NSA_GQ
    gj = lax.broadcasted_iota(jnp.int32, (NSA_HEADS, NSA_HEADS), 1) // NSA_GQ
    same_group = jnp.where(gi == gj, 1.0, 0.0)

    first = []
    for s in range(spb):
        q = _rope(_head_rmsnorm(_rows8(q_ref[s]), qg_ref[...], segm), cos, sin) * (NSA_HD ** -0.5)
        kv = kv_ref[s]
        sk_new = _rope(_head_rmsnorm(_rows8(kv[:, 2 * kvw:3 * kvw]), kg_ref[1:2, :], segm), cos, sin)[0:1]
        wk_new = _rope(_head_rmsnorm(_rows8(kv[:, 4 * kvw:5 * kvw]), kg_ref[2:3, :], segm), cos, sin)[0:1]
        skn_ref[s] = sk_new
        wkn_ref[s] = wk_new
        q16 = jnp.where(own, jnp.concatenate([q, q], axis=0), 0.0).astype(BF16)
        qbd = _dot(q16, fold_ref[...])
        qbd_b = qbd.astype(BF16)
        s_c = jnp.where(valid_c, _dot_nt(qbd_b, kc_ref[s].astype(BF16)), NEG)
        m_c = jnp.max(s_c, axis=-1, keepdims=True)
        p_c = jnp.where(valid_c, jnp.exp(s_c - m_c), 0.0)
        den = jnp.sum(p_c, axis=-1, keepdims=True)
        p_c = p_c / jnp.where(den > 0, den, 1.0)
        o_c = _dot(p_c.astype(BF16), vc_ref[s].astype(BF16))
        first.append((kv, sk_new, wk_new, qbd, qbd_b, o_c, _dot_hi(same_group, p_c)))

    psum = jnp.concatenate([f[6] for f in first] + [jnp.zeros((128 - spb * NSA_HEADS, ncmp), F32)], axis=0)
    imp = lax.dot_general(ov_ref[...], psum, (((1,), (1,)), ((), ())), preferred_element_type=F32,
                          precision=HI)
    nslot = ov_ref.shape[0]
    jb = lax.broadcasted_iota(jnp.int32, (nslot, 1), 0)
    cur = past_len // SEL_BLOCK
    forced = (jb == 0) | (jb == cur) | (jb == cur - 1)
    score = jnp.where(jb * SEL_BLOCK <= past_len, imp + jnp.where(forced, SEL_FORCE, 0.0), NEG)
    bias_all = _topn_bias(score, SEL_TOPN).T

    nwin = wk_ref.shape[2]
    lane = lax.broadcasted_iota(jnp.int32, (kvw, nwin), 1)

    def column(rowvec):
        col = jnp.broadcast_to(rowvec, (128, kvw)).T
        return jnp.concatenate([col] * (nwin // 128), axis=1)

    for s, (kv, sk_new, wk_new, qbd, qbd_b, o_c, _) in enumerate(first):
        sv_new = kv[:, 3 * kvw:4 * kvw]
        wv_new = kv[:, 5 * kvw:6 * kvw]
        bias = bias_all[s * NSA_HEADS:(s + 1) * NSA_HEADS]
        kmask = _dot(bias.astype(BF16), eblk_ref[...])
        pages = slice(s * npages, (s + 1) * npages)
        sk_all = jnp.concatenate([r[...] for r in skp[pages]], axis=1).astype(BF16)
        sv_all = jnp.concatenate([r[...] for r in svp[pages]], axis=1).astype(BF16)
        s_s = _dot(qbd_b, sk_all) + kmask
        s_new = jnp.sum(qbd * sk_new, axis=-1, keepdims=True)
        p_s, p_new = _softmax_with_extra(s_s, s_new)
        o_s = _dot_nt(p_s.astype(BF16), sv_all) + p_new * sv_new
        wk = wk_ref[s]
        wv = wv_ref[s]
        s_w = _dot(qbd_b, wk.astype(BF16))
        s_wn = jnp.sum(qbd * wk_new, axis=-1, keepdims=True)
        p_w, p_wn = _softmax_with_extra(s_w, s_wn)
        o_w = _dot_nt(p_w.astype(BF16), wv.astype(BF16)) + p_wn * wv_new
        g16 = jnp.where((lax.broadcasted_iota(jnp.int32, (NSA_HEADS, 3 * NSA_HEADS), 1) // 3)
                        == lax.broadcasted_iota(jnp.int32, (NSA_HEADS, 3 * NSA_HEADS), 0),
                        jnp.broadcast_to(_sigmoid(gate_ref[s]), (NSA_HEADS, 3 * NSA_HEADS)), 0.0)
        br = lax.broadcasted_iota(jnp.int32, (NSA_HEADS, 3 * NSA_HEADS), 1) % 3
        gsel = lambda r: jnp.sum(jnp.where(br == r, g16, 0.0), axis=-1, keepdims=True)
        o16 = gsel(0) * o_c + gsel(1) * o_s + gsel(2) * o_w
        ox = _dot(o16.astype(BF16), foldt_ref[...])
        o_ref[s] = jnp.sum(jnp.where(own, ox, 0.0), axis=0, keepdims=True).astype(o_ref.dtype)
        wko_ref[s] = jnp.where(lane == nwin - 1, column(wk_new), pltpu.roll(wk, nwin - 1, axis=1))
        wvo_ref[s] = jnp.where(lane == nwin - 1, column(wv_new), pltpu.roll(wv, nwin - 1, axis=1))


def _nsa_sample(q, kv, gates, pool_sk, pool_sv, page_table, kc, vc, win_k, win_v, w):
    bs, npages = page_table.shape
    page = pool_sk.shape[1]
    past_len = npages * page
    kvw = NSA_KV * NSA_HD
    nwin = win_k.shape[1]
    assert nwin == WINDOW and past_len % SEL_BLOCK == 0
    ncmp = kc.shape[1]
    nslot = 64
    assert past_len // SEL_BLOCK + 1 <= nslot
    cos128, sin128 = _rope_tables(jnp.full((1,), past_len, jnp.int32))
    overlap = _overlap_matrix(ncmp, nslot).T
    key_blk = np.arange(past_len) // SEL_BLOCK
    eblk = jnp.asarray((np.arange(nslot)[:, None] == key_blk[None, :]).astype(np.float32) * MASK_BIG, BF16)
    src = np.arange(NSA_HEADS * NSA_HD)
    dst = (src // NSA_HD // NSA_GQ) * NSA_HD + src % NSA_HD
    fold_np = np.zeros((NSA_HEADS * NSA_HD, kvw), np.float32)
    fold_np[src, dst] = 1.0
    fold = jnp.asarray(fold_np, BF16)
    foldt = jnp.asarray(fold_np.T, BF16)
    spb = 2 if bs % 2 == 0 else 1
    r3 = lambda t: t.reshape(bs, 1, -1)
    row = lambda wd: pl.BlockSpec((spb, 1, wd), lambda b, pt: (b, 0, 0))
    full = lambda shape: pl.BlockSpec(shape, lambda b, pt: (0,) * len(shape))
    per_b = lambda r, c: pl.BlockSpec((spb, r, c), lambda b, pt: (b, 0, 0))
    pages = [pl.BlockSpec((None, kvw, page), functools.partial(lambda b, pt, s, j: (pt[b * spb + s, j], 0, 0), s=s, j=j))
             for s in range(spb) for j in range(npages)]
    npages_all = spb * npages
    pk3 = _pool_t(pool_sk)
    pv3 = _pool_t(pool_sv)
    gs = pltpu.PrefetchScalarGridSpec(
        num_scalar_prefetch=1, grid=(bs // spb,),
        in_specs=[row(NSA_HEADS * NSA_HD), row(6 * kvw), row(3 * NSA_HEADS)] + pages + pages
        + [per_b(ncmp, kvw), per_b(ncmp, kvw), per_b(kvw, nwin), per_b(kvw, nwin),
           full((1, NSA_HEADS * NSA_HD)), full((3, kvw)), full((1, 128)), full((1, 128)), full((128, 128)),
           full((nslot, ncmp)), full((nslot, past_len)), full(fold.shape), full(foldt.shape)],
        out_specs=[row(NSA_HEADS * NSA_HD), row(kvw), row(kvw), per_b(kvw, nwin), per_b(kvw, nwin)])
    return pl.pallas_call(
        functools.partial(_nsa_sample_kernel, npages=npages, past_len=past_len, spb=spb),
        grid_spec=gs,
        out_shape=[jax.ShapeDtypeStruct((bs, 1, NSA_HEADS * NSA_HD), BF16),
                   jax.ShapeDtypeStruct((bs, 1, kvw), F32), jax.ShapeDtypeStruct((bs, 1, kvw), F32),
                   jax.ShapeDtypeStruct((bs, kvw, nwin), F32), jax.ShapeDtypeStruct((bs, kvw, nwin), F32)],
        compiler_params=_cparams("parallel"),
        name="nsa_sample",
    )(page_table, r3(q), r3(kv), r3(gates), *([pk3] * npages_all), *([pv3] * npages_all), kc, vc,
      _pool_t(win_k), _pool_t(win_v),
      jnp.tile(w['nsa_q_norm'], NSA_HEADS).reshape(1, -1), jnp.tile(w['nsa_k_norm'], (1, NSA_KV)),
      cos128, sin128, _segment_ones(), overlap, eblk, fold, foldt)


def _mem_sample_kernel(q_ref, mk_ref, mv_ref, g_ref, o_ref):
    q = q_ref[0]
    mtok = mk_ref.shape[1]
    rows = []
    for h in range(MEM_HEADS):
        qh = q[:, h * MEM_HD:(h + 1) * MEM_HD]
        ms = jnp.mean(qh * qh, axis=-1, keepdims=True)
        rows.append(qh * lax.rsqrt(ms + EPS) * g_ref[...] * (MEM_HD ** -0.5))
    q8 = jnp.concatenate(rows + rows, axis=0).astype(BF16)
    mk = mk_ref[0].reshape(mtok * MEM_HEADS, MEM_HD).astype(BF16)
    mv = mv_ref[0].reshape(mtok * MEM_HEADS, MEM_HD).astype(BF16)
    own = ((lax.broadcasted_iota(jnp.int32, (8, mtok * MEM_HEADS), 1) & (MEM_HEADS - 1))
           == (lax.broadcasted_iota(jnp.int32, (8, mtok * MEM_HEADS), 0) & (MEM_HEADS - 1)))
    s = jnp.where(own, _dot_nt(q8, mk), NEG)
    m = jnp.max(s, axis=-1, keepdims=True)
    p = jnp.exp(s - m)
    p = p / jnp.sum(p, axis=-1, keepdims=True)
    o8 = _dot(p.astype(BF16), mv)
    for h in range(MEM_HEADS):
        o_ref[0, :, h * MEM_HD:(h + 1) * MEM_HD] = o8[h:h + 1].astype(o_ref.dtype)


def _mem_sample(q_mem, mem_k, mem_v, qnorm):
    bs = q_mem.shape[0]
    width = MEM_HEADS * MEM_HD
    mtok = mem_k.shape[1]
    row = pl.BlockSpec((1, 1, width), lambda i: (i, 0, 0))
    mem = pl.BlockSpec((1, mtok, MEM_HEADS, MEM_HD), lambda i: (i, 0, 0, 0))
    return pl.pallas_call(
        _mem_sample_kernel,
        grid=(bs,),
        in_specs=[row, mem, mem, pl.BlockSpec((1, MEM_HD), lambda i: (0, 0))],
        out_specs=row,
        out_shape=jax.ShapeDtypeStruct((bs, 1, width), BF16),
        compiler_params=_cparams("parallel"),
        name="mem_sample",
    )(q_mem.reshape(bs, 1, width), mem_k, mem_v, qnorm.reshape(1, -1)).reshape(bs, width)


def _split_w_in(w_in):
    offs = np.cumsum((0,) + IN_SIZES)
    piece = lambda i: w_in[:, int(offs[i]):int(offs[i + 1])]
    small = jnp.concatenate([piece(2), piece(5)], axis=1)
    small = jnp.pad(small, ((0, 0), (0, 128 - small.shape[1])))
    return dict(z=piece(0), xbc=piece(1), small=small, q=piece(3), kv=piece(4), qmem=piece(6), gmerge=piece(7))


def _multi_mm_kernel(*refs, transposed):
    n_out = len(transposed)
    x_ref, g_ref = refs[:2]
    w_refs = refs[2:2 + n_out]
    o_refs = refs[2 + n_out:]
    x = x_ref[...]
    ms = jnp.mean(x * x, axis=-1, keepdims=True)
    xn = (x * lax.rsqrt(ms + EPS) * g_ref[...]).astype(BF16)
    for w_ref, o_ref, tr in zip(w_refs, o_refs, transposed):
        if tr:
            o_ref[0] = _dot_nt(w_ref[...], xn).astype(o_ref.dtype)
        else:
            o_ref[...] = _dot(xn, w_ref[...]).astype(o_ref.dtype)


def _multi_matmul(x, g, weights, transposed, seq_len, tm=256):
    m, k = x.shape
    tm = min(tm, m)
    assert m % tm == 0 and (not any(transposed) or seq_len % tm == 0)
    nl = seq_len // tm if any(transposed) else 1
    const = lambda shape: pl.BlockSpec(shape, lambda i: (0, 0), pipeline_mode=pl.Buffered(1))
    out_specs, out_shape = [], []
    for wm, tr in zip(weights, transposed):
        if tr:
            out_specs.append(pl.BlockSpec((1, wm.shape[0], tm), lambda i: (i // nl, 0, i % nl)))
            out_shape.append(jax.ShapeDtypeStruct((m // seq_len, wm.shape[0], seq_len), F32))
        else:
            out_specs.append(pl.BlockSpec((tm, wm.shape[1]), lambda i: (i, 0)))
            out_shape.append(jax.ShapeDtypeStruct((m, wm.shape[1]), F32))
    return pl.pallas_call(
        functools.partial(_multi_mm_kernel, transposed=tuple(transposed)),
        grid=(m // tm,),
        in_specs=[pl.BlockSpec((tm, k), lambda i: (i, 0)), const((1, k))] + [const(wm.shape) for wm in weights],
        out_specs=out_specs,
        out_shape=out_shape,
        compiler_params=_cparams("parallel"),
        name="in_proj",
    )(x, g.reshape(1, k).astype(F32), *weights)


def _in_proj(x, norm_w, wi, feature_major=(), seq_len=1):
    names = list(wi)
    weights = [wi[n].T if n in feature_major else wi[n] for n in names]
    proj = dict(zip(names, _multi_matmul(x, norm_w, weights, [n in feature_major for n in names], seq_len)))
    proj['dt'] = proj['small'][:, :SSD_HEADS]
    proj['gnsa'] = proj['small'][:, SSD_HEADS:SSD_HEADS + 3 * NSA_HEADS]
    return proj


def _prompt_layer(x_prompt, mem_prompt, w, wb, wi):
    b, length, _ = x_prompt.shape
    n = b * length
    kvw = NSA_KV * NSA_HD
    x = x_prompt.reshape(n, D_MODEL)
    proj = _in_proj(x, w['norm_mix'], wi, feature_major=('q', 'kv'), seq_len=length)
    xbc = proj['xbc'].reshape(b, length, SSD_CONV_DIM)
    y_ssd, ssd_state = _ssd_prompt(xbc, proj['z'].reshape(b, length, D_INNER),
                                   proj['dt'].reshape(b, length, SSD_HEADS), w)
    kvt = proj['kv']
    qt, skaug, skf, svt, wkr, wkf, wvt = _nsa_prep(proj['q'], kvt, w)
    kc, vc = _compress_prompt(kvt, w)
    heads = lambda t: t.reshape(b, -1, NSA_KV, NSA_HD).astype(BF16)
    t4 = lambda t: t.reshape(b, NSA_KV, NSA_HD, -1)
    gates_t = proj['gnsa'].reshape(b, length, NSA_KV, 3 * NSA_GQ).transpose(0, 2, 3, 1)
    o_nsa = _nsa_prompt(qt, heads(kc).transpose(0, 2, 1, 3), heads(vc).transpose(0, 2, 3, 1), skaug, t4(svt), wkr,
                        t4(wvt), gates_t)
    mem = mem_prompt.reshape(-1, D_MODEL)
    mkv = _matmul(mem, wb['w_mem_kv'], w['mem_norm'], tm=256)
    mtok = mem_prompt.shape[1]
    mk = mkv[:, :MEM_HEADS * MEM_HD].reshape(b, mtok, MEM_HEADS, MEM_HD)
    mk = mk * lax.rsqrt(jnp.mean(mk * mk, axis=-1, keepdims=True) + EPS) * w['mem_k_norm']
    mv = mkv[:, MEM_HEADS * MEM_HD:].reshape(b, mtok, MEM_HEADS, MEM_HD)
    o_mem = _mem_prompt(proj['qmem'], mk.reshape(b, mtok, -1).astype(BF16), mv.reshape(b, mtok, -1).astype(BF16),
                        w['mem_q_norm'], b, length)
    x1 = _merge(x, y_ssd.reshape(n, D_INNER), o_nsa, o_mem, proj['gmerge'], wb)
    y, u_last = _ffn_prompt(x1, w, wb, b, length)
    from_t = lambda t: t4(t).transpose(0, 3, 1, 2)
    keep = min(WINDOW, length)
    state = (from_t(kvt[:, 0:kvw]), from_t(kvt[:, kvw:2 * kvw]), from_t(skf), from_t(kvt[:, 3 * kvw:4 * kvw]),
             from_t(wkf[:, :, -keep:]), from_t(kvt[:, 5 * kvw:6 * kvw, -keep:]),
             xbc[:, -(SSD_CONV - 1):], ssd_state, u_last[:, -(FFN_CONV - 1):], mk, mv)
    return y.reshape(b, length, D_MODEL), state


def _sample_layer(x_sample, caches, page_table, w, wb, wi):
    (pool_ck, pool_cv, pool_sk, pool_sv, win_k, win_v, conv_prev, ssd_h0, mem_k, mem_v, ffn_prev) = caches
    bs = x_sample.shape[0]
    kvw = NSA_KV * NSA_HD
    x = x_sample.reshape(bs, D_MODEL)
    proj = _in_proj(x, w['norm_mix'], wi)
    y_ssd, ssd_state = _ssd_step(proj['xbc'], conv_prev, proj['z'], proj['dt'], ssd_h0, w)
    kv = proj['kv']
    kc, vc = _compress_sample(pool_ck, pool_cv, page_table, w)
    o_nsa, sk_new, wk_new, win_k_new, win_v_new = _nsa_sample(
        proj['q'], kv, proj['gnsa'], pool_sk, pool_sv, page_table, kc, vc, win_k, win_v, w)
    o_mem = _mem_sample(proj['qmem'], mem_k, mem_v, w['mem_q_norm'])
    x1 = _merge(x, y_ssd, o_nsa.reshape(bs, -1), o_mem, proj['gmerge'], wb)
    y, u = _ffn_sample(x1, ffn_prev, w, wb)
    r4 = lambda t: t.reshape(bs, -1, NSA_KV, NSA_HD)
    from_t = lambda t: t.reshape(bs, NSA_KV, NSA_HD, -1).transpose(0, 3, 1, 2)
    state = (r4(kv[:, 0:kvw]), r4(kv[:, kvw:2 * kvw]), r4(sk_new), r4(kv[:, 3 * kvw:4 * kvw]),
             from_t(win_k_new), from_t(win_v_new),
             jnp.concatenate([conv_prev[:, 1:], proj['xbc'][:, None]], axis=1), ssd_state,
             jnp.concatenate([ffn_prev[:, 1:], u[:, None]], axis=1))
    return y.reshape(bs, 1, D_MODEL), state


def kernel(x_prompt, x_sample, cache_nsa_cmp_k, cache_nsa_cmp_v, cache_nsa_sel_k, cache_nsa_sel_v, state_nsa_win_k, state_nsa_win_v, state_ssd_conv, state_ssd, cache_mem_k, cache_mem_v, state_ffn_conv, page_table, mem_prompt, norm_mix, w_in, b_merge, ssd_conv_w, ssd_conv_b, ssd_dt_bias, ssd_a_log, ssd_d, ssd_norm, w_ssd_o, nsa_q_norm, nsa_k_norm, cmp_k_pe, cmp_k_w1, cmp_k_w2, cmp_v_pe, cmp_v_w1, cmp_v_w2, w_nsa_o, mem_norm, w_mem_kv, mem_q_norm, mem_k_norm, w_mem_o, w_out, norm_ffn, w_up, ffn_conv_w, ffn_conv_b, w_down):
    weights = dict(norm_mix=norm_mix, w_in=w_in, b_merge=b_merge, ssd_conv_w=ssd_conv_w,
                   ssd_conv_b=ssd_conv_b, ssd_dt_bias=ssd_dt_bias, ssd_a_log=ssd_a_log, ssd_d=ssd_d,
                   ssd_norm=ssd_norm, w_ssd_o=w_ssd_o, nsa_q_norm=nsa_q_norm, nsa_k_norm=nsa_k_norm,
                   cmp_k_pe=cmp_k_pe, cmp_k_w1=cmp_k_w1, cmp_k_w2=cmp_k_w2, cmp_v_pe=cmp_v_pe,
                   cmp_v_w1=cmp_v_w1, cmp_v_w2=cmp_v_w2, w_nsa_o=w_nsa_o, mem_norm=mem_norm,
                   w_mem_kv=w_mem_kv, mem_q_norm=mem_q_norm, mem_k_norm=mem_k_norm, w_mem_o=w_mem_o,
                   w_out=w_out, norm_ffn=norm_ffn, w_up=w_up, ffn_conv_w=ffn_conv_w,
                   ffn_conv_b=ffn_conv_b, w_down=w_down)
    w = {name: arr[0] for name, arr in weights.items()}
    wb = dict(w)
    for name in ('w_ssd_o', 'w_nsa_o', 'w_mem_o', 'w_out', 'w_up', 'w_down', 'w_mem_kv'):
        wb[name] = w[name].astype(BF16)
    wi = {name: piece.astype(BF16) for name, piece in _split_w_in(w['w_in']).items()}
    y_p, st_p = _prompt_layer(x_prompt, mem_prompt, w, wb, wi)
    caches = (cache_nsa_cmp_k[0], cache_nsa_cmp_v[0], cache_nsa_sel_k[0], cache_nsa_sel_v[0], state_nsa_win_k[0],
              state_nsa_win_v[0], state_ssd_conv[0], state_ssd[0], cache_mem_k[0], cache_mem_v[0], state_ffn_conv[0])
    y_s, st_s = _sample_layer(x_sample, caches, page_table, w, wb, wi)
    return (y_p, y_s) + tuple(s[None] for s in st_p) + tuple(s[None] for s in st_s)
```

```python
import functools
import math

import numpy as np
import jax
import jax.numpy as jnp
from jax import lax
from jax.experimental import pallas as pl
from jax.experimental.pallas import tpu as pltpu

D_MODEL = 1024
D_INNER = 2048
SSD_HEAD_DIM = 64
SSD_HEADS = 32
SSD_GROUPS = 4
SSD_HPG = 8
SSD_STATE = 128
SSD_CONV = 4
SSD_CONV_DIM = 3072
SSD_CHUNK = 128
NSA_HEADS = 16
NSA_KV = 4
NSA_GQ = 4
NSA_HD = 64
CMP_LEN = 32
CMP_STRIDE = 16
CMP_HIDDEN = 128
SEL_BLOCK = 64
SEL_TOPN = 16
SEL_FORCE = 1.0e4
WINDOW = 512
QUERY_BLOCK = 256
MEM_HEADS = 4
MEM_HD = 256
D_FF = 2816
FFN_CONV = 3
ROPE_THETA = 10000.0
EPS = 1e-6
IN_SIZES = (2048, 3072, 32, 1024, 1536, 48, 1024, 3072)

VMEM_LIMIT = 48 * 1024 * 1024
MASK_BIG = 2.0 ** 20
NEG = -1.0e30

F32 = jnp.float32
BF16 = jnp.bfloat16
HI = lax.Precision.HIGHEST


def _cparams(*sem):
    return pltpu.CompilerParams(dimension_semantics=sem, vmem_limit_bytes=VMEM_LIMIT)


def _sigmoid(x):
    return 1.0 / (1.0 + jnp.exp(-x))


def _silu(x):
    return x * _sigmoid(x)


def _dot(a, b):
    return jnp.dot(a, b, preferred_element_type=F32)


def _dot_nt(a, b):
    return lax.dot_general(a, b, (((1,), (1,)), ((), ())), preferred_element_type=F32)


def _dot_hi(a, b):
    return jnp.dot(a, b, preferred_element_type=F32, precision=HI)


def _split3(x):
    hi = x.astype(BF16)
    r1 = x - hi.astype(F32)
    mid = r1.astype(BF16)
    lo = (r1 - mid.astype(F32)).astype(BF16)
    return hi, mid, lo


def _dot_x3(a, b, split):
    if split == 0:
        parts = [_dot(t, b) for t in _split3(a)]
    else:
        parts = [_dot(a, t) for t in _split3(b)]
    return parts[0] + parts[1] + parts[2]


def _shift_rows(x, k, prev, row):
    r = pltpu.roll(x, k, axis=0)
    head = r[:8]
    for i in range(k):
        head = jnp.where(row == i, prev[8 - k + i:8 - k + i + 1, :], head)
    return jnp.concatenate([head, r[8:]], axis=0)


def _ssd_prompt_kernel(xbc_ref, z_ref, dt_ref, dtT_ref, cw_ref, cb_ref, dtb_ref, dtbT_ref, a_ref, aT_ref,
                       dskip_ref, nw_ref, e_ref, y_ref, hT_ref, carry_ref, h_ref, yacc_ref):
    c = pl.program_id(1)
    q = SSD_CHUNK

    @pl.when(c == 0)
    def _():
        carry_ref[...] = jnp.zeros_like(carry_ref)
        h_ref[...] = jnp.zeros_like(h_ref)

    xbc = xbc_ref[0]
    row = lax.broadcasted_iota(jnp.int32, (8, 1), 0)
    prev = carry_ref[...]
    conv = cb_ref[...] + cw_ref[3:4, :] * xbc
    for k in range(1, SSD_CONV):
        conv = conv + cw_ref[3 - k:4 - k, :] * _shift_rows(xbc, k, prev, row)
    carry_ref[...] = xbc[q - 8:, :]
    xc = _silu(conv)
    xs = xc[:, :D_INNER]
    bm = xc[:, D_INNER:D_INNER + SSD_GROUPS * SSD_STATE].astype(BF16)
    cm = xc[:, D_INNER + SSD_GROUPS * SSD_STATE:].astype(BF16)

    def softplus(v):
        return jnp.maximum(v, 0.0) + jnp.log(1.0 + jnp.exp(-jnp.abs(v)))

    dt = softplus(dt_ref[0] + dtb_ref[...])
    dtT = softplus(dtT_ref[0] + dtbT_ref[...])
    ii = lax.broadcasted_iota(jnp.int32, (q, q), 0)
    jj = lax.broadcasted_iota(jnp.int32, (q, q), 1)
    causal = ii >= jj
    cum = _dot_x3(jnp.where(causal, 1.0, 0.0).astype(BF16), dt * a_ref[...], 1)
    cumT = _dot_x3(dtT * aT_ref[...], jnp.where(jj >= ii, 1.0, 0.0).astype(BF16), 0)
    ecum = jnp.exp(cum)
    dend = jnp.exp(cum[q - 1:q, :] - cum)
    stacked = jnp.concatenate([dt, ecum, dend], axis=0)
    hi = stacked.astype(BF16)
    lo = (stacked - hi.astype(F32)).astype(BF16)
    spread = _dot(hi, e_ref[...]) + _dot(lo, e_ref[...])
    dt_x, ecum_x, dend_x = spread[:q], spread[q:2 * q], spread[2 * q:]
    xdt = xs * dt_x
    xdt_b = xdt.astype(BF16)
    xw_b = (xdt * dend_x).astype(BF16)
    gw = SSD_HPG * SSD_HEAD_DIM
    for g in range(SSD_GROUPS):
        bg = bm[:, g * SSD_STATE:(g + 1) * SSD_STATE]
        cg = cm[:, g * SSD_STATE:(g + 1) * SSD_STATE]
        cb = _dot_nt(cg, bg)
        h_prev = h_ref[g]
        yoff = _dot(cg, h_prev.astype(BF16)) * ecum_x[:, g * gw:(g + 1) * gw]
        st = _dot(bg.astype(F32).T.astype(BF16), xw_b[:, g * gw:(g + 1) * gw])
        h_ref[g] = h_prev * ecum_x[q - 1:q, g * gw:(g + 1) * gw] + st
        for eh in range(SSD_HPG):
            hh = g * SSD_HPG + eh
            seg = cum[:, hh:hh + 1] - cumT[hh:hh + 1, :]
            decay = jnp.exp(jnp.where(causal, seg, NEG))
            mm = (cb * decay).astype(BF16)
            lo = hh * SSD_HEAD_DIM
            yd = _dot(mm, xdt_b[:, lo:lo + SSD_HEAD_DIM])
            yacc_ref[:, lo:lo + SSD_HEAD_DIM] = yd + yoff[:, eh * SSD_HEAD_DIM:(eh + 1) * SSD_HEAD_DIM]
    y = yacc_ref[...] + dskip_ref[...] * xs
    yz = y * _silu(z_ref[0])
    ms = jnp.mean(yz * yz, axis=-1, keepdims=True)
    y_ref[0] = (yz * lax.rsqrt(ms + EPS) * nw_ref[...]).astype(y_ref.dtype)

    @pl.when(c == pl.num_programs(1) - 1)
    def _():
        hT_ref[0] = h_ref[...]


def _head_expand(width=SSD_HEAD_DIM):
    e = np.zeros((SSD_HEADS, SSD_HEADS * width), np.float32)
    for h in range(SSD_HEADS):
        e[h, h * width:(h + 1) * width] = 1.0
    return jnp.asarray(e, BF16)


def _ssd_prompt(xbc, z, dt_raw, w):
    b, length, _ = xbc.shape
    q = SSD_CHUNK
    nc = length // q
    dtT = jnp.swapaxes(dt_raw, 1, 2)
    a = -jnp.exp(w['ssd_a_log'].astype(F32))
    full = lambda shape: pl.BlockSpec(shape, lambda i, j: (0,) * len(shape))
    y, hT = pl.pallas_call(
        _ssd_prompt_kernel,
        grid=(b, nc),
        in_specs=[pl.BlockSpec((1, q, SSD_CONV_DIM), lambda i, j: (i, j, 0)),
                  pl.BlockSpec((1, q, D_INNER), lambda i, j: (i, j, 0)),
                  pl.BlockSpec((1, q, SSD_HEADS), lambda i, j: (i, j, 0)),
                  pl.BlockSpec((1, SSD_HEADS, q), lambda i, j: (i, 0, j)),
                  full((SSD_CONV, SSD_CONV_DIM)), full((1, SSD_CONV_DIM)),
                  full((1, SSD_HEADS)), full((SSD_HEADS, 1)), full((1, SSD_HEADS)), full((SSD_HEADS, 1)),
                  full((1, D_INNER)), full((1, D_INNER)), full((SSD_HEADS, D_INNER))],
        out_specs=[pl.BlockSpec((1, q, D_INNER), lambda i, j: (i, j, 0)),
                   pl.BlockSpec((1, SSD_GROUPS, SSD_STATE, SSD_HPG * SSD_HEAD_DIM), lambda i, j: (i, 0, 0, 0))],
        out_shape=[jax.ShapeDtypeStruct((b, length, D_INNER), BF16),
                   jax.ShapeDtypeStruct((b, SSD_GROUPS, SSD_STATE, SSD_HPG * SSD_HEAD_DIM), F32)],
        scratch_shapes=[pltpu.VMEM((8, SSD_CONV_DIM), F32),
                        pltpu.VMEM((SSD_GROUPS, SSD_STATE, SSD_HPG * SSD_HEAD_DIM), F32),
                        pltpu.VMEM((q, D_INNER), F32)],
        compiler_params=_cparams("parallel", "arbitrary"),
        name="ssd_prompt",
    )(xbc, z, dt_raw, dtT, w['ssd_conv_w'], w['ssd_conv_b'].reshape(1, -1),
      w['ssd_dt_bias'].reshape(1, -1), w['ssd_dt_bias'].reshape(-1, 1), a.reshape(1, -1), a.reshape(-1, 1),
      jnp.repeat(w['ssd_d'].astype(F32), SSD_HEAD_DIM).reshape(1, -1), w['ssd_norm'].reshape(1, -1),
      _head_expand())
    state = hT.reshape(b, SSD_GROUPS, SSD_STATE, SSD_HPG, SSD_HEAD_DIM).transpose(0, 1, 3, 4, 2)
    return y, state.reshape(b, SSD_HEADS, SSD_HEAD_DIM, SSD_STATE)


def _segment_ones():
    i = np.arange(128)
    return jnp.asarray((i[:, None] // NSA_HD == i[None, :] // NSA_HD).astype(np.float32), BF16)


def _head_rmsnorm(x, g_row, segm):
    sq = x * x
    hi = sq.astype(BF16)
    lo = (sq - hi.astype(F32)).astype(BF16)
    parts = []
    for c in range(x.shape[1] // 128):
        sl = slice(c * 128, (c + 1) * 128)
        parts.append(_dot(hi[:, sl], segm) + _dot(lo[:, sl], segm))
    ss = parts[0] if len(parts) == 1 else jnp.concatenate(parts, axis=1)
    return x * lax.rsqrt(ss * (1.0 / NSA_HD) + EPS) * g_row


def _tile_lanes(t, width):
    reps = width // t.shape[1]
    return t if reps == 1 else jnp.concatenate([t] * reps, axis=1)


def _rope(x, cos128, sin128):
    width = x.shape[1]
    lane = lax.broadcasted_iota(jnp.int32, x.shape, 1)
    first = (lane & (NSA_HD // 2)) == 0
    rot = jnp.where(first, pltpu.roll(x, width - NSA_HD // 2, axis=1), pltpu.roll(x, NSA_HD // 2, axis=1))
    return x * _tile_lanes(cos128, width) + rot * _tile_lanes(sin128, width)


def _rope_tables(pos):
    half = NSA_HD // 2
    inv = ROPE_THETA ** (-jnp.arange(half, dtype=F32) / half)
    ang = pos.astype(F32)[:, None] * inv[None, :]
    cos, sin = jnp.cos(ang), jnp.sin(ang)
    cos128 = jnp.concatenate([cos, cos, cos, cos], axis=1)
    sin128 = jnp.concatenate([-sin, sin, -sin, sin], axis=1)
    return cos128, sin128


def _norm_rope_t(x, g_col, cos, sin):
    nh = x.shape[0] // NSA_HD
    x3 = x.reshape(nh, NSA_HD, x.shape[1])
    ms = jnp.sum(x3 * x3, axis=1, keepdims=True) * (1.0 / NSA_HD)
    x3 = x3 * lax.rsqrt(ms + EPS) * g_col[None]
    half = NSA_HD // 2
    x1, x2 = x3[:, :half], x3[:, half:]
    out = jnp.concatenate([x1 * cos[None] - x2 * sin[None], x2 * cos[None] + x1 * sin[None]], axis=1)
    return out.reshape(x.shape)


def _nsa_prep_kernel(q_ref, kv_ref, cos_ref, sin_ref, qg_ref, kg_ref,
                     qt_ref, skaug_ref, skf_ref, svt_ref, wkr_ref, wkf_ref, wvt_ref):
    tl = q_ref.shape[2]
    i = pl.program_id(1)
    cos, sin = cos_ref[...], sin_ref[...]
    kvw = NSA_KV * NSA_HD
    qt_ref[0] = (_norm_rope_t(q_ref[0], qg_ref[...], cos, sin) * (NSA_HD ** -0.5 * math.log2(math.e))).astype(BF16)
    sk = _norm_rope_t(kv_ref[0, 2 * kvw:3 * kvw, :], kg_ref[:, 1:2], cos, sin)
    wk = _norm_rope_t(kv_ref[0, 4 * kvw:5 * kvw, :], kg_ref[:, 2:3], cos, sin)
    skf_ref[0] = sk
    wkf_ref[0] = wk
    svt_ref[0] = kv_ref[0, 3 * kvw:4 * kvw, :].astype(BF16)
    wvt_ref[0] = kv_ref[0, 5 * kvw:6 * kvw, :].astype(BF16)
    pos = i * tl + lax.broadcasted_iota(jnp.int32, (tl, NSA_HD), 0)
    blk = lax.broadcasted_iota(jnp.int32, (tl, NSA_HD), 1)
    onehot = jnp.where((pos >> 6) == blk, MASK_BIG, 0.0).astype(BF16)
    for k in range(NSA_KV):
        sl = slice(k * NSA_HD, (k + 1) * NSA_HD)
        skaug_ref[0, k] = jnp.concatenate([sk[sl].T.astype(BF16), onehot], axis=1)
        wkr_ref[0, k] = wk[sl].T.astype(BF16)


def _nsa_prep(qt, kvt, w, tl=256):
    b, _, length = qt.shape
    nl = length // tl
    half = NSA_HD // 2
    inv = ROPE_THETA ** (-jnp.arange(half, dtype=F32) / half)
    ang = inv[:, None] * jnp.arange(length, dtype=F32)[None, :]
    kvw = NSA_KV * NSA_HD
    fm = lambda rows: pl.BlockSpec((1, rows, tl), lambda bi, i: (bi, 0, i))
    tab = pl.BlockSpec((half, tl), lambda bi, i: (0, i))
    full = lambda shape: pl.BlockSpec(shape, lambda bi, i: (0,) * len(shape))
    rows = lambda wd: pl.BlockSpec((1, NSA_KV, tl, wd), lambda bi, i: (bi, 0, i, 0))
    return pl.pallas_call(
        _nsa_prep_kernel,
        grid=(b, nl),
        in_specs=[fm(NSA_HEADS * NSA_HD), fm(6 * kvw), tab, tab, full((NSA_HD, 1)), full((NSA_HD, 3))],
        out_specs=[fm(NSA_HEADS * NSA_HD), rows(2 * NSA_HD), fm(kvw), fm(kvw), rows(NSA_HD), fm(kvw), fm(kvw)],
        out_shape=[jax.ShapeDtypeStruct((b, NSA_HEADS * NSA_HD, length), BF16),
                   jax.ShapeDtypeStruct((b, NSA_KV, length, 2 * NSA_HD), BF16),
                   jax.ShapeDtypeStruct((b, kvw, length), F32),
                   jax.ShapeDtypeStruct((b, kvw, length), BF16),
                   jax.ShapeDtypeStruct((b, NSA_KV, length, NSA_HD), BF16),
                   jax.ShapeDtypeStruct((b, kvw, length), F32),
                   jax.ShapeDtypeStruct((b, kvw, length), BF16)],
        compiler_params=_cparams("parallel", "parallel"),
        name="nsa_prep",
    )(qt, kvt, jnp.cos(ang), jnp.sin(ang), w['nsa_q_norm'].reshape(-1, 1), w['nsa_k_norm'].T)


def _compress_kernel(*refs, n_x, n_prefetch):
    refs = refs[n_prefetch:]
    (wk_ref, pek_ref, w2k_ref, wv_ref, pev_ref, w2v_ref, g_ref, cos_ref, sin_ref, segm_ref, ok_ref, ov_ref,
     x0k_ref, x1k_ref, x0v_ref, x1v_ref) = refs[2 * n_x:]
    shared = (g_ref, cos_ref, sin_ref, segm_ref)
    _compress_one(refs[:n_x], wk_ref, pek_ref, w2k_ref, shared, ok_ref, x0k_ref, x1k_ref, True)
    _compress_one(refs[n_x:2 * n_x], wv_ref, pev_ref, w2v_ref, shared, ov_ref, x0v_ref, x1v_ref, False)


def _compress_one(page_refs, wbd_ref, pe_ref, w2_ref, shared, o_ref, x0_ref, x1_ref, is_k):
    g_ref, cos_ref, sin_ref, segm_ref = shared
    plen = page_refs[0].shape[-1]
    for j, r in enumerate(page_refs):
        rows = r[...].T
        x0_ref[j * plen:(j + 1) * plen, :] = rows[:, :128]
        x1_ref[j * plen:(j + 1) * plen, :] = rows[:, 128:]
    nchunk = x0_ref.shape[0] // CMP_STRIDE
    row = lax.broadcasted_iota(jnp.int32, (nchunk, 1), 0)
    outs = []
    for x_ref in (x0_ref, x1_ref):
        x = jnp.concatenate([x_ref[pl.ds(s, nchunk, stride=CMP_STRIDE), :].astype(BF16)
                             for s in range(CMP_STRIDE)], axis=1)
        acc = _dot(x, wbd_ref[...])
        hid = []
        for kl in range(2):
            lo = kl * 2 * CMP_HIDDEN
            first = acc[:, lo:lo + CMP_HIDDEN]
            second = pltpu.roll(acc[:, lo + CMP_HIDDEN:lo + 2 * CMP_HIDDEN], nchunk - 1, axis=0)
            hid.append(_silu(first + second + pe_ref[...]))
        outs.append(_dot(jnp.concatenate(hid, axis=1).astype(BF16), w2_ref[...]))
    out = jnp.concatenate(outs, axis=1)
    if is_k:
        out = _rope(_head_rmsnorm(out, g_ref[...], segm_ref[...]), cos_ref[...], sin_ref[...])
    o_ref[0] = jnp.where(row < nchunk - 1, out, 0.0)


def _compress_weights(pe, w1, w2):
    w1r = w1.reshape(CMP_LEN, NSA_HD, CMP_HIDDEN)
    pe_term = jnp.einsum('ld,ldm->m', pe, w1r, precision=HI).reshape(1, CMP_HIDDEN)
    both = jnp.concatenate([w1r[:CMP_STRIDE], w1r[CMP_STRIDE:]], axis=-1)
    zero = jnp.zeros_like(both)
    wbd = jnp.concatenate([jnp.concatenate([both, zero], axis=-1), jnp.concatenate([zero, both], axis=-1)], axis=1)
    zero2 = jnp.zeros_like(w2)
    w2bd = jnp.concatenate([jnp.concatenate([w2, zero2], axis=-1), jnp.concatenate([zero2, w2], axis=-1)], axis=0)
    wbd = wbd.reshape(CMP_STRIDE * 2 * NSA_HD, 4 * CMP_HIDDEN)
    return wbd.astype(BF16), pe_term, w2bd.astype(BF16)


def _compress_call(x_args, x_specs, grid, nchunk, batch, w, n_prefetch=0, prefetch=()):
    wk = _compress_weights(w['cmp_k_pe'], w['cmp_k_w1'], w['cmp_k_w2'])
    wv = _compress_weights(w['cmp_v_pe'], w['cmp_v_w1'], w['cmp_v_w2'])
    cos128, sin128 = _rope_tables(jnp.arange(nchunk, dtype=jnp.int32) * CMP_STRIDE + (CMP_LEN - 1))
    kvw = NSA_KV * NSA_HD
    full = lambda shape: pl.BlockSpec(shape, lambda *a: (0,) * len(shape))
    wspecs = [full(wk[0].shape), full((1, CMP_HIDDEN)), full(wk[2].shape)]
    out = pl.BlockSpec((1, nchunk, kvw), lambda i, *a: (i, 0, 0))
    gs = pltpu.PrefetchScalarGridSpec(
        num_scalar_prefetch=n_prefetch, grid=grid,
        in_specs=list(x_specs) + wspecs + wspecs + [full((1, kvw)), full((nchunk, 128)), full((nchunk, 128)),
                                                    full((128, 128))],
        out_specs=[out, out],
        scratch_shapes=[pltpu.VMEM((nchunk * CMP_STRIDE, 128), F32)] * 4)
    return pl.pallas_call(
        functools.partial(_compress_kernel, n_x=len(x_specs) // 2, n_prefetch=n_prefetch),
        grid_spec=gs,
        out_shape=[jax.ShapeDtypeStruct((batch, nchunk, kvw), F32)] * 2,
        compiler_params=_cparams("parallel"),
        name="compress",
    )(*prefetch, *x_args, *wk, *wv, jnp.tile(w['nsa_k_norm'][0], NSA_KV).reshape(1, -1), cos128, sin128,
      _segment_ones())


def _compress_prompt(kvt, w, page=128):
    b, _, length = kvt.shape
    npages = length // page
    specs = [pl.BlockSpec((None, NSA_KV * NSA_HD, page), functools.partial(lambda i, c, j: (i, c, j), c=c, j=j))
             for c in range(2) for j in range(npages)]
    return _compress_call([kvt] * len(specs), specs, (b,), length // CMP_STRIDE, b, w)


SEL_TILE = 512


def _flash_step(carry, s, vt):
    m, l, acc = carry
    m_new = jnp.maximum(m, jnp.max(s, axis=0, keepdims=True))
    alpha = jnp.exp2(m - m_new)
    p = jnp.exp2(s - m_new)
    l = alpha * l + jnp.sum(p, axis=0, keepdims=True)
    acc = alpha * acc + _dot(vt, p.astype(BF16))
    return m_new, l, acc


def _topn_bias(score, n):
    nblk, nq = score.shape
    groups = [score[8 * v:8 * v + 8] for v in range(nblk // 8)]
    sub = lax.broadcasted_iota(jnp.int32, (8, nq), 0)
    cnt = [jnp.zeros((8, nq), F32) for _ in groups]
    for jp in range(nblk):
        row = score[jp:jp + 1]
        for v, grp in enumerate(groups):
            if v < jp // 8:
                inc = jnp.where(row > grp, 1.0, 0.0)
            elif v > jp // 8:
                inc = jnp.where(row >= grp, 1.0, 0.0)
            else:
                inc = jnp.where(sub > jp % 8, jnp.where(row >= grp, 1.0, 0.0), jnp.where(row > grp, 1.0, 0.0))
            cnt[v] = cnt[v] + inc
    return jnp.concatenate([jnp.where(c < n, 0.0, -1.0) for c in cnt], axis=0)


def _nsa_prompt_kernel(q_ref, kc_ref, vc_ref, sk_ref, sv_ref, wk_ref, wv_ref, g_ref, ov_ref, o_ref):
    qb = pl.program_id(2)
    qlen = QUERY_BLOCK
    cols = NSA_GQ * qlen
    t0 = qb * qlen
    qt = jnp.concatenate([q_ref[0, g * NSA_HD:(g + 1) * NSA_HD, :] for g in range(NSA_GQ)], axis=1)
    tcol = t0 + (lax.broadcasted_iota(jnp.int32, (1, cols), 1) & (qlen - 1))
    ncmp = kc_ref.shape[2]
    s = _dot(kc_ref[0, 0], qt)
    kend = lax.broadcasted_iota(jnp.int32, (ncmp, 1), 0) * CMP_STRIDE + (CMP_LEN - 1)
    valid = kend <= tcol
    s = jnp.where(valid, s, NEG)
    m = jnp.max(s, axis=0, keepdims=True)
    p = jnp.where(valid, jnp.exp2(s - m), 0.0)
    den = jnp.sum(p, axis=0, keepdims=True)
    p = p * (1.0 / jnp.where(den > 0, den, 1.0))
    o_c = _dot(vc_ref[0, 0], p.astype(BF16))
    psum = p[:, 0:qlen] + p[:, qlen:2 * qlen] + p[:, 2 * qlen:3 * qlen] + p[:, 3 * qlen:4 * qlen]
    nblk = ov_ref.shape[0]
    imp = _dot_hi(ov_ref[...], psum)
    jb = lax.broadcasted_iota(jnp.int32, (nblk, 1), 0)
    t = t0 + lax.broadcasted_iota(jnp.int32, (1, qlen), 1)
    cur = t >> 6
    forced = (jb == 0) | (jb == cur) | (jb == cur - 1)
    score = jnp.where(jb * SEL_BLOCK <= t, imp + jnp.where(forced, SEL_FORCE, 0.0), NEG)
    bias = _topn_bias(score, min(SEL_TOPN, nblk)).astype(BF16)
    qaug = jnp.concatenate([qt, jnp.concatenate([bias] * NSA_GQ, axis=1)], axis=0)
    init = (jnp.full((1, cols), NEG, F32), jnp.zeros((1, cols), F32), jnp.zeros((NSA_HD, cols), F32))

    def sel_scores(k0):
        return _dot(sk_ref[0, 0, pl.ds(k0, SEL_TILE), :], qaug)

    def sel_pair(kp, carry):
        m, l, acc = carry
        k0 = pl.multiple_of(kp * (2 * SEL_TILE), 2 * SEL_TILE)
        k1 = pl.multiple_of(k0 + SEL_TILE, SEL_TILE)
        s0, s1 = sel_scores(k0), sel_scores(k1)
        m_new = jnp.maximum(m, jnp.maximum(jnp.max(s0, axis=0, keepdims=True), jnp.max(s1, axis=0, keepdims=True)))
        alpha = jnp.exp2(m - m_new)
        p0, p1 = jnp.exp2(s0 - m_new), jnp.exp2(s1 - m_new)
        l = alpha * l + jnp.sum(p0, axis=0, keepdims=True) + jnp.sum(p1, axis=0, keepdims=True)
        acc = (alpha * acc + _dot(sv_ref[0, 0, :, pl.ds(k0, SEL_TILE)], p0.astype(BF16))
               + _dot(sv_ref[0, 0, :, pl.ds(k1, SEL_TILE)], p1.astype(BF16)))
        return m_new, l, acc

    def sel_single(kt, carry):
        k0 = pl.multiple_of(kt * SEL_TILE, SEL_TILE)
        return _flash_step(carry, sel_scores(k0), sv_ref[0, 0, :, pl.ds(k0, SEL_TILE)])

    n_full = t0 // SEL_TILE
    carry = lax.fori_loop(0, n_full // 2, sel_pair, init)
    carry = lax.fori_loop(n_full - n_full % 2, n_full, sel_single, carry)
    k0 = pl.multiple_of(n_full * SEL_TILE, SEL_TILE)

    def sel_diag(carry, width):
        kpos = k0 + lax.broadcasted_iota(jnp.int32, (width, 1), 0)
        sc = jnp.where(kpos <= tcol, _dot(sk_ref[0, 0, pl.ds(k0, width), :], qaug), NEG)
        return _flash_step(carry, sc, sv_ref[0, 0, :, pl.ds(k0, width)])

    widths = [qlen * (r + 1) for r in range(SEL_TILE // qlen)] if SEL_TILE % qlen == 0 else [SEL_TILE]
    if len(widths) == 1:
        carry = sel_diag(carry, widths[0])
    else:
        carry = lax.switch((t0 - k0) // qlen, [functools.partial(sel_diag, width=wd) for wd in widths], carry)
    _, l_s, acc_s = carry
    o_s = acc_s * (1.0 / l_s)

    span = min(WINDOW + qlen, wk_ref.shape[2])
    kw0 = pl.multiple_of(jnp.maximum(t0 - WINDOW, 0), qlen)
    kpos = kw0 + lax.broadcasted_iota(jnp.int32, (span, 1), 0)
    s_w = _dot(wk_ref[0, 0, pl.ds(kw0, span), :], qt)
    s_w = jnp.where(kpos <= tcol, jnp.where(kpos >= tcol - WINDOW, s_w, NEG), NEG)
    p_w = jnp.exp2(s_w - jnp.max(s_w, axis=0, keepdims=True))
    o_w = _dot(wv_ref[0, 0, :, pl.ds(kw0, span)], p_w.astype(BF16)) * (1.0 / jnp.sum(p_w, axis=0, keepdims=True))

    gate = _sigmoid(g_ref[0, 0])
    outs = []
    for g in range(NSA_GQ):
        c = slice(g * qlen, (g + 1) * qlen)
        outs.append(gate[3 * g:3 * g + 1] * o_c[:, c] + gate[3 * g + 1:3 * g + 2] * o_s[:, c]
                    + gate[3 * g + 2:3 * g + 3] * o_w[:, c])
    o_ref[...] = jnp.concatenate(outs, axis=0).T.astype(o_ref.dtype)


def _overlap_matrix(ncmp, nblk):
    ci = np.arange(ncmp)[:, None] * CMP_STRIDE
    sj = np.arange(nblk)[None, :] * SEL_BLOCK
    return jnp.asarray(((ci <= sj + SEL_BLOCK - 1) & (ci + CMP_LEN - 1 >= sj)).astype(np.float32))


def _nsa_prompt(qt, kc, vct, skaug, svt, wkr, wvt, gates_t):
    b, _, length = qt.shape
    nb = length // QUERY_BLOCK
    ncmp = kc.shape[2]
    nblk = NSA_HD
    assert length // SEL_BLOCK <= nblk
    per_kv = lambda r, c: pl.BlockSpec((1, 1, r, c), lambda bi, k, i: (bi, k, 0, 0))
    return pl.pallas_call(
        _nsa_prompt_kernel,
        grid=(b, NSA_KV, nb),
        in_specs=[pl.BlockSpec((1, NSA_GQ * NSA_HD, QUERY_BLOCK), lambda bi, k, i: (bi, k, i)),
                  per_kv(ncmp, NSA_HD), per_kv(NSA_HD, ncmp), per_kv(length, 2 * NSA_HD), per_kv(NSA_HD, length),
                  per_kv(length, NSA_HD), per_kv(NSA_HD, length),
                  pl.BlockSpec((1, 1, 3 * NSA_GQ, QUERY_BLOCK), lambda bi, k, i: (bi, k, 0, i)),
                  pl.BlockSpec((nblk, ncmp), lambda bi, k, i: (0, 0))],
        out_specs=pl.BlockSpec((QUERY_BLOCK, NSA_GQ * NSA_HD), lambda bi, k, i: (bi * nb + i, k)),
        out_shape=jax.ShapeDtypeStruct((b * length, NSA_HEADS * NSA_HD), BF16),
        compiler_params=_cparams("parallel", "parallel", "arbitrary"),
        name="nsa_prompt",
    )(qt, kc, vct, skaug, svt, wkr, wvt, gates_t, _overlap_matrix(ncmp, nblk).T)


def _mem_prompt_kernel(q_ref, mk_ref, mv_ref, g_ref, o_ref):
    for h in range(MEM_HEADS):
        sl = slice(h * MEM_HD, (h + 1) * MEM_HD)
        q = q_ref[:, sl]
        ms = jnp.mean(q * q, axis=-1, keepdims=True)
        qn = (q * lax.rsqrt(ms + EPS) * g_ref[...] * (MEM_HD ** -0.5)).astype(BF16)
        s = _dot_nt(qn, mk_ref[0, :, sl])
        m = jnp.max(s, axis=-1, keepdims=True)
        p = jnp.exp(s - m)
        p = p / jnp.sum(p, axis=-1, keepdims=True)
        o_ref[:, sl] = _dot(p.astype(BF16), mv_ref[0, :, sl]).astype(o_ref.dtype)


def _mem_prompt(q_mem, mk, mv, qnorm, b, length, tq=256):
    nl = length // tq
    width = MEM_HEADS * MEM_HD
    mem = pl.BlockSpec((1, mk.shape[1], width), lambda bi, i: (bi, 0, 0))
    return pl.pallas_call(
        _mem_prompt_kernel,
        grid=(b, nl),
        in_specs=[pl.BlockSpec((tq, width), lambda bi, i: (bi * nl + i, 0)), mem, mem,
                  pl.BlockSpec((1, MEM_HD), lambda bi, i: (0, 0))],
        out_specs=pl.BlockSpec((tq, width), lambda bi, i: (bi * nl + i, 0)),
        out_shape=jax.ShapeDtypeStruct((b * length, width), BF16),
        compiler_params=_cparams("parallel", "parallel"),
        name="mem_prompt",
    )(q_mem, mk, mv, qnorm.reshape(1, -1))


def _merge_kernel(x_ref, ys_ref, yn_ref, ym_ref, gm_ref, bm_ref, ws_ref, wn_ref, wm_ref, wo_ref, o_ref):
    gate = _sigmoid(gm_ref[...] + bm_ref[...])
    mixed = (gate[:, :D_MODEL] * _dot(ys_ref[...], ws_ref[...])
             + gate[:, D_MODEL:2 * D_MODEL] * _dot(yn_ref[...], wn_ref[...])
             + gate[:, 2 * D_MODEL:] * _dot(ym_ref[...], wm_ref[...]))
    o_ref[...] = x_ref[...] + _dot(mixed.astype(BF16), wo_ref[...])


def _merge(x, ys, yn, ym, gm, wb, tm=256):
    m = x.shape[0]
    tm = min(tm, m)
    row = lambda wd: pl.BlockSpec((tm, wd), lambda i: (i, 0))
    full = lambda shape: pl.BlockSpec(shape, lambda i: (0,) * len(shape))
    return pl.pallas_call(
        _merge_kernel,
        grid=(m // tm,),
        in_specs=[row(D_MODEL), row(D_INNER), row(D_MODEL), row(D_MODEL), row(3 * D_MODEL), full((1, 3 * D_MODEL)),
                  full((D_INNER, D_MODEL)), full((D_MODEL, D_MODEL)), full((D_MODEL, D_MODEL)),
                  full((D_MODEL, D_MODEL))],
        out_specs=row(D_MODEL),
        out_shape=jax.ShapeDtypeStruct((m, D_MODEL), F32),
        compiler_params=_cparams("parallel"),
        name="merge",
    )(x, ys, yn, ym, gm, wb['b_merge'].reshape(1, -1), wb['w_ssd_o'], wb['w_nsa_o'], wb['w_mem_o'], wb['w_out'])


FFN_COLS = 2816


def _ffn_kernel(*refs, seq):
    if seq:
        x_ref, g_ref, wu_ref, cw_ref, cb_ref, wd_ref, o_ref, ul_ref, carry_ref = refs
    else:
        x_ref, g_ref, wu_ref, cw_ref, cb_ref, wd_ref, p0_ref, p1_ref, o_ref, ul_ref = refs
    x = x_ref[...]
    tm = x.shape[0]
    ms = jnp.mean(x * x, axis=-1, keepdims=True)
    xn = (x * lax.rsqrt(ms + EPS) * g_ref[...]).astype(BF16)
    if seq:
        @pl.when(pl.program_id(1) == 0)
        def _():
            carry_ref[...] = jnp.zeros_like(carry_ref)

        row = lax.broadcasted_iota(jnp.int32, (8, 1), 0)

    def conv_cols(lo):
        sl = slice(lo, lo + FFN_COLS)
        u = _dot(xn, wu_ref[:, sl])
        if seq:
            prev = carry_ref[:, sl]
            u1 = _shift_rows(u, 1, prev, row)
            u2 = _shift_rows(u, 2, prev, row)
            carry_ref[:, sl] = u[tm - 8:, :]
        else:
            u2, u1 = p0_ref[:, sl], p1_ref[:, sl]
            ul_ref[:, sl] = u
        return cb_ref[:, sl] + cw_ref[0:1, sl] * u2 + cw_ref[1:2, sl] * u1 + cw_ref[2:3, sl] * u

    acc = x
    for j in range(D_FF // FFN_COLS):
        act = (_silu(conv_cols(j * FFN_COLS)) * conv_cols(D_FF + j * FFN_COLS)).astype(BF16)
        acc = acc + _dot(act, wd_ref[j * FFN_COLS:(j + 1) * FFN_COLS, :])
    o_ref[...] = acc
    if seq:
        @pl.when(pl.program_id(1) == pl.num_programs(1) - 1)
        def _():
            ul_ref[0] = carry_ref[...]


def _ffn_prompt(x1, w, wb, b, length, tm=256):
    nl = length // tm
    row = lambda wd: pl.BlockSpec((tm, wd), lambda bi, i: (bi * nl + i, 0))
    const = lambda shape: pl.BlockSpec(shape, lambda bi, i: (0,) * len(shape), pipeline_mode=pl.Buffered(1))
    return pl.pallas_call(
        functools.partial(_ffn_kernel, seq=True),
        grid=(b, nl),
        in_specs=[row(D_MODEL), const((1, D_MODEL)), const((D_MODEL, 2 * D_FF)), const((FFN_CONV, 2 * D_FF)),
                  const((1, 2 * D_FF)), const((D_FF, D_MODEL))],
        out_specs=[row(D_MODEL), pl.BlockSpec((1, 8, 2 * D_FF), lambda bi, i: (bi, 0, 0))],
        out_shape=[jax.ShapeDtypeStruct((b * length, D_MODEL), F32), jax.ShapeDtypeStruct((b, 8, 2 * D_FF), F32)],
        scratch_shapes=[pltpu.VMEM((8, 2 * D_FF), F32)],
        compiler_params=_cparams("parallel", "arbitrary"),
        name="ffn_prompt",
    )(x1, w['norm_ffn'].reshape(1, -1), wb['w_up'], wb['ffn_conv_w'], wb['ffn_conv_b'].reshape(1, -1), wb['w_down'])


def _ffn_sample(x1, prev, w, wb):
    m = x1.shape[0]
    full = lambda shape: pl.BlockSpec(shape, lambda i: (0,) * len(shape))
    return pl.pallas_call(
        functools.partial(_ffn_kernel, seq=False),
        grid=(1,),
        in_specs=[full((m, D_MODEL)), full((1, D_MODEL)), full((D_MODEL, 2 * D_FF)), full((FFN_CONV, 2 * D_FF)),
                  full((1, 2 * D_FF)), full((D_FF, D_MODEL)), full((m, 2 * D_FF)), full((m, 2 * D_FF))],
        out_specs=[full((m, D_MODEL)), full((m, 2 * D_FF))],
        out_shape=[jax.ShapeDtypeStruct((m, D_MODEL), F32), jax.ShapeDtypeStruct((m, 2 * D_FF), F32)],
        compiler_params=_cparams("arbitrary"),
        name="ffn_sample",
    )(x1, w['norm_ffn'].reshape(1, -1), wb['w_up'], wb['ffn_conv_w'], wb['ffn_conv_b'].reshape(1, -1), wb['w_down'],
      prev[:, 0], prev[:, 1])


def _compress_sample(pool_k, pool_v, page_table, w):
    bs, npages = page_table.shape
    page = pool_k.shape[1]
    specs = [pl.BlockSpec((None, NSA_KV * NSA_HD, page), functools.partial(lambda b, pt, j: (pt[b, j], 0, 0), j=j))
             for j in range(npages)]
    return _compress_call([_pool_t(pool_k)] * npages + [_pool_t(pool_v)] * npages, specs + specs, (bs,),
                          npages * page // CMP_STRIDE, bs, w, n_prefetch=1, prefetch=(page_table,))


def _pool_t(pool):
    n, page = pool.shape[:2]
    return pool.transpose(0, 2, 3, 1).reshape(n, NSA_KV * NSA_HD, page)


def _rows8(x):
    return jnp.broadcast_to(x, (8, x.shape[1]))


def _ssd_step_kernel(xbc_ref, prev_ref, z_ref, dt_ref, h_ref, cw_ref, cb_ref, dtb_ref, a_ref, dskip_ref, nw_ref,
                     e64_ref, e128_ref, y_ref, hn_ref):
    conv = cb_ref[...] + cw_ref[3:4, :] * xbc_ref[0]
    for k in range(SSD_CONV - 1):
        conv = conv + cw_ref[k:k + 1, :] * prev_ref[0, k:k + 1, :]
    xc = _silu(conv)
    xs = xc[:, :D_INNER]
    nb = SSD_GROUPS * SSD_STATE
    bm = xc[:, D_INNER:D_INNER + nb]
    cm = xc[:, D_INNER + nb:]
    v = dt_ref[0] + dtb_ref[...]
    dt = jnp.maximum(v, 0.0) + jnp.log(1.0 + jnp.exp(-jnp.abs(v)))
    dec = jnp.exp(dt * a_ref[...])
    xdt = xs * _dot_x3(_rows8(dt), e64_ref[...], 0)[0:1]
    dec128 = _dot_x3(_rows8(dec), e128_ref[...], 0)
    pieces = []
    for c in range(D_INNER // 128):
        g = c // (SSD_HPG // 2)
        xcol = jnp.broadcast_to(xdt[:, c * 128:(c + 1) * 128], (128, 128)).T
        bg = jnp.broadcast_to(bm[:, g * SSD_STATE:(g + 1) * SSD_STATE], (128, SSD_STATE))
        decv = jnp.concatenate(
            [jnp.concatenate([dec128[:, hh * 128:(hh + 1) * 128]] * (SSD_HEAD_DIM // 8), axis=0)
             for hh in (2 * c, 2 * c + 1)], axis=0)
        hnew = decv * h_ref[0, c * 128:(c + 1) * 128, :] + xcol * bg
        hn_ref[0, c * 128:(c + 1) * 128, :] = hnew
        cg = _rows8(cm[:, g * SSD_STATE:(g + 1) * SSD_STATE]).astype(BF16)
        pieces.append(_dot_nt(cg, hnew.astype(BF16))[0:1])
    y = jnp.concatenate(pieces, axis=1) + dskip_ref[...] * xs
    yz = y * _silu(z_ref[0])
    ms = jnp.mean(yz * yz, axis=-1, keepdims=True)
    y_ref[0] = (yz * lax.rsqrt(ms + EPS) * nw_ref[...]).astype(y_ref.dtype)


def _ssd_step(xbc, prev, z, dt_raw, h0, w):
    bs = xbc.shape[0]
    a = -jnp.exp(w['ssd_a_log'].astype(F32))
    row = lambda wd: pl.BlockSpec((1, 1, wd), lambda i: (i, 0, 0))
    full = lambda shape: pl.BlockSpec(shape, lambda i: (0,) * len(shape))
    st = pl.BlockSpec((1, D_INNER, SSD_STATE), lambda i: (i, 0, 0))
    y, hn = pl.pallas_call(
        _ssd_step_kernel,
        grid=(bs,),
        in_specs=[row(SSD_CONV_DIM), pl.BlockSpec((1, SSD_CONV - 1, SSD_CONV_DIM), lambda i: (i, 0, 0)),
                  row(D_INNER), row(SSD_HEADS), st,
                  full((SSD_CONV, SSD_CONV_DIM)), full((1, SSD_CONV_DIM)), full((1, SSD_HEADS)), full((1, SSD_HEADS)),
                  full((1, D_INNER)), full((1, D_INNER)), full((SSD_HEADS, D_INNER)),
                  full((SSD_HEADS, SSD_HEADS * 128))],
        out_specs=[row(D_INNER), st],
        out_shape=[jax.ShapeDtypeStruct((bs, 1, D_INNER), BF16),
                   jax.ShapeDtypeStruct((bs, D_INNER, SSD_STATE), F32)],
        compiler_params=_cparams("parallel"),
        name="ssd_step",
    )(xbc.reshape(bs, 1, -1), prev, z.reshape(bs, 1, -1), dt_raw.reshape(bs, 1, -1),
      h0.reshape(bs, D_INNER, SSD_STATE), w['ssd_conv_w'], w['ssd_conv_b'].reshape(1, -1),
      w['ssd_dt_bias'].reshape(1, -1), a.reshape(1, -1),
      jnp.repeat(w['ssd_d'].astype(F32), SSD_HEAD_DIM).reshape(1, -1), w['ssd_norm'].reshape(1, -1),
      _head_expand(), _head_expand(128))
    return y.reshape(bs, D_INNER), hn.reshape(bs, SSD_HEADS, SSD_HEAD_DIM, SSD_STATE)


def _softmax_with_extra(s, s_new, valid=None):
    if valid is not None:
        s = jnp.where(valid, s, NEG)
    m = jnp.maximum(jnp.max(s, axis=-1, keepdims=True), s_new)
    p = jnp.exp(s - m)
    if valid is not None:
        p = jnp.where(valid, p, 0.0)
    p_new = jnp.exp(s_new - m)
    inv = 1.0 / (jnp.sum(p, axis=-1, keepdims=True) + p_new)
    return p * inv, p_new * inv


def _nsa_sample_kernel(*refs, npages, past_len, spb):
    q_ref, kv_ref, gate_ref = refs[1:4]
    skp = refs[4:4 + spb * npages]
    svp = refs[4 + spb * npages:4 + 2 * spb * npages]
    (kc_ref, vc_ref, wk_ref, wv_ref, qg_ref, kg_ref, cos_ref, sin_ref, segm_ref, ov_ref, eblk_ref, fold_ref,
     foldt_ref, o_ref, skn_ref, wkn_ref, wko_ref, wvo_ref) = refs[4 + 2 * spb * npages:]
    kvw = NSA_KV * NSA_HD
    segm = segm_ref[...]
    cos, sin = _rows8(cos_ref[...]), _rows8(sin_ref[...])
    hrow = lax.broadcasted_iota(jnp.int32, (NSA_HEADS, NSA_HEADS * NSA_HD), 0)
    hcol = lax.broadcasted_iota(jnp.int32, (NSA_HEADS, NSA_HEADS * NSA_HD), 1)
    own = (hcol >> 6) == hrow
    ncmp = kc_ref.shape[1]
    kend = lax.broadcasted_iota(jnp.int32, (1, ncmp), 1) * CMP_STRIDE + (CMP_LEN - 1)
    valid_c = kend <= past_len
    gi = lax.broadcasted_iota(jnp.int32, (NSA_HEADS, NSA_HEADS), 0) // NSA_GQ
    gj = lax.broadcasted_iota(jnp.int32, (NSA_HEADS, NSA_HEADS), 1) // NSA_GQ
    same_group = jnp.where(gi == gj, 1.0, 0.0)

    first = []
    for s in range(spb):
        q = _rope(_head_rmsnorm(_rows8(q_ref[s]), qg_ref[...], segm), cos, sin) * (NSA_HD ** -0.5)
        kv = kv_ref[s]
        sk_new = _rope(_head_rmsnorm(_rows8(kv[:, 2 * kvw:3 * kvw]), kg_ref[1:2, :], segm), cos, sin)[0:1]
        wk_new = _rope(_head_rmsnorm(_rows8(kv[:, 4 * kvw:5 * kvw]), kg_ref[2:3, :], segm), cos, sin)[0:1]
        skn_ref[s] = sk_new
        wkn_ref[s] = wk_new
        q16 = jnp.where(own, jnp.concatenate([q, q], axis=0), 0.0).astype(BF16)
        qbd = _dot(q16, fold_ref[...])
        qbd_b = qbd.astype(BF16)
        s_c = jnp.where(valid_c, _dot_nt(qbd_b, kc_ref[s].astype(BF16)), NEG)
        m_c = jnp.max(s_c, axis=-1, keepdims=True)
        p_c = jnp.where(valid_c, jnp.exp(s_c - m_c), 0.0)
        den = jnp.sum(p_c, axis=-1, keepdims=True)
        p_c = p_c / jnp.where(den > 0, den, 1.0)
        o_c = _dot(p_c.astype(BF16), vc_ref[s].astype(BF16))
        first.append((kv, sk_new, wk_new, qbd, qbd_b, o_c, _dot_hi(same_group, p_c)))

    psum = jnp.concatenate([f[6] for f in first] + [jnp.zeros((128 - spb * NSA_HEADS, ncmp), F32)], axis=0)
    imp = lax.dot_general(ov_ref[...], psum, (((1,), (1,)), ((), ())), preferred_element_type=F32,
                          precision=HI)
    nslot = ov_ref.shape[0]
    jb = lax.broadcasted_iota(jnp.int32, (nslot, 1), 0)
    cur = past_len // SEL_BLOCK
    forced = (jb == 0) | (jb == cur) | (jb == cur - 1)
    score = jnp.where(jb * SEL_BLOCK <= past_len, imp + jnp.where(forced, SEL_FORCE, 0.0), NEG)
    bias_all = _topn_bias(score, SEL_TOPN).T

    nwin = wk_ref.shape[2]
    lane = lax.broadcasted_iota(jnp.int32, (kvw, nwin), 1)

    def column(rowvec):
        col = jnp.broadcast_to(rowvec, (128, kvw)).T
        return jnp.concatenate([col] * (nwin // 128), axis=1)

    for s, (kv, sk_new, wk_new, qbd, qbd_b, o_c, _) in enumerate(first):
        sv_new = kv[:, 3 * kvw:4 * kvw]
        wv_new = kv[:, 5 * kvw:6 * kvw]
        bias = bias_all[s * NSA_HEADS:(s + 1) * NSA_HEADS]
        kmask = _dot(bias.astype(BF16), eblk_ref[...])
        pages = slice(s * npages, (s + 1) * npages)
        sk_all = jnp.concatenate([r[...] for r in skp[pages]], axis=1).astype(BF16)
        sv_all = jnp.concatenate([r[...] for r in svp[pages]], axis=1).astype(BF16)
        s_s = _dot(qbd_b, sk_all) + kmask
        s_new = jnp.sum(qbd * sk_new, axis=-1, keepdims=True)
        p_s, p_new = _softmax_with_extra(s_s, s_new)
        o_s = _dot_nt(p_s.astype(BF16), sv_all) + p_new * sv_new
        wk = wk_ref[s]
        wv = wv_ref[s]
        s_w = _dot(qbd_b, wk.astype(BF16))
        s_wn = jnp.sum(qbd * wk_new, axis=-1, keepdims=True)
        p_w, p_wn = _softmax_with_extra(s_w, s_wn)
        o_w = _dot_nt(p_w.astype(BF16), wv.astype(BF16)) + p_wn * wv_new
        g16 = jnp.where((lax.broadcasted_iota(jnp.int32, (NSA_HEADS, 3 * NSA_HEADS), 1) // 3)
                        == lax.broadcasted_iota(jnp.int32, (NSA_HEADS, 3 * NSA_HEADS), 0),
                        jnp.broadcast_to(_sigmoid(gate_ref[s]), (NSA_HEADS, 3 * NSA_HEADS)), 0.0)
        br = lax.broadcasted_iota(jnp.int32, (NSA_HEADS, 3 * NSA_HEADS), 1) % 3
        gsel = lambda r: jnp.sum(jnp.where(br == r, g16, 0.0), axis=-1, keepdims=True)
        o16 = gsel(0) * o_c + gsel(1) * o_s + gsel(2) * o_w
        ox = _dot(o16.astype(BF16), foldt_ref[...])
        o_ref[s] = jnp.sum(jnp.where(own, ox, 0.0), axis=0, keepdims=True).astype(o_ref.dtype)
        wko_ref[s] = jnp.where(lane == nwin - 1, column(wk_new), pltpu.roll(wk, nwin - 1, axis=1))
        wvo_ref[s] = jnp.where(lane == nwin - 1, column(wv_new), pltpu.roll(wv, nwin - 1, axis=1))


def _nsa_sample(q, kv, gates, pool_sk, pool_sv, page_table, kc, vc, win_k, win_v, w):
    bs, npages = page_table.shape
    page = pool_sk.shape[1]
    past_len = npages * page
    kvw = NSA_KV * NSA_HD
    nwin = win_k.shape[1]
    assert nwin == WINDOW and past_len % SEL_BLOCK == 0
    ncmp = kc.shape[1]
    nslot = 64
    assert past_len // SEL_BLOCK + 1 <= nslot
    cos128, sin128 = _rope_tables(jnp.full((1,), past_len, jnp.int32))
    overlap = _overlap_matrix(ncmp, nslot).T
    key_blk = np.arange(past_len) // SEL_BLOCK
    eblk = jnp.asarray((np.arange(nslot)[:, None] == key_blk[None, :]).astype(np.float32) * MASK_BIG, BF16)
    src = np.arange(NSA_HEADS * NSA_HD)
    dst = (src // NSA_HD // NSA_GQ) * NSA_HD + src % NSA_HD
    fold_np = np.zeros((NSA_HEADS * NSA_HD, kvw), np.float32)
    fold_np[src, dst] = 1.0
    fold = jnp.asarray(fold_np, BF16)
    foldt = jnp.asarray(fold_np.T, BF16)
    spb = 2 if bs % 2 == 0 else 1
    r3 = lambda t: t.reshape(bs, 1, -1)
    row = lambda wd: pl.BlockSpec((spb, 1, wd), lambda b, pt: (b, 0, 0))
    full = lambda shape: pl.BlockSpec(shape, lambda b, pt: (0,) * len(shape))
    per_b = lambda r, c: pl.BlockSpec((spb, r, c), lambda b, pt: (b, 0, 0))
    pages = [pl.BlockSpec((None, kvw, page), functools.partial(lambda b, pt, s, j: (pt[b * spb + s, j], 0, 0), s=s, j=j))
             for s in range(spb) for j in range(npages)]
    npages_all = spb * npages
    pk3 = _pool_t(pool_sk)
    pv3 = _pool_t(pool_sv)
    gs = pltpu.PrefetchScalarGridSpec(
        num_scalar_prefetch=1, grid=(bs // spb,),
        in_specs=[row(NSA_HEADS * NSA_HD), row(6 * kvw), row(3 * NSA_HEADS)] + pages + pages
        + [per_b(ncmp, kvw), per_b(ncmp, kvw), per_b(kvw, nwin), per_b(kvw, nwin),
           full((1, NSA_HEADS * NSA_HD)), full((3, kvw)), full((1, 128)), full((1, 128)), full((128, 128)),
           full((nslot, ncmp)), full((nslot, past_len)), full(fold.shape), full(foldt.shape)],
        out_specs=[row(NSA_HEADS * NSA_HD), row(kvw), row(kvw), per_b(kvw, nwin), per_b(kvw, nwin)])
    return pl.pallas_call(
        functools.partial(_nsa_sample_kernel, npages=npages, past_len=past_len, spb=spb),
        grid_spec=gs,
        out_shape=[jax.ShapeDtypeStruct((bs, 1, NSA_HEADS * NSA_HD), BF16),
                   jax.ShapeDtypeStruct((bs, 1, kvw), F32), jax.ShapeDtypeStruct((bs, 1, kvw), F32),
                   jax.ShapeDtypeStruct((bs, kvw, nwin), F32), jax.ShapeDtypeStruct((bs, kvw, nwin), F32)],
        compiler_params=_cparams("parallel"),
        name="nsa_sample",
    )(page_table, r3(q), r3(kv), r3(gates), *([pk3] * npages_all), *([pv3] * npages_all), kc, vc,
      _pool_t(win_k), _pool_t(win_v),
      jnp.tile(w['nsa_q_norm'], NSA_HEADS).reshape(1, -1), jnp.tile(w['nsa_k_norm'], (1, NSA_KV)),
      cos128, sin128, _segment_ones(), overlap, eblk, fold, foldt)


def _mem_sample_kernel(q_ref, mk_ref, mv_ref, g_ref, o_ref):
    spb, mtok = mk_ref.shape[0], mk_ref.shape[1]
    nrow = mtok * MEM_HEADS
    rows = []
    for s in range(spb):
        q = q_ref[s]
        for h in range(MEM_HEADS):
            qh = q[:, h * MEM_HD:(h + 1) * MEM_HD]
            ms = jnp.mean(qh * qh, axis=-1, keepdims=True)
            rows.append(qh * lax.rsqrt(ms + EPS) * g_ref[...] * (MEM_HD ** -0.5))
    q8 = jnp.concatenate(rows, axis=0).astype(BF16)
    mk = jnp.concatenate([mk_ref[s].reshape(nrow, MEM_HD) for s in range(spb)], axis=0).astype(BF16)
    mv = jnp.concatenate([mv_ref[s].reshape(nrow, MEM_HD) for s in range(spb)], axis=0).astype(BF16)
    col = lax.broadcasted_iota(jnp.int32, (spb * MEM_HEADS, spb * nrow), 1)
    own = ((col // nrow) * MEM_HEADS + (col & (MEM_HEADS - 1))
           == lax.broadcasted_iota(jnp.int32, (spb * MEM_HEADS, spb * nrow), 0))
    s = jnp.where(own, _dot_nt(q8, mk), NEG)
    m = jnp.max(s, axis=-1, keepdims=True)
    p = jnp.exp(s - m)
    p = p / jnp.sum(p, axis=-1, keepdims=True)
    o8 = _dot(p.astype(BF16), mv)
    for s in range(spb):
        for h in range(MEM_HEADS):
            r = s * MEM_HEADS + h
            o_ref[s, :, h * MEM_HD:(h + 1) * MEM_HD] = o8[r:r + 1].astype(o_ref.dtype)


def _mem_sample(q_mem, mem_k, mem_v, qnorm):
    bs = q_mem.shape[0]
    width = MEM_HEADS * MEM_HD
    mtok = mem_k.shape[1]
    spb = 2
    assert bs % spb == 0
    row = pl.BlockSpec((spb, 1, width), lambda i: (i, 0, 0))
    mem = pl.BlockSpec((spb, mtok, MEM_HEADS, MEM_HD), lambda i: (i, 0, 0, 0))
    return pl.pallas_call(
        _mem_sample_kernel,
        grid=(bs // spb,),
        in_specs=[row, mem, mem, pl.BlockSpec((1, MEM_HD), lambda i: (0, 0))],
        out_specs=row,
        out_shape=jax.ShapeDtypeStruct((bs, 1, width), BF16),
        compiler_params=_cparams("parallel"),
        name="mem_sample",
    )(q_mem.reshape(bs, 1, width), mem_k, mem_v, qnorm.reshape(1, -1)).reshape(bs, width)


def _split_w_in(w_in):
    offs = np.cumsum((0,) + IN_SIZES)
    piece = lambda i: w_in[:, int(offs[i]):int(offs[i + 1])]
    small = jnp.concatenate([piece(2), piece(5)], axis=1)
    small = jnp.pad(small, ((0, 0), (0, 128 - small.shape[1])))
    return dict(z=piece(0), xbc=piece(1), small=small, q=piece(3), kv=piece(4), qmem=piece(6), gmerge=piece(7))


def _multi_mm_kernel(*refs, transposed):
    n_out = len(transposed)
    x_ref, g_ref = refs[:2]
    w_refs = refs[2:2 + n_out]
    o_refs = refs[2 + n_out:]
    x = x_ref[...]
    ms = jnp.mean(x * x, axis=-1, keepdims=True)
    xn = (x * lax.rsqrt(ms + EPS) * g_ref[...]).astype(BF16)
    for w_ref, o_ref, tr in zip(w_refs, o_refs, transposed):
        if tr:
            o_ref[0] = _dot_nt(w_ref[...], xn).astype(o_ref.dtype)
        else:
            o_ref[...] = _dot(xn, w_ref[...]).astype(o_ref.dtype)


def _multi_matmul(x, g, weights, transposed, seq_len, tm=256):
    m, k = x.shape
    tm = min(tm, m)
    assert m % tm == 0 and (not any(transposed) or seq_len % tm == 0)
    nl = seq_len // tm if any(transposed) else 1
    const = lambda shape: pl.BlockSpec(shape, lambda i: (0, 0), pipeline_mode=pl.Buffered(1))
    out_specs, out_shape = [], []
    for wm, tr in zip(weights, transposed):
        if tr:
            out_specs.append(pl.BlockSpec((1, wm.shape[0], tm), lambda i: (i // nl, 0, i % nl)))
            out_shape.append(jax.ShapeDtypeStruct((m // seq_len, wm.shape[0], seq_len), F32))
        else:
            out_specs.append(pl.BlockSpec((tm, wm.shape[1]), lambda i: (i, 0)))
            out_shape.append(jax.ShapeDtypeStruct((m, wm.shape[1]), F32))
    return pl.pallas_call(
        functools.partial(_multi_mm_kernel, transposed=tuple(transposed)),
        grid=(m // tm,),
        in_specs=[pl.BlockSpec((tm, k), lambda i: (i, 0)), const((1, k))] + [const(wm.shape) for wm in weights],
        out_specs=out_specs,
        out_shape=out_shape,
        compiler_params=_cparams("parallel"),
        name="in_proj",
    )(x, g.reshape(1, k).astype(F32), *weights)


def _in_proj(x, norm_w, wi, feature_major=(), seq_len=1):
    names = list(wi)
    weights = [wi[n].T if n in feature_major else wi[n] for n in names]
    proj = dict(zip(names, _multi_matmul(x, norm_w, weights, [n in feature_major for n in names], seq_len)))
    proj['dt'] = proj['small'][:, :SSD_HEADS]
    proj['gnsa'] = proj['small'][:, SSD_HEADS:SSD_HEADS + 3 * NSA_HEADS]
    return proj


def _prompt_layer(x_prompt, mem_prompt, w, wb, wi):
    b, length, _ = x_prompt.shape
    n = b * length
    kvw = NSA_KV * NSA_HD
    x = x_prompt.reshape(n, D_MODEL)
    proj = _in_proj(x, w['norm_mix'], wi, feature_major=('q', 'kv'), seq_len=length)
    xbc = proj['xbc'].reshape(b, length, SSD_CONV_DIM)
    y_ssd, ssd_state = _ssd_prompt(xbc, proj['z'].reshape(b, length, D_INNER),
                                   proj['dt'].reshape(b, length, SSD_HEADS), w)
    kvt = proj['kv']
    qt, skaug, skf, svt, wkr, wkf, wvt = _nsa_prep(proj['q'], kvt, w)
    kc, vc = _compress_prompt(kvt, w)
    heads = lambda t: t.reshape(b, -1, NSA_KV, NSA_HD).astype(BF16)
    t4 = lambda t: t.reshape(b, NSA_KV, NSA_HD, -1)
    gates_t = proj['gnsa'].reshape(b, length, NSA_KV, 3 * NSA_GQ).transpose(0, 2, 3, 1)
    o_nsa = _nsa_prompt(qt, heads(kc).transpose(0, 2, 1, 3), heads(vc).transpose(0, 2, 3, 1), skaug, t4(svt), wkr,
                        t4(wvt), gates_t)
    mem = mem_prompt.reshape(-1, D_MODEL)
    mkv, = _multi_matmul(mem, w['mem_norm'], [wb['w_mem_kv']], [False], 1)
    mtok = mem_prompt.shape[1]
    mk = mkv[:, :MEM_HEADS * MEM_HD].reshape(b, mtok, MEM_HEADS, MEM_HD)
    mk = mk * lax.rsqrt(jnp.mean(mk * mk, axis=-1, keepdims=True) + EPS) * w['mem_k_norm']
    mv = mkv[:, MEM_HEADS * MEM_HD:].reshape(b, mtok, MEM_HEADS, MEM_HD)
    o_mem = _mem_prompt(proj['qmem'], mk.reshape(b, mtok, -1).astype(BF16), mv.reshape(b, mtok, -1).astype(BF16),
                        w['mem_q_norm'], b, length)
    x1 = _merge(x, y_ssd.reshape(n, D_INNER), o_nsa, o_mem, proj['gmerge'], wb)
    y, u_last = _ffn_prompt(x1, w, wb, b, length)
    from_t = lambda t: t4(t).transpose(0, 3, 1, 2)
    keep = min(WINDOW, length)
    state = (from_t(kvt[:, 0:kvw]), from_t(kvt[:, kvw:2 * kvw]), from_t(skf), from_t(kvt[:, 3 * kvw:4 * kvw]),
             from_t(wkf[:, :, -keep:]), from_t(kvt[:, 5 * kvw:6 * kvw, -keep:]),
             xbc[:, -(SSD_CONV - 1):], ssd_state, u_last[:, -(FFN_CONV - 1):], mk, mv)
    return y.reshape(b, length, D_MODEL), state


def _sample_layer(x_sample, caches, page_table, w, wb, wi):
    (pool_ck, pool_cv, pool_sk, pool_sv, win_k, win_v, conv_prev, ssd_h0, mem_k, mem_v, ffn_prev) = caches
    bs = x_sample.shape[0]
    kvw = NSA_KV * NSA_HD
    x = x_sample.reshape(bs, D_MODEL)
    proj = _in_proj(x, w['norm_mix'], wi)
    y_ssd, ssd_state = _ssd_step(proj['xbc'], conv_prev, proj['z'], proj['dt'], ssd_h0, w)
    kv = proj['kv']
    kc, vc = _compress_sample(pool_ck, pool_cv, page_table, w)
    o_nsa, sk_new, wk_new, win_k_new, win_v_new = _nsa_sample(
        proj['q'], kv, proj['gnsa'], pool_sk, pool_sv, page_table, kc, vc, win_k, win_v, w)
    o_mem = _mem_sample(proj['qmem'], mem_k, mem_v, w['mem_q_norm'])
    x1 = _merge(x, y_ssd, o_nsa.reshape(bs, -1), o_mem, proj['gmerge'], wb)
    y, u = _ffn_sample(x1, ffn_prev, w, wb)
    r4 = lambda t: t.reshape(bs, -1, NSA_KV, NSA_HD)
    from_t = lambda t: t.reshape(bs, NSA_KV, NSA_HD, -1).transpose(0, 3, 1, 2)
    state = (r4(kv[:, 0:kvw]), r4(kv[:, kvw:2 * kvw]), r4(sk_new), r4(kv[:, 3 * kvw:4 * kvw]),
             from_t(win_k_new), from_t(win_v_new),
             jnp.concatenate([conv_prev[:, 1:], proj['xbc'][:, None]], axis=1), ssd_state,
             jnp.concatenate([ffn_prev[:, 1:], u[:, None]], axis=1))
    return y.reshape(bs, 1, D_MODEL), state


def kernel(x_prompt, x_sample, cache_nsa_cmp_k, cache_nsa_cmp_v, cache_nsa_sel_k, cache_nsa_sel_v, state_nsa_win_k, state_nsa_win_v, state_ssd_conv, state_ssd, cache_mem_k, cache_mem_v, state_ffn_conv, page_table, mem_prompt, norm_mix, w_in, b_merge, ssd_conv_w, ssd_conv_b, ssd_dt_bias, ssd_a_log, ssd_d, ssd_norm, w_ssd_o, nsa_q_norm, nsa_k_norm, cmp_k_pe, cmp_k_w1, cmp_k_w2, cmp_v_pe, cmp_v_w1, cmp_v_w2, w_nsa_o, mem_norm, w_mem_kv, mem_q_norm, mem_k_norm, w_mem_o, w_out, norm_ffn, w_up, ffn_conv_w, ffn_conv_b, w_down):
    weights = dict(norm_mix=norm_mix, w_in=w_in, b_merge=b_merge, ssd_conv_w=ssd_conv_w,
                   ssd_conv_b=ssd_conv_b, ssd_dt_bias=ssd_dt_bias, ssd_a_log=ssd_a_log, ssd_d=ssd_d,
                   ssd_norm=ssd_norm, w_ssd_o=w_ssd_o, nsa_q_norm=nsa_q_norm, nsa_k_norm=nsa_k_norm,
                   cmp_k_pe=cmp_k_pe, cmp_k_w1=cmp_k_w1, cmp_k_w2=cmp_k_w2, cmp_v_pe=cmp_v_pe,
                   cmp_v_w1=cmp_v_w1, cmp_v_w2=cmp_v_w2, w_nsa_o=w_nsa_o, mem_norm=mem_norm,
                   w_mem_kv=w_mem_kv, mem_q_norm=mem_q_norm, mem_k_norm=mem_k_norm, w_mem_o=w_mem_o,
                   w_out=w_out, norm_ffn=norm_ffn, w_up=w_up, ffn_conv_w=ffn_conv_w,
                   ffn_conv_b=ffn_conv_b, w_down=w_down)
    w = {name: arr[0] for name, arr in weights.items()}
    wb = dict(w)
    for name in ('w_ssd_o', 'w_nsa_o', 'w_mem_o', 'w_out', 'w_up', 'w_down', 'w_mem_kv'):
        wb[name] = w[name].astype(BF16)
    wi = {name: piece.astype(BF16) for name, piece in _split_w_in(w['w_in']).items()}
    y_p, st_p = _prompt_layer(x_prompt, mem_prompt, w, wb, wi)
    caches = (cache_nsa_cmp_k[0], cache_nsa_cmp_v[0], cache_nsa_sel_k[0], cache_nsa_sel_v[0], state_nsa_win_k[0],
              state_nsa_win_v[0], state_ssd_conv[0], state_ssd[0], cache_mem_k[0], cache_mem_v[0], state_ffn_conv[0])
    y_s, st_s = _sample_layer(x_sample, caches, page_table, w, wb, wi)
    return (y_p, y_s) + tuple(s[None] for s in st_p) + tuple(s[None] for s in st_s)
```

```python
import functools
import math

import numpy as np
import jax
import jax.numpy as jnp
from jax import lax
from jax.experimental import pallas as pl
from jax.experimental.pallas import tpu as pltpu

D_MODEL = 1024
D_INNER = 2048
SSD_HEAD_DIM = 64
SSD_HEADS = 32
SSD_GROUPS = 4
SSD_HPG = 8
SSD_STATE = 128
SSD_CONV = 4
SSD_CONV_DIM = 3072
SSD_CHUNK = 128
NSA_HEADS = 16
NSA_KV = 4
NSA_GQ = 4
NSA_HD = 64
CMP_LEN = 32
CMP_STRIDE = 16
CMP_HIDDEN = 128
SEL_BLOCK = 64
SEL_TOPN = 16
SEL_FORCE = 1.0e4
WINDOW = 512
QUERY_BLOCK = 512
MEM_HEADS = 4
MEM_HD = 256
D_FF = 2816
FFN_CONV = 3
ROPE_THETA = 10000.0
EPS = 1e-6
IN_SIZES = (2048, 3072, 32, 1024, 1536, 48, 1024, 3072)

VMEM_LIMIT = 48 * 1024 * 1024
MASK_BIG = 2.0 ** 20
NEG = -1.0e30

F32 = jnp.float32
BF16 = jnp.bfloat16
HI = lax.Precision.HIGHEST


def _cparams(*sem):
    return pltpu.CompilerParams(dimension_semantics=sem, vmem_limit_bytes=VMEM_LIMIT)


def _sigmoid(x):
    return 1.0 / (1.0 + jnp.exp(-x))


def _silu(x):
    return x * _sigmoid(x)


def _dot(a, b):
    return jnp.dot(a, b, preferred_element_type=F32)


def _dot_nt(a, b):
    return lax.dot_general(a, b, (((1,), (1,)), ((), ())), preferred_element_type=F32)


def _dot_hi(a, b):
    return jnp.dot(a, b, preferred_element_type=F32, precision=HI)


def _split3(x):
    hi = x.astype(BF16)
    r1 = x - hi.astype(F32)
    mid = r1.astype(BF16)
    lo = (r1 - mid.astype(F32)).astype(BF16)
    return hi, mid, lo


def _dot_x3(a, b, split):
    if split == 0:
        parts = [_dot(t, b) for t in _split3(a)]
    else:
        parts = [_dot(a, t) for t in _split3(b)]
    return parts[0] + parts[1] + parts[2]


def _shift_rows(x, k, prev, row):
    r = pltpu.roll(x, k, axis=0)
    head = r[:8]
    for i in range(k):
        head = jnp.where(row == i, prev[8 - k + i:8 - k + i + 1, :], head)
    return jnp.concatenate([head, r[8:]], axis=0)


def _ssd_prompt_kernel(xbc_ref, z_ref, dt_ref, dtT_ref, cw_ref, cb_ref, dtb_ref, dtbT_ref, a_ref, aT_ref,
                       dskip_ref, nw_ref, e_ref, y_ref, hT_ref, carry_ref, h_ref, yacc_ref):
    c = pl.program_id(1)
    q = SSD_CHUNK

    @pl.when(c == 0)
    def _():
        carry_ref[...] = jnp.zeros_like(carry_ref)
        h_ref[...] = jnp.zeros_like(h_ref)

    xbc = xbc_ref[0]
    row = lax.broadcasted_iota(jnp.int32, (8, 1), 0)
    prev = carry_ref[...]
    conv = cb_ref[...] + cw_ref[3:4, :] * xbc
    for k in range(1, SSD_CONV):
        conv = conv + cw_ref[3 - k:4 - k, :] * _shift_rows(xbc, k, prev, row)
    carry_ref[...] = xbc[q - 8:, :]
    xc = _silu(conv)
    xs = xc[:, :D_INNER]
    bm = xc[:, D_INNER:D_INNER + SSD_GROUPS * SSD_STATE].astype(BF16)
    cm = xc[:, D_INNER + SSD_GROUPS * SSD_STATE:].astype(BF16)

    def softplus(v):
        return jnp.maximum(v, 0.0) + jnp.log(1.0 + jnp.exp(-jnp.abs(v)))

    dt = softplus(dt_ref[0] + dtb_ref[...])
    dtT = softplus(dtT_ref[0] + dtbT_ref[...])
    ii = lax.broadcasted_iota(jnp.int32, (q, q), 0)
    jj = lax.broadcasted_iota(jnp.int32, (q, q), 1)
    causal = ii >= jj
    cum = _dot_x3(jnp.where(causal, 1.0, 0.0).astype(BF16), dt * a_ref[...], 1)
    cumT = _dot_x3(dtT * aT_ref[...], jnp.where(jj >= ii, 1.0, 0.0).astype(BF16), 0)
    ecum = jnp.exp(cum)
    dend = jnp.exp(cum[q - 1:q, :] - cum)
    stacked = jnp.concatenate([dt, ecum, dend], axis=0)
    hi = stacked.astype(BF16)
    lo = (stacked - hi.astype(F32)).astype(BF16)
    spread = _dot(hi, e_ref[...]) + _dot(lo, e_ref[...])
    dt_x, ecum_x, dend_x = spread[:q], spread[q:2 * q], spread[2 * q:]
    xdt = xs * dt_x
    xdt_b = xdt.astype(BF16)
    xw_b = (xdt * dend_x).astype(BF16)
    gw = SSD_HPG * SSD_HEAD_DIM
    for g in range(SSD_GROUPS):
        bg = bm[:, g * SSD_STATE:(g + 1) * SSD_STATE]
        cg = cm[:, g * SSD_STATE:(g + 1) * SSD_STATE]
        cb = _dot_nt(cg, bg)
        h_prev = h_ref[g]
        yoff = _dot(cg, h_prev.astype(BF16)) * ecum_x[:, g * gw:(g + 1) * gw]
        st = _dot(bg.astype(F32).T.astype(BF16), xw_b[:, g * gw:(g + 1) * gw])
        h_ref[g] = h_prev * ecum_x[q - 1:q, g * gw:(g + 1) * gw] + st
        for eh in range(SSD_HPG):
            hh = g * SSD_HPG + eh
            seg = cum[:, hh:hh + 1] - cumT[hh:hh + 1, :]
            decay = jnp.exp(jnp.where(causal, seg, NEG))
            mm = (cb * decay).astype(BF16)
            lo = hh * SSD_HEAD_DIM
            yd = _dot(mm, xdt_b[:, lo:lo + SSD_HEAD_DIM])
            yacc_ref[:, lo:lo + SSD_HEAD_DIM] = yd + yoff[:, eh * SSD_HEAD_DIM:(eh + 1) * SSD_HEAD_DIM]
    y = yacc_ref[...] + dskip_ref[...] * xs
    yz = y * _silu(z_ref[0])
    ms = jnp.mean(yz * yz, axis=-1, keepdims=True)
    y_ref[0] = (yz * lax.rsqrt(ms + EPS) * nw_ref[...]).astype(y_ref.dtype)

    @pl.when(c == pl.num_programs(1) - 1)
    def _():
        hT_ref[0] = h_ref[...]


def _head_expand(width=SSD_HEAD_DIM):
    e = np.zeros((SSD_HEADS, SSD_HEADS * width), np.float32)
    for h in range(SSD_HEADS):
        e[h, h * width:(h + 1) * width] = 1.0
    return jnp.asarray(e, BF16)


def _ssd_prompt(xbc, z, dt_raw, w):
    b, length, _ = xbc.shape
    q = SSD_CHUNK
    nc = length // q
    dtT = jnp.swapaxes(dt_raw, 1, 2)
    a = -jnp.exp(w['ssd_a_log'].astype(F32))
    full = lambda shape: pl.BlockSpec(shape, lambda i, j: (0,) * len(shape))
    y, hT = pl.pallas_call(
        _ssd_prompt_kernel,
        grid=(b, nc),
        in_specs=[pl.BlockSpec((1, q, SSD_CONV_DIM), lambda i, j: (i, j, 0)),
                  pl.BlockSpec((1, q, D_INNER), lambda i, j: (i, j, 0)),
                  pl.BlockSpec((1, q, SSD_HEADS), lambda i, j: (i, j, 0)),
                  pl.BlockSpec((1, SSD_HEADS, q), lambda i, j: (i, 0, j)),
                  full((SSD_CONV, SSD_CONV_DIM)), full((1, SSD_CONV_DIM)),
                  full((1, SSD_HEADS)), full((SSD_HEADS, 1)), full((1, SSD_HEADS)), full((SSD_HEADS, 1)),
                  full((1, D_INNER)), full((1, D_INNER)), full((SSD_HEADS, D_INNER))],
        out_specs=[pl.BlockSpec((1, q, D_INNER), lambda i, j: (i, j, 0)),
                   pl.BlockSpec((1, SSD_GROUPS, SSD_STATE, SSD_HPG * SSD_HEAD_DIM), lambda i, j: (i, 0, 0, 0))],
        out_shape=[jax.ShapeDtypeStruct((b, length, D_INNER), BF16),
                   jax.ShapeDtypeStruct((b, SSD_GROUPS, SSD_STATE, SSD_HPG * SSD_HEAD_DIM), F32)],
        scratch_shapes=[pltpu.VMEM((8, SSD_CONV_DIM), F32),
                        pltpu.VMEM((SSD_GROUPS, SSD_STATE, SSD_HPG * SSD_HEAD_DIM), F32),
                        pltpu.VMEM((q, D_INNER), F32)],
        compiler_params=_cparams("parallel", "arbitrary"),
        name="ssd_prompt",
    )(xbc, z, dt_raw, dtT, w['ssd_conv_w'], w['ssd_conv_b'].reshape(1, -1),
      w['ssd_dt_bias'].reshape(1, -1), w['ssd_dt_bias'].reshape(-1, 1), a.reshape(1, -1), a.reshape(-1, 1),
      jnp.repeat(w['ssd_d'].astype(F32), SSD_HEAD_DIM).reshape(1, -1), w['ssd_norm'].reshape(1, -1),
      _head_expand())
    state = hT.reshape(b, SSD_GROUPS, SSD_STATE, SSD_HPG, SSD_HEAD_DIM).transpose(0, 1, 3, 4, 2)
    return y, state.reshape(b, SSD_HEADS, SSD_HEAD_DIM, SSD_STATE)


def _segment_ones():
    i = np.arange(128)
    return jnp.asarray((i[:, None] // NSA_HD == i[None, :] // NSA_HD).astype(np.float32), BF16)


def _head_rmsnorm(x, g_row, segm):
    sq = x * x
    hi = sq.astype(BF16)
    lo = (sq - hi.astype(F32)).astype(BF16)
    parts = []
    for c in range(x.shape[1] // 128):
        sl = slice(c * 128, (c + 1) * 128)
        parts.append(_dot(hi[:, sl], segm) + _dot(lo[:, sl], segm))
    ss = parts[0] if len(parts) == 1 else jnp.concatenate(parts, axis=1)
    return x * lax.rsqrt(ss * (1.0 / NSA_HD) + EPS) * g_row


def _tile_lanes(t, width):
    reps = width // t.shape[1]
    return t if reps == 1 else jnp.concatenate([t] * reps, axis=1)


def _rope(x, cos128, sin128):
    width = x.shape[1]
    lane = lax.broadcasted_iota(jnp.int32, x.shape, 1)
    first = (lane & (NSA_HD // 2)) == 0
    rot = jnp.where(first, pltpu.roll(x, width - NSA_HD // 2, axis=1), pltpu.roll(x, NSA_HD // 2, axis=1))
    return x * _tile_lanes(cos128, width) + rot * _tile_lanes(sin128, width)


def _rope_tables(pos):
    half = NSA_HD // 2
    inv = ROPE_THETA ** (-jnp.arange(half, dtype=F32) / half)
    ang = pos.astype(F32)[:, None] * inv[None, :]
    cos, sin = jnp.cos(ang), jnp.sin(ang)
    cos128 = jnp.concatenate([cos, cos, cos, cos], axis=1)
    sin128 = jnp.concatenate([-sin, sin, -sin, sin], axis=1)
    return cos128, sin128


def _norm_rope_t(x, g_col, cos, sin):
    nh = x.shape[0] // NSA_HD
    x3 = x.reshape(nh, NSA_HD, x.shape[1])
    ms = jnp.sum(x3 * x3, axis=1, keepdims=True) * (1.0 / NSA_HD)
    x3 = x3 * lax.rsqrt(ms + EPS) * g_col[None]
    half = NSA_HD // 2
    x1, x2 = x3[:, :half], x3[:, half:]
    out = jnp.concatenate([x1 * cos[None] - x2 * sin[None], x2 * cos[None] + x1 * sin[None]], axis=1)
    return out.reshape(x.shape)


def _nsa_prep_kernel(q_ref, kv_ref, cos_ref, sin_ref, qg_ref, kg_ref,
                     qt_ref, skaug_ref, skf_ref, svt_ref, wkr_ref, wkf_ref, wvt_ref):
    tl = q_ref.shape[2]
    i = pl.program_id(1)
    cos, sin = cos_ref[...], sin_ref[...]
    kvw = NSA_KV * NSA_HD
    qt_ref[0] = (_norm_rope_t(q_ref[0], qg_ref[...], cos, sin) * (NSA_HD ** -0.5 * math.log2(math.e))).astype(BF16)
    sk = _norm_rope_t(kv_ref[0, 2 * kvw:3 * kvw, :], kg_ref[:, 1:2], cos, sin)
    wk = _norm_rope_t(kv_ref[0, 4 * kvw:5 * kvw, :], kg_ref[:, 2:3], cos, sin)
    skf_ref[0] = sk
    wkf_ref[0] = wk
    svt_ref[0] = kv_ref[0, 3 * kvw:4 * kvw, :].astype(BF16)
    wvt_ref[0] = kv_ref[0, 5 * kvw:6 * kvw, :].astype(BF16)
    pos = i * tl + lax.broadcasted_iota(jnp.int32, (tl, NSA_HD), 0)
    blk = lax.broadcasted_iota(jnp.int32, (tl, NSA_HD), 1)
    onehot = jnp.where((pos >> 6) == blk, MASK_BIG, 0.0).astype(BF16)
    for k in range(NSA_KV):
        sl = slice(k * NSA_HD, (k + 1) * NSA_HD)
        skaug_ref[0, k] = jnp.concatenate([sk[sl].T.astype(BF16), onehot], axis=1)
        wkr_ref[0, k] = wk[sl].T.astype(BF16)


def _nsa_prep(qt, kvt, w, tl=256):
    b, _, length = qt.shape
    nl = length // tl
    half = NSA_HD // 2
    inv = ROPE_THETA ** (-jnp.arange(half, dtype=F32) / half)
    ang = inv[:, None] * jnp.arange(length, dtype=F32)[None, :]
    kvw = NSA_KV * NSA_HD
    fm = lambda rows: pl.BlockSpec((1, rows, tl), lambda bi, i: (bi, 0, i))
    tab = pl.BlockSpec((half, tl), lambda bi, i: (0, i))
    full = lambda shape: pl.BlockSpec(shape, lambda bi, i: (0,) * len(shape))
    rows = lambda wd: pl.BlockSpec((1, NSA_KV, tl, wd), lambda bi, i: (bi, 0, i, 0))
    return pl.pallas_call(
        _nsa_prep_kernel,
        grid=(b, nl),
        in_specs=[fm(NSA_HEADS * NSA_HD), fm(6 * kvw), tab, tab, full((NSA_HD, 1)), full((NSA_HD, 3))],
        out_specs=[fm(NSA_HEADS * NSA_HD), rows(2 * NSA_HD), fm(kvw), fm(kvw), rows(NSA_HD), fm(kvw), fm(kvw)],
        out_shape=[jax.ShapeDtypeStruct((b, NSA_HEADS * NSA_HD, length), BF16),
                   jax.ShapeDtypeStruct((b, NSA_KV, length, 2 * NSA_HD), BF16),
                   jax.ShapeDtypeStruct((b, kvw, length), F32),
                   jax.ShapeDtypeStruct((b, kvw, length), BF16),
                   jax.ShapeDtypeStruct((b, NSA_KV, length, NSA_HD), BF16),
                   jax.ShapeDtypeStruct((b, kvw, length), F32),
                   jax.ShapeDtypeStruct((b, kvw, length), BF16)],
        compiler_params=_cparams("parallel", "parallel"),
        name="nsa_prep",
    )(qt, kvt, jnp.cos(ang), jnp.sin(ang), w['nsa_q_norm'].reshape(-1, 1), w['nsa_k_norm'].T)


def _compress_kernel(*refs, n_x, n_prefetch):
    refs = refs[n_prefetch:]
    (wk_ref, pek_ref, w2k_ref, wv_ref, pev_ref, w2v_ref, g_ref, cos_ref, sin_ref, segm_ref, ok_ref, ov_ref,
     x0k_ref, x1k_ref, x0v_ref, x1v_ref) = refs[2 * n_x:]
    shared = (g_ref, cos_ref, sin_ref, segm_ref)
    _compress_one(refs[:n_x], wk_ref, pek_ref, w2k_ref, shared, ok_ref, x0k_ref, x1k_ref, True)
    _compress_one(refs[n_x:2 * n_x], wv_ref, pev_ref, w2v_ref, shared, ov_ref, x0v_ref, x1v_ref, False)


def _compress_one(page_refs, wbd_ref, pe_ref, w2_ref, shared, o_ref, x0_ref, x1_ref, is_k):
    g_ref, cos_ref, sin_ref, segm_ref = shared
    plen = page_refs[0].shape[-1]
    for j, r in enumerate(page_refs):
        rows = r[...].T
        x0_ref[j * plen:(j + 1) * plen, :] = rows[:, :128]
        x1_ref[j * plen:(j + 1) * plen, :] = rows[:, 128:]
    nseq = o_ref.shape[0]
    ntot = x0_ref.shape[0] // CMP_STRIDE
    nchunk = ntot // nseq
    row = lax.broadcasted_iota(jnp.int32, (ntot, 1), 0)
    outs = []
    for x_ref in (x0_ref, x1_ref):
        x = jnp.concatenate([x_ref[pl.ds(s, ntot, stride=CMP_STRIDE), :].astype(BF16)
                             for s in range(CMP_STRIDE)], axis=1)
        acc = _dot(x, wbd_ref[...])
        hid = []
        for kl in range(2):
            lo = kl * 2 * CMP_HIDDEN
            first = acc[:, lo:lo + CMP_HIDDEN]
            second = pltpu.roll(acc[:, lo + CMP_HIDDEN:lo + 2 * CMP_HIDDEN], ntot - 1, axis=0)
            hid.append(_silu(first + second + pe_ref[...]))
        outs.append(_dot(jnp.concatenate(hid, axis=1).astype(BF16), w2_ref[...]))
    out = jnp.concatenate(outs, axis=1)
    if is_k:
        tile = lambda t: t if nseq == 1 else jnp.concatenate([t] * nseq, axis=0)
        out = _rope(_head_rmsnorm(out, g_ref[...], segm_ref[...]), tile(cos_ref[...]), tile(sin_ref[...]))
    out = jnp.where((row & (nchunk - 1)) < nchunk - 1, out, 0.0)
    for s in range(nseq):
        o_ref[s] = out[s * nchunk:(s + 1) * nchunk]


def _compress_weights(pe, w1, w2):
    w1r = w1.reshape(CMP_LEN, NSA_HD, CMP_HIDDEN)
    pe_term = jnp.einsum('ld,ldm->m', pe, w1r, precision=HI).reshape(1, CMP_HIDDEN)
    both = jnp.concatenate([w1r[:CMP_STRIDE], w1r[CMP_STRIDE:]], axis=-1)
    zero = jnp.zeros_like(both)
    wbd = jnp.concatenate([jnp.concatenate([both, zero], axis=-1), jnp.concatenate([zero, both], axis=-1)], axis=1)
    zero2 = jnp.zeros_like(w2)
    w2bd = jnp.concatenate([jnp.concatenate([w2, zero2], axis=-1), jnp.concatenate([zero2, w2], axis=-1)], axis=0)
    wbd = wbd.reshape(CMP_STRIDE * 2 * NSA_HD, 4 * CMP_HIDDEN)
    return wbd.astype(BF16), pe_term, w2bd.astype(BF16)


def _compress_call(x_args, x_specs, grid, nchunk, batch, w, n_prefetch=0, prefetch=(), spb=1):
    assert nchunk & (nchunk - 1) == 0
    wk = _compress_weights(w['cmp_k_pe'], w['cmp_k_w1'], w['cmp_k_w2'])
    wv = _compress_weights(w['cmp_v_pe'], w['cmp_v_w1'], w['cmp_v_w2'])
    cos128, sin128 = _rope_tables(jnp.arange(nchunk, dtype=jnp.int32) * CMP_STRIDE + (CMP_LEN - 1))
    kvw = NSA_KV * NSA_HD
    full = lambda shape: pl.BlockSpec(shape, lambda *a: (0,) * len(shape))
    wspecs = [full(wk[0].shape), full((1, CMP_HIDDEN)), full(wk[2].shape)]
    out = pl.BlockSpec((spb, nchunk, kvw), lambda i, *a: (i, 0, 0))
    gs = pltpu.PrefetchScalarGridSpec(
        num_scalar_prefetch=n_prefetch, grid=grid,
        in_specs=list(x_specs) + wspecs + wspecs + [full((1, kvw)), full((nchunk, 128)), full((nchunk, 128)),
                                                    full((128, 128))],
        out_specs=[out, out],
        scratch_shapes=[pltpu.VMEM((spb * nchunk * CMP_STRIDE, 128), F32)] * 4)
    return pl.pallas_call(
        functools.partial(_compress_kernel, n_x=len(x_specs) // 2, n_prefetch=n_prefetch),
        grid_spec=gs,
        out_shape=[jax.ShapeDtypeStruct((batch, nchunk, kvw), F32)] * 2,
        compiler_params=_cparams("parallel"),
        name="compress",
    )(*prefetch, *x_args, *wk, *wv, jnp.tile(w['nsa_k_norm'][0], NSA_KV).reshape(1, -1), cos128, sin128,
      _segment_ones())


def _compress_prompt(kvt, w, page=128):
    b, _, length = kvt.shape
    npages = length // page
    specs = [pl.BlockSpec((None, NSA_KV * NSA_HD, page), functools.partial(lambda i, c, j: (i, c, j), c=c, j=j))
             for c in range(2) for j in range(npages)]
    return _compress_call([kvt] * len(specs), specs, (b,), length // CMP_STRIDE, b, w)


SEL_TILE = 512


def _flash_step(carry, s, vt):
    m, l, acc = carry
    m_new = jnp.maximum(m, jnp.max(s, axis=0, keepdims=True))
    alpha = jnp.exp2(m - m_new)
    p = jnp.exp2(s - m_new)
    l = alpha * l + jnp.sum(p, axis=0, keepdims=True)
    acc = alpha * acc + _dot(vt, p.astype(BF16))
    return m_new, l, acc


def _topn_bias(score, n):
    nblk, nq = score.shape
    groups = [score[8 * v:8 * v + 8] for v in range(nblk // 8)]
    sub = lax.broadcasted_iota(jnp.int32, (8, nq), 0)
    cnt = [jnp.zeros((8, nq), F32) for _ in groups]
    for jp in range(nblk):
        row = score[jp:jp + 1]
        for v, grp in enumerate(groups):
            if v < jp // 8:
                inc = jnp.where(row > grp, 1.0, 0.0)
            elif v > jp // 8:
                inc = jnp.where(row >= grp, 1.0, 0.0)
            else:
                inc = jnp.where(sub > jp % 8, jnp.where(row >= grp, 1.0, 0.0), jnp.where(row > grp, 1.0, 0.0))
            cnt[v] = cnt[v] + inc
    return jnp.concatenate([jnp.where(c < n, 0.0, -1.0) for c in cnt], axis=0)


def _nsa_prompt_kernel(q_ref, kc_ref, vc_ref, sk_ref, sv_ref, wk_ref, wv_ref, g_ref, ov_ref, o_ref):
    qb = pl.program_id(2)
    qlen = QUERY_BLOCK
    cols = NSA_GQ * qlen
    t0 = qb * qlen
    qt = jnp.concatenate([q_ref[0, g * NSA_HD:(g + 1) * NSA_HD, :] for g in range(NSA_GQ)], axis=1)
    tcol = t0 + (lax.broadcasted_iota(jnp.int32, (1, cols), 1) & (qlen - 1))
    ncmp = kc_ref.shape[2]
    s = _dot(kc_ref[0, 0], qt)
    kend = lax.broadcasted_iota(jnp.int32, (ncmp, 1), 0) * CMP_STRIDE + (CMP_LEN - 1)
    valid = kend <= tcol
    s = jnp.where(valid, s, NEG)
    m = jnp.max(s, axis=0, keepdims=True)
    p = jnp.where(valid, jnp.exp2(s - m), 0.0)
    den = jnp.sum(p, axis=0, keepdims=True)
    p = p * (1.0 / jnp.where(den > 0, den, 1.0))
    o_c = _dot(vc_ref[0, 0], p.astype(BF16))
    psum = p[:, 0:qlen] + p[:, qlen:2 * qlen] + p[:, 2 * qlen:3 * qlen] + p[:, 3 * qlen:4 * qlen]
    nblk = ov_ref.shape[0]
    imp = _dot_hi(ov_ref[...], psum)
    jb = lax.broadcasted_iota(jnp.int32, (nblk, 1), 0)
    t = t0 + lax.broadcasted_iota(jnp.int32, (1, qlen), 1)
    cur = t >> 6
    forced = (jb == 0) | (jb == cur) | (jb == cur - 1)
    score = jnp.where(jb * SEL_BLOCK <= t, imp + jnp.where(forced, SEL_FORCE, 0.0), NEG)
    bias = _topn_bias(score, min(SEL_TOPN, nblk)).astype(BF16)
    qaug = jnp.concatenate([qt, jnp.concatenate([bias] * NSA_GQ, axis=1)], axis=0)
    init = (jnp.full((1, cols), NEG, F32), jnp.zeros((1, cols), F32), jnp.zeros((NSA_HD, cols), F32))

    def sel_scores(k0):
        return _dot(sk_ref[0, 0, pl.ds(k0, SEL_TILE), :], qaug)

    def sel_pair(kp, carry):
        m, l, acc = carry
        k0 = pl.multiple_of(kp * (2 * SEL_TILE), 2 * SEL_TILE)
        k1 = pl.multiple_of(k0 + SEL_TILE, SEL_TILE)
        s0, s1 = sel_scores(k0), sel_scores(k1)
        m_new = jnp.maximum(m, jnp.maximum(jnp.max(s0, axis=0, keepdims=True), jnp.max(s1, axis=0, keepdims=True)))
        alpha = jnp.exp2(m - m_new)
        p0, p1 = jnp.exp2(s0 - m_new), jnp.exp2(s1 - m_new)
        l = alpha * l + jnp.sum(p0, axis=0, keepdims=True) + jnp.sum(p1, axis=0, keepdims=True)
        acc = (alpha * acc + _dot(sv_ref[0, 0, :, pl.ds(k0, SEL_TILE)], p0.astype(BF16))
               + _dot(sv_ref[0, 0, :, pl.ds(k1, SEL_TILE)], p1.astype(BF16)))
        return m_new, l, acc

    def sel_single(kt, carry):
        k0 = pl.multiple_of(kt * SEL_TILE, SEL_TILE)
        return _flash_step(carry, sel_scores(k0), sv_ref[0, 0, :, pl.ds(k0, SEL_TILE)])

    n_full = t0 // SEL_TILE
    carry = lax.fori_loop(0, n_full // 2, sel_pair, init)
    carry = lax.fori_loop(n_full - n_full % 2, n_full, sel_single, carry)
    k0 = pl.multiple_of(n_full * SEL_TILE, SEL_TILE)

    kpos = k0 + lax.broadcasted_iota(jnp.int32, (SEL_TILE, 1), 0)
    sc = jnp.where(kpos <= tcol, sel_scores(k0), NEG)
    _, l_s, acc_s = _flash_step(carry, sc, sv_ref[0, 0, :, pl.ds(k0, SEL_TILE)])
    o_s = acc_s * (1.0 / l_s)

    span = min(WINDOW + qlen, wk_ref.shape[2])
    kw0 = pl.multiple_of(jnp.maximum(t0 - WINDOW, 0), qlen)
    kpos = kw0 + lax.broadcasted_iota(jnp.int32, (span, 1), 0)
    s_w = _dot(wk_ref[0, 0, pl.ds(kw0, span), :], qt)
    s_w = jnp.where(kpos <= tcol, jnp.where(kpos >= tcol - WINDOW, s_w, NEG), NEG)
    p_w = jnp.exp2(s_w - jnp.max(s_w, axis=0, keepdims=True))
    o_w = _dot(wv_ref[0, 0, :, pl.ds(kw0, span)], p_w.astype(BF16)) * (1.0 / jnp.sum(p_w, axis=0, keepdims=True))

    gate = _sigmoid(g_ref[0, 0])
    outs = []
    for g in range(NSA_GQ):
        c = slice(g * qlen, (g + 1) * qlen)
        outs.append(gate[3 * g:3 * g + 1] * o_c[:, c] + gate[3 * g + 1:3 * g + 2] * o_s[:, c]
                    + gate[3 * g + 2:3 * g + 3] * o_w[:, c])
    o_ref[...] = jnp.concatenate(outs, axis=0).T.astype(o_ref.dtype)


def _overlap_matrix(ncmp, nblk):
    ci = np.arange(ncmp)[:, None] * CMP_STRIDE
    sj = np.arange(nblk)[None, :] * SEL_BLOCK
    return jnp.asarray(((ci <= sj + SEL_BLOCK - 1) & (ci + CMP_LEN - 1 >= sj)).astype(np.float32))


def _nsa_prompt(qt, kc, vct, skaug, svt, wkr, wvt, gates_t):
    b, _, length = qt.shape
    nb = length // QUERY_BLOCK
    ncmp = kc.shape[2]
    nblk = NSA_HD
    assert length // SEL_BLOCK <= nblk
    per_kv = lambda r, c: pl.BlockSpec((1, 1, r, c), lambda bi, k, i: (bi, k, 0, 0))
    return pl.pallas_call(
        _nsa_prompt_kernel,
        grid=(b, NSA_KV, nb),
        in_specs=[pl.BlockSpec((1, NSA_GQ * NSA_HD, QUERY_BLOCK), lambda bi, k, i: (bi, k, i)),
                  per_kv(ncmp, NSA_HD), per_kv(NSA_HD, ncmp), per_kv(length, 2 * NSA_HD), per_kv(NSA_HD, length),
                  per_kv(length, NSA_HD), per_kv(NSA_HD, length),
                  pl.BlockSpec((1, 1, 3 * NSA_GQ, QUERY_BLOCK), lambda bi, k, i: (bi, k, 0, i)),
                  pl.BlockSpec((nblk, ncmp), lambda bi, k, i: (0, 0))],
        out_specs=pl.BlockSpec((QUERY_BLOCK, NSA_GQ * NSA_HD), lambda bi, k, i: (bi * nb + i, k)),
        out_shape=jax.ShapeDtypeStruct((b * length, NSA_HEADS * NSA_HD), BF16),
        compiler_params=_cparams("parallel", "parallel", "arbitrary"),
        name="nsa_prompt",
    )(qt, kc, vct, skaug, svt, wkr, wvt, gates_t, _overlap_matrix(ncmp, nblk).T)


def _mem_prompt_kernel(q_ref, mk_ref, mv_ref, g_ref, o_ref):
    for h in range(MEM_HEADS):
        sl = slice(h * MEM_HD, (h + 1) * MEM_HD)
        q = q_ref[:, sl]
        ms = jnp.mean(q * q, axis=-1, keepdims=True)
        qn = (q * lax.rsqrt(ms + EPS) * g_ref[...] * (MEM_HD ** -0.5)).astype(BF16)
        s = _dot_nt(qn, mk_ref[0, :, sl])
        m = jnp.max(s, axis=-1, keepdims=True)
        p = jnp.exp(s - m)
        p = p / jnp.sum(p, axis=-1, keepdims=True)
        o_ref[:, sl] = _dot(p.astype(BF16), mv_ref[0, :, sl]).astype(o_ref.dtype)


def _mem_prompt(q_mem, mk, mv, qnorm, b, length, tq=256):
    nl = length // tq
    width = MEM_HEADS * MEM_HD
    mem = pl.BlockSpec((1, mk.shape[1], width), lambda bi, i: (bi, 0, 0))
    return pl.pallas_call(
        _mem_prompt_kernel,
        grid=(b, nl),
        in_specs=[pl.BlockSpec((tq, width), lambda bi, i: (bi * nl + i, 0)), mem, mem,
                  pl.BlockSpec((1, MEM_HD), lambda bi, i: (0, 0))],
        out_specs=pl.BlockSpec((tq, width), lambda bi, i: (bi * nl + i, 0)),
        out_shape=jax.ShapeDtypeStruct((b * length, width), BF16),
        compiler_params=_cparams("parallel", "parallel"),
        name="mem_prompt",
    )(q_mem, mk, mv, qnorm.reshape(1, -1))


def _merge_kernel(x_ref, ys_ref, yn_ref, ym_ref, gm_ref, bm_ref, ws_ref, wn_ref, wm_ref, wo_ref, o_ref):
    gate = _sigmoid(gm_ref[...] + bm_ref[...])
    mixed = (gate[:, :D_MODEL] * _dot(ys_ref[...], ws_ref[...])
             + gate[:, D_MODEL:2 * D_MODEL] * _dot(yn_ref[...], wn_ref[...])
             + gate[:, 2 * D_MODEL:] * _dot(ym_ref[...], wm_ref[...]))
    o_ref[...] = x_ref[...] + _dot(mixed.astype(BF16), wo_ref[...])


def _merge(x, ys, yn, ym, gm, wb, tm=256):
    m = x.shape[0]
    tm = min(tm, m)
    row = lambda wd: pl.BlockSpec((tm, wd), lambda i: (i, 0))
    full = lambda shape: pl.BlockSpec(shape, lambda i: (0,) * len(shape))
    return pl.pallas_call(
        _merge_kernel,
        grid=(m // tm,),
        in_specs=[row(D_MODEL), row(D_INNER), row(D_MODEL), row(D_MODEL), row(3 * D_MODEL), full((1, 3 * D_MODEL)),
                  full((D_INNER, D_MODEL)), full((D_MODEL, D_MODEL)), full((D_MODEL, D_MODEL)),
                  full((D_MODEL, D_MODEL))],
        out_specs=row(D_MODEL),
        out_shape=jax.ShapeDtypeStruct((m, D_MODEL), F32),
        compiler_params=_cparams("parallel"),
        name="merge",
    )(x, ys, yn, ym, gm, wb['b_merge'].reshape(1, -1), wb['w_ssd_o'], wb['w_nsa_o'], wb['w_mem_o'], wb['w_out'])


FFN_COLS = 2816


def _ffn_kernel(*refs, seq):
    if seq:
        x_ref, g_ref, wu_ref, cw_ref, cb_ref, wd_ref, o_ref, ul_ref, carry_ref = refs
    else:
        x_ref, g_ref, wu_ref, cw_ref, cb_ref, wd_ref, p0_ref, p1_ref, o_ref, ul_ref = refs
    x = x_ref[...]
    tm = x.shape[0]
    ms = jnp.mean(x * x, axis=-1, keepdims=True)
    xn = (x * lax.rsqrt(ms + EPS) * g_ref[...]).astype(BF16)
    if seq:
        @pl.when(pl.program_id(1) == 0)
        def _():
            carry_ref[...] = jnp.zeros_like(carry_ref)

        row = lax.broadcasted_iota(jnp.int32, (8, 1), 0)

    def conv_cols(lo):
        sl = slice(lo, lo + FFN_COLS)
        u = _dot(xn, wu_ref[:, sl])
        if seq:
            prev = carry_ref[:, sl]
            u1 = _shift_rows(u, 1, prev, row)
            u2 = _shift_rows(u, 2, prev, row)
            carry_ref[:, sl] = u[tm - 8:, :]
        else:
            u2, u1 = p0_ref[:, sl], p1_ref[:, sl]
            ul_ref[:, sl] = u
        return cb_ref[:, sl] + cw_ref[0:1, sl] * u2 + cw_ref[1:2, sl] * u1 + cw_ref[2:3, sl] * u

    acc = x
    for j in range(D_FF // FFN_COLS):
        act = (_silu(conv_cols(j * FFN_COLS)) * conv_cols(D_FF + j * FFN_COLS)).astype(BF16)
        acc = acc + _dot(act, wd_ref[j * FFN_COLS:(j + 1) * FFN_COLS, :])
    o_ref[...] = acc
    if seq:
        @pl.when(pl.program_id(1) == pl.num_programs(1) - 1)
        def _():
            ul_ref[0] = carry_ref[...]


def _ffn_prompt(x1, w, wb, b, length, tm=256):
    nl = length // tm
    row = lambda wd: pl.BlockSpec((tm, wd), lambda bi, i: (bi * nl + i, 0))
    const = lambda shape: pl.BlockSpec(shape, lambda bi, i: (0,) * len(shape), pipeline_mode=pl.Buffered(1))
    return pl.pallas_call(
        functools.partial(_ffn_kernel, seq=True),
        grid=(b, nl),
        in_specs=[row(D_MODEL), const((1, D_MODEL)), const((D_MODEL, 2 * D_FF)), const((FFN_CONV, 2 * D_FF)),
                  const((1, 2 * D_FF)), const((D_FF, D_MODEL))],
        out_specs=[row(D_MODEL), pl.BlockSpec((1, 8, 2 * D_FF), lambda bi, i: (bi, 0, 0))],
        out_shape=[jax.ShapeDtypeStruct((b * length, D_MODEL), F32), jax.ShapeDtypeStruct((b, 8, 2 * D_FF), F32)],
        scratch_shapes=[pltpu.VMEM((8, 2 * D_FF), F32)],
        compiler_params=_cparams("parallel", "arbitrary"),
        name="ffn_prompt",
    )(x1, w['norm_ffn'].reshape(1, -1), wb['w_up'], wb['ffn_conv_w'], wb['ffn_conv_b'].reshape(1, -1), wb['w_down'])


def _ffn_sample(x1, prev, w, wb):
    m = x1.shape[0]
    full = lambda shape: pl.BlockSpec(shape, lambda i: (0,) * len(shape))
    return pl.pallas_call(
        functools.partial(_ffn_kernel, seq=False),
        grid=(1,),
        in_specs=[full((m, D_MODEL)), full((1, D_MODEL)), full((D_MODEL, 2 * D_FF)), full((FFN_CONV, 2 * D_FF)),
                  full((1, 2 * D_FF)), full((D_FF, D_MODEL)), full((m, 2 * D_FF)), full((m, 2 * D_FF))],
        out_specs=[full((m, D_MODEL)), full((m, 2 * D_FF))],
        out_shape=[jax.ShapeDtypeStruct((m, D_MODEL), F32), jax.ShapeDtypeStruct((m, 2 * D_FF), F32)],
        compiler_params=_cparams("arbitrary"),
        name="ffn_sample",
    )(x1, w['norm_ffn'].reshape(1, -1), wb['w_up'], wb['ffn_conv_w'], wb['ffn_conv_b'].reshape(1, -1), wb['w_down'],
      prev[:, 0], prev[:, 1])


def _compress_sample(pool_k, pool_v, page_table, w):
    bs, npages = page_table.shape
    page = pool_k.shape[1]
    spb = 2 if bs % 2 == 0 else 1
    specs = [pl.BlockSpec((None, NSA_KV * NSA_HD, page),
                          functools.partial(lambda b, pt, s, j: (pt[b * spb + s, j], 0, 0), s=s, j=j))
             for s in range(spb) for j in range(npages)]
    return _compress_call([_pool_t(pool_k)] * len(specs) + [_pool_t(pool_v)] * len(specs), specs + specs,
                          (bs // spb,), npages * page // CMP_STRIDE, bs, w, n_prefetch=1, prefetch=(page_table,),
                          spb=spb)


def _pool_t(pool):
    n, page = pool.shape[:2]
    return pool.transpose(0, 2, 3, 1).reshape(n, NSA_KV * NSA_HD, page)


def _rows8(x):
    return jnp.broadcast_to(x, (8, x.shape[1]))


def _ssd_step_kernel(xbc_ref, prev_ref, z_ref, dt_ref, h_ref, cw_ref, cb_ref, dtb_ref, a_ref, dskip_ref, nw_ref,
                     e64_ref, e128_ref, y_ref, hn_ref):
    conv = cb_ref[...] + cw_ref[3:4, :] * xbc_ref[0]
    for k in range(SSD_CONV - 1):
        conv = conv + cw_ref[k:k + 1, :] * prev_ref[0, k:k + 1, :]
    xc = _silu(conv)
    xs = xc[:, :D_INNER]
    nb = SSD_GROUPS * SSD_STATE
    bm = xc[:, D_INNER:D_INNER + nb]
    cm = xc[:, D_INNER + nb:]
    v = dt_ref[0] + dtb_ref[...]
    dt = jnp.maximum(v, 0.0) + jnp.log(1.0 + jnp.exp(-jnp.abs(v)))
    dec = jnp.exp(dt * a_ref[...])
    xdt = xs * _dot_x3(_rows8(dt), e64_ref[...], 0)[0:1]
    dec128 = _dot_x3(_rows8(dec), e128_ref[...], 0)
    pieces = []
    for c in range(D_INNER // 128):
        g = c // (SSD_HPG // 2)
        xcol = jnp.broadcast_to(xdt[:, c * 128:(c + 1) * 128], (128, 128)).T
        bg = jnp.broadcast_to(bm[:, g * SSD_STATE:(g + 1) * SSD_STATE], (128, SSD_STATE))
        decv = jnp.concatenate(
            [jnp.concatenate([dec128[:, hh * 128:(hh + 1) * 128]] * (SSD_HEAD_DIM // 8), axis=0)
             for hh in (2 * c, 2 * c + 1)], axis=0)
        hnew = decv * h_ref[0, c * 128:(c + 1) * 128, :] + xcol * bg
        hn_ref[0, c * 128:(c + 1) * 128, :] = hnew
        cg = _rows8(cm[:, g * SSD_STATE:(g + 1) * SSD_STATE]).astype(BF16)
        pieces.append(_dot_nt(cg, hnew.astype(BF16))[0:1])
    y = jnp.concatenate(pieces, axis=1) + dskip_ref[...] * xs
    yz = y * _silu(z_ref[0])
    ms = jnp.mean(yz * yz, axis=-1, keepdims=True)
    y_ref[0] = (yz * lax.rsqrt(ms + EPS) * nw_ref[...]).astype(y_ref.dtype)


def _ssd_step(xbc, prev, z, dt_raw, h0, w):
    bs = xbc.shape[0]
    a = -jnp.exp(w['ssd_a_log'].astype(F32))
    row = lambda wd: pl.BlockSpec((1, 1, wd), lambda i: (i, 0, 0))
    full = lambda shape: pl.BlockSpec(shape, lambda i: (0,) * len(shape))
    st = pl.BlockSpec((1, D_INNER, SSD_STATE), lambda i: (i, 0, 0))
    y, hn = pl.pallas_call(
        _ssd_step_kernel,
        grid=(bs,),
        in_specs=[row(SSD_CONV_DIM), pl.BlockSpec((1, SSD_CONV - 1, SSD_CONV_DIM), lambda i: (i, 0, 0)),
                  row(D_INNER), row(SSD_HEADS), st,
                  full((SSD_CONV, SSD_CONV_DIM)), full((1, SSD_CONV_DIM)), full((1, SSD_HEADS)), full((1, SSD_HEADS)),
                  full((1, D_INNER)), full((1, D_INNER)), full((SSD_HEADS, D_INNER)),
                  full((SSD_HEADS, SSD_HEADS * 128))],
        out_specs=[row(D_INNER), st],
        out_shape=[jax.ShapeDtypeStruct((bs, 1, D_INNER), BF16),
                   jax.ShapeDtypeStruct((bs, D_INNER, SSD_STATE), F32)],
        compiler_params=_cparams("parallel"),
        name="ssd_step",
    )(xbc.reshape(bs, 1, -1), prev, z.reshape(bs, 1, -1), dt_raw.reshape(bs, 1, -1),
      h0.reshape(bs, D_INNER, SSD_STATE), w['ssd_conv_w'], w['ssd_conv_b'].reshape(1, -1),
      w['ssd_dt_bias'].reshape(1, -1), a.reshape(1, -1),
      jnp.repeat(w['ssd_d'].astype(F32), SSD_HEAD_DIM).reshape(1, -1), w['ssd_norm'].reshape(1, -1),
      _head_expand(), _head_expand(128))
    return y.reshape(bs, D_INNER), hn.reshape(bs, SSD_HEADS, SSD_HEAD_DIM, SSD_STATE)


def _softmax_with_extra(s, s_new, valid=None):
    if valid is not None:
        s = jnp.where(valid, s, NEG)
    m = jnp.maximum(jnp.max(s, axis=-1, keepdims=True), s_new)
    p = jnp.exp(s - m)
    if valid is not None:
        p = jnp.where(valid, p, 0.0)
    p_new = jnp.exp(s_new - m)
    inv = 1.0 / (jnp.sum(p, axis=-1, keepdims=True) + p_new)
    return p * inv, p_new * inv


def _nsa_sample_kernel(*refs, npages, past_len, spb):
    q_ref, kv_ref, gate_ref = refs[1:4]
    skp = refs[4:4 + spb * npages]
    svp = refs[4 + spb * npages:4 + 2 * spb * npages]
    (kc_ref, vc_ref, wk_ref, wv_ref, qg_ref, kg_ref, cos_ref, sin_ref, segm_ref, ov_ref, eblk_ref, fold_ref,
     foldt_ref, o_ref, skn_ref, wkn_ref, wko_ref, wvo_ref) = refs[4 + 2 * spb * npages:]
    kvw = NSA_KV * NSA_HD
    segm = segm_ref[...]
    cos, sin = _rows8(cos_ref[...]), _rows8(sin_ref[...])
    hrow = lax.broadcasted_iota(jnp.int32, (NSA_HEADS, NSA_HEADS * NSA_HD), 0)
    hcol = lax.broadcasted_iota(jnp.int32, (NSA_HEADS, NSA_HEADS * NSA_HD), 1)
    own = (hcol >> 6) == hrow
    ncmp = kc_ref.shape[1]
    kend = lax.broadcasted_iota(jnp.int32, (1, ncmp), 1) * CMP_STRIDE + (CMP_LEN - 1)
    valid_c = kend <= past_len
    gi = lax.broadcasted_iota(jnp.int32, (NSA_HEADS, NSA_HEADS), 0) // NSA_GQ
    gj = lax.broadcasted_iota(jnp.int32, (NSA_HEADS, NSA_HEADS), 1) // NSA_GQ
    same_group = jnp.where(gi == gj, 1.0, 0.0)

    first = []
    for s in range(spb):
        q = _rope(_head_rmsnorm(_rows8(q_ref[s]), qg_ref[...], segm), cos, sin) * (NSA_HD ** -0.5)
        kv = kv_ref[s]
        sk_new = _rope(_head_rmsnorm(_rows8(kv[:, 2 * kvw:3 * kvw]), kg_ref[1:2, :], segm), cos, sin)[0:1]
        wk_new = _rope(_head_rmsnorm(_rows8(kv[:, 4 * kvw:5 * kvw]), kg_ref[2:3, :], segm), cos, sin)[0:1]
        skn_ref[s] = sk_new
        wkn_ref[s] = wk_new
        q16 = jnp.where(own, jnp.concatenate([q, q], axis=0), 0.0).astype(BF16)
        qbd = _dot(q16, fold_ref[...])
        qbd_b = qbd.astype(BF16)
        s_c = jnp.where(valid_c, _dot_nt(qbd_b, kc_ref[s].astype(BF16)), NEG)
        m_c = jnp.max(s_c, axis=-1, keepdims=True)
        p_c = jnp.where(valid_c, jnp.exp(s_c - m_c), 0.0)
        den = jnp.sum(p_c, axis=-1, keepdims=True)
        p_c = p_c / jnp.where(den > 0, den, 1.0)
        o_c = _dot(p_c.astype(BF16), vc_ref[s].astype(BF16))
        first.append((kv, sk_new, wk_new, qbd, qbd_b, o_c, _dot_hi(same_group, p_c)))

    psum = jnp.concatenate([f[6] for f in first] + [jnp.zeros((128 - spb * NSA_HEADS, ncmp), F32)], axis=0)
    imp = lax.dot_general(ov_ref[...], psum, (((1,), (1,)), ((), ())), preferred_element_type=F32,
                          precision=HI)
    nslot = ov_ref.shape[0]
    jb = lax.broadcasted_iota(jnp.int32, (nslot, 1), 0)
    cur = past_len // SEL_BLOCK
    forced = (jb == 0) | (jb == cur) | (jb == cur - 1)
    score = jnp.where(jb * SEL_BLOCK <= past_len, imp + jnp.where(forced, SEL_FORCE, 0.0), NEG)
    bias_all = _topn_bias(score, SEL_TOPN).T

    nwin = wk_ref.shape[2]
    lane = lax.broadcasted_iota(jnp.int32, (kvw, nwin), 1)

    def column(rowvec):
        col = jnp.broadcast_to(rowvec, (128, kvw)).T
        return jnp.concatenate([col] * (nwin // 128), axis=1)

    for s, (kv, sk_new, wk_new, qbd, qbd_b, o_c, _) in enumerate(first):
        sv_new = kv[:, 3 * kvw:4 * kvw]
        wv_new = kv[:, 5 * kvw:6 * kvw]
        bias = bias_all[s * NSA_HEADS:(s + 1) * NSA_HEADS]
        kmask = _dot(bias.astype(BF16), eblk_ref[...])
        pages = slice(s * npages, (s + 1) * npages)
        sk_all = jnp.concatenate([r[...] for r in skp[pages]], axis=1).astype(BF16)
        sv_all = jnp.concatenate([r[...] for r in svp[pages]], axis=1).astype(BF16)
        s_s = _dot(qbd_b, sk_all) + kmask
        s_new = jnp.sum(qbd * sk_new, axis=-1, keepdims=True)
        p_s, p_new = _softmax_with_extra(s_s, s_new)
        o_s = _dot_nt(p_s.astype(BF16), sv_all) + p_new * sv_new
        wk = wk_ref[s]
        wv = wv_ref[s]
        s_w = _dot(qbd_b, wk.astype(BF16))
        s_wn = jnp.sum(qbd * wk_new, axis=-1, keepdims=True)
        p_w, p_wn = _softmax_with_extra(s_w, s_wn)
        o_w = _dot_nt(p_w.astype(BF16), wv.astype(BF16)) + p_wn * wv_new
        g16 = jnp.where((lax.broadcasted_iota(jnp.int32, (NSA_HEADS, 3 * NSA_HEADS), 1) // 3)
                        == lax.broadcasted_iota(jnp.int32, (NSA_HEADS, 3 * NSA_HEADS), 0),
                        jnp.broadcast_to(_sigmoid(gate_ref[s]), (NSA_HEADS, 3 * NSA_HEADS)), 0.0)
        br = lax.broadcasted_iota(jnp.int32, (NSA_HEADS, 3 * NSA_HEADS), 1) % 3
        gsel = lambda r: jnp.sum(jnp.where(br == r, g16, 0.0), axis=-1, keepdims=True)
        o16 = gsel(0) * o_c + gsel(1) * o_s + gsel(2) * o_w
        ox = _dot(o16.astype(BF16), foldt_ref[...])
        o_ref[s] = jnp.sum(jnp.where(own, ox, 0.0), axis=0, keepdims=True).astype(o_ref.dtype)
        wko_ref[s] = jnp.where(lane == nwin - 1, column(wk_new), pltpu.roll(wk, nwin - 1, axis=1))
        wvo_ref[s] = jnp.where(lane == nwin - 1, column(wv_new), pltpu.roll(wv, nwin - 1, axis=1))


def _nsa_sample(q, kv, gates, pool_sk, pool_sv, page_table, kc, vc, win_k, win_v, w):
    bs, npages = page_table.shape
    page = pool_sk.shape[1]
    past_len = npages * page
    kvw = NSA_KV * NSA_HD
    nwin = win_k.shape[1]
    assert nwin == WINDOW and past_len % SEL_BLOCK == 0
    ncmp = kc.shape[1]
    nslot = 64
    assert past_len // SEL_BLOCK + 1 <= nslot
    cos128, sin128 = _rope_tables(jnp.full((1,), past_len, jnp.int32))
    overlap = _overlap_matrix(ncmp, nslot).T
    key_blk = np.arange(past_len) // SEL_BLOCK
    eblk = jnp.asarray((np.arange(nslot)[:, None] == key_blk[None, :]).astype(np.float32) * MASK_BIG, BF16)
    src = np.arange(NSA_HEADS * NSA_HD)
    dst = (src // NSA_HD // NSA_GQ) * NSA_HD + src % NSA_HD
    fold_np = np.zeros((NSA_HEADS * NSA_HD, kvw), np.float32)
    fold_np[src, dst] = 1.0
    fold = jnp.asarray(fold_np, BF16)
    foldt = jnp.asarray(fold_np.T, BF16)
    spb = 2 if bs % 2 == 0 else 1
    r3 = lambda t: t.reshape(bs, 1, -1)
    row = lambda wd: pl.BlockSpec((spb, 1, wd), lambda b, pt: (b, 0, 0))
    full = lambda shape: pl.BlockSpec(shape, lambda b, pt: (0,) * len(shape))
    per_b = lambda r, c: pl.BlockSpec((spb, r, c), lambda b, pt: (b, 0, 0))
    pages = [pl.BlockSpec((None, kvw, page), functools.partial(lambda b, pt, s, j: (pt[b * spb + s, j], 0, 0), s=s, j=j))
             for s in range(spb) for j in range(npages)]
    npages_all = spb * npages
    pk3 = _pool_t(pool_sk)
    pv3 = _pool_t(pool_sv)
    gs = pltpu.PrefetchScalarGridSpec(
        num_scalar_prefetch=1, grid=(bs // spb,),
        in_specs=[row(NSA_HEADS * NSA_HD), row(6 * kvw), row(3 * NSA_HEADS)] + pages + pages
        + [per_b(ncmp, kvw), per_b(ncmp, kvw), per_b(kvw, nwin), per_b(kvw, nwin),
           full((1, NSA_HEADS * NSA_HD)), full((3, kvw)), full((1, 128)), full((1, 128)), full((128, 128)),
           full((nslot, ncmp)), full((nslot, past_len)), full(fold.shape), full(foldt.shape)],
        out_specs=[row(NSA_HEADS * NSA_HD), row(kvw), row(kvw), per_b(kvw, nwin), per_b(kvw, nwin)])
    return pl.pallas_call(
        functools.partial(_nsa_sample_kernel, npages=npages, past_len=past_len, spb=spb),
        grid_spec=gs,
        out_shape=[jax.ShapeDtypeStruct((bs, 1, NSA_HEADS * NSA_HD), BF16),
                   jax.ShapeDtypeStruct((bs, 1, kvw), F32), jax.ShapeDtypeStruct((bs, 1, kvw), F32),
                   jax.ShapeDtypeStruct((bs, kvw, nwin), F32), jax.ShapeDtypeStruct((bs, kvw, nwin), F32)],
        compiler_params=_cparams("parallel"),
        name="nsa_sample",
    )(page_table, r3(q), r3(kv), r3(gates), *([pk3] * npages_all), *([pv3] * npages_all), kc, vc,
      _pool_t(win_k), _pool_t(win_v),
      jnp.tile(w['nsa_q_norm'], NSA_HEADS).reshape(1, -1), jnp.tile(w['nsa_k_norm'], (1, NSA_KV)),
      cos128, sin128, _segment_ones(), overlap, eblk, fold, foldt)


def _mem_sample_kernel(q_ref, mk_ref, mv_ref, g_ref, o_ref):
    spb, mtok = mk_ref.shape[0], mk_ref.shape[1]
    nrow = mtok * MEM_HEADS
    rows = []
    for s in range(spb):
        q = q_ref[s]
        for h in range(MEM_HEADS):
            qh = q[:, h * MEM_HD:(h + 1) * MEM_HD]
            ms = jnp.mean(qh * qh, axis=-1, keepdims=True)
            rows.append(qh * lax.rsqrt(ms + EPS) * g_ref[...] * (MEM_HD ** -0.5))
    q8 = jnp.concatenate(rows, axis=0).astype(BF16)
    mk = jnp.concatenate([mk_ref[s].reshape(nrow, MEM_HD) for s in range(spb)], axis=0).astype(BF16)
    mv = jnp.concatenate([mv_ref[s].reshape(nrow, MEM_HD) for s in range(spb)], axis=0).astype(BF16)
    col = lax.broadcasted_iota(jnp.int32, (spb * MEM_HEADS, spb * nrow), 1)
    own = ((col // nrow) * MEM_HEADS + (col & (MEM_HEADS - 1))
           == lax.broadcasted_iota(jnp.int32, (spb * MEM_HEADS, spb * nrow), 0))
    s = jnp.where(own, _dot_nt(q8, mk), NEG)
    m = jnp.max(s, axis=-1, keepdims=True)
    p = jnp.exp(s - m)
    p = p / jnp.sum(p, axis=-1, keepdims=True)
    o8 = _dot(p.astype(BF16), mv)
    for s in range(spb):
        for h in range(MEM_HEADS):
            r = s * MEM_HEADS + h
            o_ref[s, :, h * MEM_HD:(h + 1) * MEM_HD] = o8[r:r + 1].astype(o_ref.dtype)


def _mem_sample(q_mem, mem_k, mem_v, qnorm):
    bs = q_mem.shape[0]
    width = MEM_HEADS * MEM_HD
    mtok = mem_k.shape[1]
    spb = 2
    assert bs % spb == 0
    row = pl.BlockSpec((spb, 1, width), lambda i: (i, 0, 0))
    mem = pl.BlockSpec((spb, mtok, MEM_HEADS, MEM_HD), lambda i: (i, 0, 0, 0))
    return pl.pallas_call(
        _mem_sample_kernel,
        grid=(bs // spb,),
        in_specs=[row, mem, mem, pl.BlockSpec((1, MEM_HD), lambda i: (0, 0))],
        out_specs=row,
        out_shape=jax.ShapeDtypeStruct((bs, 1, width), BF16),
        compiler_params=_cparams("parallel"),
        name="mem_sample",
    )(q_mem.reshape(bs, 1, width), mem_k, mem_v, qnorm.reshape(1, -1)).reshape(bs, width)


def _split_w_in(w_in):
    offs = np.cumsum((0,) + IN_SIZES)
    piece = lambda i: w_in[:, int(offs[i]):int(offs[i + 1])]
    small = jnp.concatenate([piece(2), piece(5)], axis=1)
    small = jnp.pad(small, ((0, 0), (0, 128 - small.shape[1])))
    return dict(z=piece(0), xbc=piece(1), small=small, q=piece(3), kv=piece(4), qmem=piece(6), gmerge=piece(7))


def _multi_mm_kernel(*refs, transposed):
    n_out = len(transposed)
    x_ref, g_ref = refs[:2]
    w_refs = refs[2:2 + n_out]
    o_refs = refs[2 + n_out:]
    x = x_ref[...]
    ms = jnp.mean(x * x, axis=-1, keepdims=True)
    xn = (x * lax.rsqrt(ms + EPS) * g_ref[...]).astype(BF16)
    for w_ref, o_ref, tr in zip(w_refs, o_refs, transposed):
        if tr:
            o_ref[0] = _dot_nt(w_ref[...], xn).astype(o_ref.dtype)
        else:
            o_ref[...] = _dot(xn, w_ref[...]).astype(o_ref.dtype)


def _multi_matmul(x, g, weights, transposed, seq_len, tm=256):
    m, k = x.shape
    tm = min(tm, m)
    assert m % tm == 0 and (not any(transposed) or seq_len % tm == 0)
    nl = seq_len // tm if any(transposed) else 1
    const = lambda shape: pl.BlockSpec(shape, lambda i: (0, 0), pipeline_mode=pl.Buffered(1))
    out_specs, out_shape = [], []
    for wm, tr in zip(weights, transposed):
        if tr:
            out_specs.append(pl.BlockSpec((1, wm.shape[0], tm), lambda i: (i // nl, 0, i % nl)))
            out_shape.append(jax.ShapeDtypeStruct((m // seq_len, wm.shape[0], seq_len), F32))
        else:
            out_specs.append(pl.BlockSpec((tm, wm.shape[1]), lambda i: (i, 0)))
            out_shape.append(jax.ShapeDtypeStruct((m, wm.shape[1]), F32))
    return pl.pallas_call(
        functools.partial(_multi_mm_kernel, transposed=tuple(transposed)),
        grid=(m // tm,),
        in_specs=[pl.BlockSpec((tm, k), lambda i: (i, 0)), const((1, k))] + [const(wm.shape) for wm in weights],
        out_specs=out_specs,
        out_shape=out_shape,
        compiler_params=_cparams("parallel"),
        name="in_proj",
    )(x, g.reshape(1, k).astype(F32), *weights)


def _in_proj(x, norm_w, wi, feature_major=(), seq_len=1):
    names = list(wi)
    weights = [wi[n].T if n in feature_major else wi[n] for n in names]
    proj = dict(zip(names, _multi_matmul(x, norm_w, weights, [n in feature_major for n in names], seq_len)))
    proj['dt'] = proj['small'][:, :SSD_HEADS]
    proj['gnsa'] = proj['small'][:, SSD_HEADS:SSD_HEADS + 3 * NSA_HEADS]
    return proj


def _prompt_layer(x_prompt, mem_prompt, w, wb, wi):
    b, length, _ = x_prompt.shape
    n = b * length
    kvw = NSA_KV * NSA_HD
    x = x_prompt.reshape(n, D_MODEL)
    proj = _in_proj(x, w['norm_mix'], wi, feature_major=('q', 'kv'), seq_len=length)
    xbc = proj['xbc'].reshape(b, length, SSD_CONV_DIM)
    y_ssd, ssd_state = _ssd_prompt(xbc, proj['z'].reshape(b, length, D_INNER),
                                   proj['dt'].reshape(b, length, SSD_HEADS), w)
    kvt = proj['kv']
    qt, skaug, skf, svt, wkr, wkf, wvt = _nsa_prep(proj['q'], kvt, w)
    kc, vc = _compress_prompt(kvt, w)
    heads = lambda t: t.reshape(b, -1, NSA_KV, NSA_HD).astype(BF16)
    t4 = lambda t: t.reshape(b, NSA_KV, NSA_HD, -1)
    gates_t = proj['gnsa'].reshape(b, length, NSA_KV, 3 * NSA_GQ).transpose(0, 2, 3, 1)
    o_nsa = _nsa_prompt(qt, heads(kc).transpose(0, 2, 1, 3), heads(vc).transpose(0, 2, 3, 1), skaug, t4(svt), wkr,
                        t4(wvt), gates_t)
    mem = mem_prompt.reshape(-1, D_MODEL)
    mkv, = _multi_matmul(mem, w['mem_norm'], [wb['w_mem_kv']], [False], 1)
    mtok = mem_prompt.shape[1]
    mk = mkv[:, :MEM_HEADS * MEM_HD].reshape(b, mtok, MEM_HEADS, MEM_HD)
    mk = mk * lax.rsqrt(jnp.mean(mk * mk, axis=-1, keepdims=True) + EPS) * w['mem_k_norm']
    mv = mkv[:, MEM_HEADS * MEM_HD:].reshape(b, mtok, MEM_HEADS, MEM_HD)
    o_mem = _mem_prompt(proj['qmem'], mk.reshape(b, mtok, -1).astype(BF16), mv.reshape(b, mtok, -1).astype(BF16),
                        w['mem_q_norm'], b, length)
    x1 = _merge(x, y_ssd.reshape(n, D_INNER), o_nsa, o_mem, proj['gmerge'], wb)
    y, u_last = _ffn_prompt(x1, w, wb, b, length)
    from_t = lambda t: t4(t).transpose(0, 3, 1, 2)
    keep = min(WINDOW, length)
    state = (from_t(kvt[:, 0:kvw]), from_t(kvt[:, kvw:2 * kvw]), from_t(skf), from_t(kvt[:, 3 * kvw:4 * kvw]),
             from_t(wkf[:, :, -keep:]), from_t(kvt[:, 5 * kvw:6 * kvw, -keep:]),
             xbc[:, -(SSD_CONV - 1):], ssd_state, u_last[:, -(FFN_CONV - 1):], mk, mv)
    return y.reshape(b, length, D_MODEL), state


def _sample_layer(x_sample, caches, page_table, w, wb, wi):
    (pool_ck, pool_cv, pool_sk, pool_sv, win_k, win_v, conv_prev, ssd_h0, mem_k, mem_v, ffn_prev) = caches
    bs = x_sample.shape[0]
    kvw = NSA_KV * NSA_HD
    x = x_sample.reshape(bs, D_MODEL)
    proj = _in_proj(x, w['norm_mix'], wi)
    y_ssd, ssd_state = _ssd_step(proj['xbc'], conv_prev, proj['z'], proj['dt'], ssd_h0, w)
    kv = proj['kv']
    kc, vc = _compress_sample(pool_ck, pool_cv, page_table, w)
    o_nsa, sk_new, wk_new, win_k_new, win_v_new = _nsa_sample(
        proj['q'], kv, proj['gnsa'], pool_sk, pool_sv, page_table, kc, vc, win_k, win_v, w)
    o_mem = _mem_sample(proj['qmem'], mem_k, mem_v, w['mem_q_norm'])
    x1 = _merge(x, y_ssd, o_nsa.reshape(bs, -1), o_mem, proj['gmerge'], wb)
    y, u = _ffn_sample(x1, ffn_prev, w, wb)
    r4 = lambda t: t.reshape(bs, -1, NSA_KV, NSA_HD)
    from_t = lambda t: t.reshape(bs, NSA_KV, NSA_HD, -1).transpose(0, 3, 1, 2)
    state = (r4(kv[:, 0:kvw]), r4(kv[:, kvw:2 * kvw]), r4(sk_new), r4(kv[:, 3 * kvw:4 * kvw]),
             from_t(win_k_new), from_t(win_v_new),
             jnp.concatenate([conv_prev[:, 1:], proj['xbc'][:, None]], axis=1), ssd_state,
             jnp.concatenate([ffn_prev[:, 1:], u[:, None]], axis=1))
    return y.reshape(bs, 1, D_MODEL), state


def kernel(x_prompt, x_sample, cache_nsa_cmp_k, cache_nsa_cmp_v, cache_nsa_sel_k, cache_nsa_sel_v, state_nsa_win_k, state_nsa_win_v, state_ssd_conv, state_ssd, cache_mem_k, cache_mem_v, state_ffn_conv, page_table, mem_prompt, norm_mix, w_in, b_merge, ssd_conv_w, ssd_conv_b, ssd_dt_bias, ssd_a_log, ssd_d, ssd_norm, w_ssd_o, nsa_q_norm, nsa_k_norm, cmp_k_pe, cmp_k_w1, cmp_k_w2, cmp_v_pe, cmp_v_w1, cmp_v_w2, w_nsa_o, mem_norm, w_mem_kv, mem_q_norm, mem_k_norm, w_mem_o, w_out, norm_ffn, w_up, ffn_conv_w, ffn_conv_b, w_down):
    weights = dict(norm_mix=norm_mix, w_in=w_in, b_merge=b_merge, ssd_conv_w=ssd_conv_w,
                   ssd_conv_b=ssd_conv_b, ssd_dt_bias=ssd_dt_bias, ssd_a_log=ssd_a_log, ssd_d=ssd_d,
                   ssd_norm=ssd_norm, w_ssd_o=w_ssd_o, nsa_q_norm=nsa_q_norm, nsa_k_norm=nsa_k_norm,
                   cmp_k_pe=cmp_k_pe, cmp_k_w1=cmp_k_w1, cmp_k_w2=cmp_k_w2, cmp_v_pe=cmp_v_pe,
                   cmp_v_w1=cmp_v_w1, cmp_v_w2=cmp_v_w2, w_nsa_o=w_nsa_o, mem_norm=mem_norm,
                   w_mem_kv=w_mem_kv, mem_q_norm=mem_q_norm, mem_k_norm=mem_k_norm, w_mem_o=w_mem_o,
                   w_out=w_out, norm_ffn=norm_ffn, w_up=w_up, ffn_conv_w=ffn_conv_w,
                   ffn_conv_b=ffn_conv_b, w_down=w_down)
    w = {name: arr[0] for name, arr in weights.items()}
    wb = dict(w)
    for name in ('w_ssd_o', 'w_nsa_o', 'w_mem_o', 'w_out', 'w_up', 'w_down', 'w_mem_kv'):
        wb[name] = w[name].astype(BF16)
    wi = {name: piece.astype(BF16) for name, piece in _split_w_in(w['w_in']).items()}
    y_p, st_p = _prompt_layer(x_prompt, mem_prompt, w, wb, wi)
    caches = (cache_nsa_cmp_k[0], cache_nsa_cmp_v[0], cache_nsa_sel_k[0], cache_nsa_sel_v[0], state_nsa_win_k[0],
              state_nsa_win_v[0], state_ssd_conv[0], state_ssd[0], cache_mem_k[0], cache_mem_v[0], state_ffn_conv[0])
    y_s, st_s = _sample_layer(x_sample, caches, page_table, w, wb, wi)
    return (y_p, y_s) + tuple(s[None] for s in st_p) + tuple(s[None] for s in st_s)
```

```python
import functools
import math

import numpy as np
import jax
import jax.numpy as jnp
from jax import lax
from jax.experimental import pallas as pl
from jax.experimental.pallas import tpu as pltpu

D_MODEL = 1024
D_INNER = 2048
SSD_HEAD_DIM = 64
SSD_HEADS = 32
SSD_GROUPS = 4
SSD_HPG = 8
SSD_STATE = 128
SSD_CONV = 4
SSD_CONV_DIM = 3072
SSD_CHUNK = 128
NSA_HEADS = 16
NSA_KV = 4
NSA_GQ = 4
NSA_HD = 64
CMP_LEN = 32
CMP_STRIDE = 16
CMP_HIDDEN = 128
SEL_BLOCK = 64
SEL_TOPN = 16
SEL_FORCE = 1.0e4
WINDOW = 512
QUERY_BLOCK = 512
MEM_HEADS = 4
MEM_HD = 256
D_FF = 2816
FFN_CONV = 3
ROPE_THETA = 10000.0
EPS = 1e-6
IN_SIZES = (2048, 3072, 32, 1024, 1536, 48, 1024, 3072)

VMEM_LIMIT = 48 * 1024 * 1024
MASK_BIG = 2.0 ** 20
NEG = -1.0e30

F32 = jnp.float32
BF16 = jnp.bfloat16
HI = lax.Precision.HIGHEST


def _cparams(*sem):
    return pltpu.CompilerParams(dimension_semantics=sem, vmem_limit_bytes=VMEM_LIMIT)


def _sigmoid(x):
    return 1.0 / (1.0 + jnp.exp(-x))


def _silu(x):
    return x * _sigmoid(x)


def _dot(a, b):
    return jnp.dot(a, b, preferred_element_type=F32)


def _dot_nt(a, b):
    return lax.dot_general(a, b, (((1,), (1,)), ((), ())), preferred_element_type=F32)


def _dot_hi(a, b):
    return jnp.dot(a, b, preferred_element_type=F32, precision=HI)


def _split3(x):
    hi = x.astype(BF16)
    r1 = x - hi.astype(F32)
    mid = r1.astype(BF16)
    lo = (r1 - mid.astype(F32)).astype(BF16)
    return hi, mid, lo


def _dot_x3(a, b, split):
    if split == 0:
        parts = [_dot(t, b) for t in _split3(a)]
    else:
        parts = [_dot(a, t) for t in _split3(b)]
    return parts[0] + parts[1] + parts[2]


def _shift_rows(x, k, prev, row):
    r = pltpu.roll(x, k, axis=0)
    head = r[:8]
    for i in range(k):
        head = jnp.where(row == i, prev[8 - k + i:8 - k + i + 1, :], head)
    return jnp.concatenate([head, r[8:]], axis=0)


def _ssd_prompt_kernel(xbc_ref, z_ref, dt_ref, dtT_ref, cw_ref, cb_ref, dtb_ref, dtbT_ref, a_ref, aT_ref,
                       dskip_ref, nw_ref, e_ref, y_ref, hT_ref, carry_ref, h_ref, yacc_ref):
    c = pl.program_id(1)
    q = SSD_CHUNK

    @pl.when(c == 0)
    def _():
        carry_ref[...] = jnp.zeros_like(carry_ref)
        h_ref[...] = jnp.zeros_like(h_ref)

    xbc = xbc_ref[0]
    row = lax.broadcasted_iota(jnp.int32, (8, 1), 0)
    prev = carry_ref[...]
    conv = cb_ref[...] + cw_ref[3:4, :] * xbc
    for k in range(1, SSD_CONV):
        conv = conv + cw_ref[3 - k:4 - k, :] * _shift_rows(xbc, k, prev, row)
    carry_ref[...] = xbc[q - 8:, :]
    xc = _silu(conv)
    xs = xc[:, :D_INNER]
    bm = xc[:, D_INNER:D_INNER + SSD_GROUPS * SSD_STATE].astype(BF16)
    cm = xc[:, D_INNER + SSD_GROUPS * SSD_STATE:].astype(BF16)

    def softplus(v):
        return jnp.maximum(v, 0.0) + jnp.log(1.0 + jnp.exp(-jnp.abs(v)))

    dt = softplus(dt_ref[0] + dtb_ref[...])
    dtT = softplus(dtT_ref[0] + dtbT_ref[...])
    ii = lax.broadcasted_iota(jnp.int32, (q, q), 0)
    jj = lax.broadcasted_iota(jnp.int32, (q, q), 1)
    causal = ii >= jj
    cum = _dot_x3(jnp.where(causal, 1.0, 0.0).astype(BF16), dt * a_ref[...], 1)
    cumT = _dot_x3(dtT * aT_ref[...], jnp.where(jj >= ii, 1.0, 0.0).astype(BF16), 0)
    ecum = jnp.exp(cum)
    dend = jnp.exp(cum[q - 1:q, :] - cum)
    stacked = jnp.concatenate([dt, ecum, dend], axis=0)
    hi = stacked.astype(BF16)
    lo = (stacked - hi.astype(F32)).astype(BF16)
    spread = _dot(hi, e_ref[...]) + _dot(lo, e_ref[...])
    dt_x, ecum_x, dend_x = spread[:q], spread[q:2 * q], spread[2 * q:]
    xdt = xs * dt_x
    xdt_b = xdt.astype(BF16)
    xw_b = (xdt * dend_x).astype(BF16)
    gw = SSD_HPG * SSD_HEAD_DIM
    for g in range(SSD_GROUPS):
        bg = bm[:, g * SSD_STATE:(g + 1) * SSD_STATE]
        cg = cm[:, g * SSD_STATE:(g + 1) * SSD_STATE]
        cb = _dot_nt(cg, bg)
        h_prev = h_ref[g]
        yoff = _dot(cg, h_prev.astype(BF16)) * ecum_x[:, g * gw:(g + 1) * gw]
        st = _dot(bg.astype(F32).T.astype(BF16), xw_b[:, g * gw:(g + 1) * gw])
        h_ref[g] = h_prev * ecum_x[q - 1:q, g * gw:(g + 1) * gw] + st
        for eh in range(SSD_HPG):
            hh = g * SSD_HPG + eh
            seg = cum[:, hh:hh + 1] - cumT[hh:hh + 1, :]
            decay = jnp.exp(jnp.where(causal, seg, NEG))
            mm = (cb * decay).astype(BF16)
            lo = hh * SSD_HEAD_DIM
            yd = _dot(mm, xdt_b[:, lo:lo + SSD_HEAD_DIM])
            yacc_ref[:, lo:lo + SSD_HEAD_DIM] = yd + yoff[:, eh * SSD_HEAD_DIM:(eh + 1) * SSD_HEAD_DIM]
    y = yacc_ref[...] + dskip_ref[...] * xs
    yz = y * _silu(z_ref[0])
    ms = jnp.mean(yz * yz, axis=-1, keepdims=True)
    y_ref[0] = (yz * lax.rsqrt(ms + EPS) * nw_ref[...]).astype(y_ref.dtype)

    @pl.when(c == pl.num_programs(1) - 1)
    def _():
        hT_ref[0] = h_ref[...]


def _head_expand(width=SSD_HEAD_DIM):
    e = np.zeros((SSD_HEADS, SSD_HEADS * width), np.float32)
    for h in range(SSD_HEADS):
        e[h, h * width:(h + 1) * width] = 1.0
    return jnp.asarray(e, BF16)


def _ssd_prompt(xbc, z, dt_raw, w):
    b, length, _ = xbc.shape
    q = SSD_CHUNK
    nc = length // q
    dtT = jnp.swapaxes(dt_raw, 1, 2)
    a = -jnp.exp(w['ssd_a_log'].astype(F32))
    full = lambda shape: pl.BlockSpec(shape, lambda i, j: (0,) * len(shape))
    y, hT = pl.pallas_call(
        _ssd_prompt_kernel,
        grid=(b, nc),
        in_specs=[pl.BlockSpec((1, q, SSD_CONV_DIM), lambda i, j: (i, j, 0)),
                  pl.BlockSpec((1, q, D_INNER), lambda i, j: (i, j, 0)),
                  pl.BlockSpec((1, q, SSD_HEADS), lambda i, j: (i, j, 0)),
                  pl.BlockSpec((1, SSD_HEADS, q), lambda i, j: (i, 0, j)),
                  full((SSD_CONV, SSD_CONV_DIM)), full((1, SSD_CONV_DIM)),
                  full((1, SSD_HEADS)), full((SSD_HEADS, 1)), full((1, SSD_HEADS)), full((SSD_HEADS, 1)),
                  full((1, D_INNER)), full((1, D_INNER)), full((SSD_HEADS, D_INNER))],
        out_specs=[pl.BlockSpec((1, q, D_INNER), lambda i, j: (i, j, 0)),
                   pl.BlockSpec((1, SSD_GROUPS, SSD_STATE, SSD_HPG * SSD_HEAD_DIM), lambda i, j: (i, 0, 0, 0))],
        out_shape=[jax.ShapeDtypeStruct((b, length, D_INNER), BF16),
                   jax.ShapeDtypeStruct((b, SSD_GROUPS, SSD_STATE, SSD_HPG * SSD_HEAD_DIM), F32)],
        scratch_shapes=[pltpu.VMEM((8, SSD_CONV_DIM), F32),
                        pltpu.VMEM((SSD_GROUPS, SSD_STATE, SSD_HPG * SSD_HEAD_DIM), F32),
                        pltpu.VMEM((q, D_INNER), F32)],
        compiler_params=_cparams("parallel", "arbitrary"),
        name="ssd_prompt",
    )(xbc, z, dt_raw, dtT, w['ssd_conv_w'], w['ssd_conv_b'].reshape(1, -1),
      w['ssd_dt_bias'].reshape(1, -1), w['ssd_dt_bias'].reshape(-1, 1), a.reshape(1, -1), a.reshape(-1, 1),
      jnp.repeat(w['ssd_d'].astype(F32), SSD_HEAD_DIM).reshape(1, -1), w['ssd_norm'].reshape(1, -1),
      _head_expand())
    state = hT.reshape(b, SSD_GROUPS, SSD_STATE, SSD_HPG, SSD_HEAD_DIM).transpose(0, 1, 3, 4, 2)
    return y, state.reshape(b, SSD_HEADS, SSD_HEAD_DIM, SSD_STATE)


def _segment_ones():
    i = np.arange(128)
    return jnp.asarray((i[:, None] // NSA_HD == i[None, :] // NSA_HD).astype(np.float32), BF16)


def _head_rmsnorm(x, g_row, segm):
    sq = x * x
    hi = sq.astype(BF16)
    lo = (sq - hi.astype(F32)).astype(BF16)
    parts = []
    for c in range(x.shape[1] // 128):
        sl = slice(c * 128, (c + 1) * 128)
        parts.append(_dot(hi[:, sl], segm) + _dot(lo[:, sl], segm))
    ss = parts[0] if len(parts) == 1 else jnp.concatenate(parts, axis=1)
    return x * lax.rsqrt(ss * (1.0 / NSA_HD) + EPS) * g_row


def _tile_lanes(t, width):
    reps = width // t.shape[1]
    return t if reps == 1 else jnp.concatenate([t] * reps, axis=1)


def _rope(x, cos128, sin128):
    width = x.shape[1]
    lane = lax.broadcasted_iota(jnp.int32, x.shape, 1)
    first = (lane & (NSA_HD // 2)) == 0
    rot = jnp.where(first, pltpu.roll(x, width - NSA_HD // 2, axis=1), pltpu.roll(x, NSA_HD // 2, axis=1))
    return x * _tile_lanes(cos128, width) + rot * _tile_lanes(sin128, width)


def _rope_tables(pos):
    half = NSA_HD // 2
    inv = ROPE_THETA ** (-jnp.arange(half, dtype=F32) / half)
    ang = pos.astype(F32)[:, None] * inv[None, :]
    cos, sin = jnp.cos(ang), jnp.sin(ang)
    cos128 = jnp.concatenate([cos, cos, cos, cos], axis=1)
    sin128 = jnp.concatenate([-sin, sin, -sin, sin], axis=1)
    return cos128, sin128


def _norm_rope_t(x, g_col, cos, sin):
    nh = x.shape[0] // NSA_HD
    x3 = x.reshape(nh, NSA_HD, x.shape[1])
    ms = jnp.sum(x3 * x3, axis=1, keepdims=True) * (1.0 / NSA_HD)
    x3 = x3 * lax.rsqrt(ms + EPS) * g_col[None]
    half = NSA_HD // 2
    x1, x2 = x3[:, :half], x3[:, half:]
    out = jnp.concatenate([x1 * cos[None] - x2 * sin[None], x2 * cos[None] + x1 * sin[None]], axis=1)
    return out.reshape(x.shape)


def _nsa_prep_kernel(q_ref, kv_ref, cos_ref, sin_ref, qg_ref, kg_ref,
                     qt_ref, skaug_ref, skf_ref, svt_ref, wkr_ref, wkf_ref, wvt_ref):
    tl = q_ref.shape[2]
    i = pl.program_id(1)
    cos, sin = cos_ref[...], sin_ref[...]
    kvw = NSA_KV * NSA_HD
    qt_ref[0] = (_norm_rope_t(q_ref[0], qg_ref[...], cos, sin) * (NSA_HD ** -0.5 * math.log2(math.e))).astype(BF16)
    sk = _norm_rope_t(kv_ref[0, 2 * kvw:3 * kvw, :], kg_ref[:, 1:2], cos, sin)
    wk = _norm_rope_t(kv_ref[0, 4 * kvw:5 * kvw, :], kg_ref[:, 2:3], cos, sin)
    skf_ref[0] = sk
    wkf_ref[0] = wk
    svt_ref[0] = kv_ref[0, 3 * kvw:4 * kvw, :].astype(BF16)
    wvt_ref[0] = kv_ref[0, 5 * kvw:6 * kvw, :].astype(BF16)
    pos = i * tl + lax.broadcasted_iota(jnp.int32, (tl, NSA_HD), 0)
    blk = lax.broadcasted_iota(jnp.int32, (tl, NSA_HD), 1)
    onehot = jnp.where((pos >> 6) == blk, MASK_BIG, 0.0).astype(BF16)
    for k in range(NSA_KV):
        sl = slice(k * NSA_HD, (k + 1) * NSA_HD)
        skaug_ref[0, k] = jnp.concatenate([sk[sl].T.astype(BF16), onehot], axis=1)
        wkr_ref[0, k] = wk[sl].T.astype(BF16)


def _nsa_prep(qt, kvt, w, tl=256):
    b, _, length = qt.shape
    nl = length // tl
    half = NSA_HD // 2
    inv = ROPE_THETA ** (-jnp.arange(half, dtype=F32) / half)
    ang = inv[:, None] * jnp.arange(length, dtype=F32)[None, :]
    kvw = NSA_KV * NSA_HD
    fm = lambda rows: pl.BlockSpec((1, rows, tl), lambda bi, i: (bi, 0, i))
    tab = pl.BlockSpec((half, tl), lambda bi, i: (0, i))
    full = lambda shape: pl.BlockSpec(shape, lambda bi, i: (0,) * len(shape))
    rows = lambda wd: pl.BlockSpec((1, NSA_KV, tl, wd), lambda bi, i: (bi, 0, i, 0))
    return pl.pallas_call(
        _nsa_prep_kernel,
        grid=(b, nl),
        in_specs=[fm(NSA_HEADS * NSA_HD), fm(6 * kvw), tab, tab, full((NSA_HD, 1)), full((NSA_HD, 3))],
        out_specs=[fm(NSA_HEADS * NSA_HD), rows(2 * NSA_HD), fm(kvw), fm(kvw), rows(NSA_HD), fm(kvw), fm(kvw)],
        out_shape=[jax.ShapeDtypeStruct((b, NSA_HEADS * NSA_HD, length), BF16),
                   jax.ShapeDtypeStruct((b, NSA_KV, length, 2 * NSA_HD), BF16),
                   jax.ShapeDtypeStruct((b, kvw, length), F32),
                   jax.ShapeDtypeStruct((b, kvw, length), BF16),
                   jax.ShapeDtypeStruct((b, NSA_KV, length, NSA_HD), BF16),
                   jax.ShapeDtypeStruct((b, kvw, length), F32),
                   jax.ShapeDtypeStruct((b, kvw, length), BF16)],
        compiler_params=_cparams("parallel", "parallel"),
        name="nsa_prep",
    )(qt, kvt, jnp.cos(ang), jnp.sin(ang), w['nsa_q_norm'].reshape(-1, 1), w['nsa_k_norm'].T)


def _compress_kernel(*refs, n_x, n_prefetch):
    refs = refs[n_prefetch:]
    (wk_ref, pek_ref, w2k_ref, wv_ref, pev_ref, w2v_ref, g_ref, cos_ref, sin_ref, segm_ref, ok_ref, ov_ref,
     x0k_ref, x1k_ref, x0v_ref, x1v_ref) = refs[2 * n_x:]
    shared = (g_ref, cos_ref, sin_ref, segm_ref)
    _compress_one(refs[:n_x], wk_ref, pek_ref, w2k_ref, shared, ok_ref, x0k_ref, x1k_ref, True)
    _compress_one(refs[n_x:2 * n_x], wv_ref, pev_ref, w2v_ref, shared, ov_ref, x0v_ref, x1v_ref, False)


def _compress_one(page_refs, wbd_ref, pe_ref, w2_ref, shared, o_ref, x0_ref, x1_ref, is_k):
    g_ref, cos_ref, sin_ref, segm_ref = shared
    plen = page_refs[0].shape[-1]
    for j, r in enumerate(page_refs):
        rows = r[...].T
        x0_ref[j * plen:(j + 1) * plen, :] = rows[:, :128]
        x1_ref[j * plen:(j + 1) * plen, :] = rows[:, 128:]
    nseq = o_ref.shape[0]
    ntot = x0_ref.shape[0] // CMP_STRIDE
    nchunk = ntot // nseq
    row = lax.broadcasted_iota(jnp.int32, (ntot, 1), 0)
    outs = []
    for x_ref in (x0_ref, x1_ref):
        x = jnp.concatenate([x_ref[pl.ds(s, ntot, stride=CMP_STRIDE), :].astype(BF16)
                             for s in range(CMP_STRIDE)], axis=1)
        acc = _dot(x, wbd_ref[...])
        hid = []
        for kl in range(2):
            lo = kl * 2 * CMP_HIDDEN
            first = acc[:, lo:lo + CMP_HIDDEN]
            second = pltpu.roll(acc[:, lo + CMP_HIDDEN:lo + 2 * CMP_HIDDEN], ntot - 1, axis=0)
            hid.append(_silu(first + second + pe_ref[...]))
        outs.append(_dot(jnp.concatenate(hid, axis=1).astype(BF16), w2_ref[...]))
    out = jnp.concatenate(outs, axis=1)
    if is_k:
        tile = lambda t: t if nseq == 1 else jnp.concatenate([t] * nseq, axis=0)
        out = _rope(_head_rmsnorm(out, g_ref[...], segm_ref[...]), tile(cos_ref[...]), tile(sin_ref[...]))
    out = jnp.where((row & (nchunk - 1)) < nchunk - 1, out, 0.0)
    for s in range(nseq):
        o_ref[s] = out[s * nchunk:(s + 1) * nchunk]


def _compress_weights(pe, w1, w2):
    w1r = w1.reshape(CMP_LEN, NSA_HD, CMP_HIDDEN)
    pe_term = jnp.einsum('ld,ldm->m', pe, w1r, precision=HI).reshape(1, CMP_HIDDEN)
    both = jnp.concatenate([w1r[:CMP_STRIDE], w1r[CMP_STRIDE:]], axis=-1)
    zero = jnp.zeros_like(both)
    wbd = jnp.concatenate([jnp.concatenate([both, zero], axis=-1), jnp.concatenate([zero, both], axis=-1)], axis=1)
    zero2 = jnp.zeros_like(w2)
    w2bd = jnp.concatenate([jnp.concatenate([w2, zero2], axis=-1), jnp.concatenate([zero2, w2], axis=-1)], axis=0)
    wbd = wbd.reshape(CMP_STRIDE * 2 * NSA_HD, 4 * CMP_HIDDEN)
    return wbd.astype(BF16), pe_term, w2bd.astype(BF16)


def _compress_call(x_args, x_specs, grid, nchunk, batch, w, n_prefetch=0, prefetch=(), spb=1):
    assert nchunk & (nchunk - 1) == 0
    wk = _compress_weights(w['cmp_k_pe'], w['cmp_k_w1'], w['cmp_k_w2'])
    wv = _compress_weights(w['cmp_v_pe'], w['cmp_v_w1'], w['cmp_v_w2'])
    cos128, sin128 = _rope_tables(jnp.arange(nchunk, dtype=jnp.int32) * CMP_STRIDE + (CMP_LEN - 1))
    kvw = NSA_KV * NSA_HD
    full = lambda shape: pl.BlockSpec(shape, lambda *a: (0,) * len(shape))
    wspecs = [full(wk[0].shape), full((1, CMP_HIDDEN)), full(wk[2].shape)]
    out = pl.BlockSpec((spb, nchunk, kvw), lambda i, *a: (i, 0, 0))
    gs = pltpu.PrefetchScalarGridSpec(
        num_scalar_prefetch=n_prefetch, grid=grid,
        in_specs=list(x_specs) + wspecs + wspecs + [full((1, kvw)), full((nchunk, 128)), full((nchunk, 128)),
                                                    full((128, 128))],
        out_specs=[out, out],
        scratch_shapes=[pltpu.VMEM((spb * nchunk * CMP_STRIDE, 128), F32)] * 4)
    return pl.pallas_call(
        functools.partial(_compress_kernel, n_x=len(x_specs) // 2, n_prefetch=n_prefetch),
        grid_spec=gs,
        out_shape=[jax.ShapeDtypeStruct((batch, nchunk, kvw), F32)] * 2,
        compiler_params=_cparams("parallel"),
        name="compress",
    )(*prefetch, *x_args, *wk, *wv, jnp.tile(w['nsa_k_norm'][0], NSA_KV).reshape(1, -1), cos128, sin128,
      _segment_ones())


def _compress_prompt(kvt, w, page=128):
    b, _, length = kvt.shape
    npages = length // page
    specs = [pl.BlockSpec((None, NSA_KV * NSA_HD, page), functools.partial(lambda i, c, j: (i, c, j), c=c, j=j))
             for c in range(2) for j in range(npages)]
    return _compress_call([kvt] * len(specs), specs, (b,), length // CMP_STRIDE, b, w)


SEL_TILE = 512


def _flash_step(carry, s, vt):
    m, l, acc = carry
    m_new = jnp.maximum(m, jnp.max(s, axis=0, keepdims=True))
    alpha = jnp.exp2(m - m_new)
    p = jnp.exp2(s - m_new)
    l = alpha * l + jnp.sum(p, axis=0, keepdims=True)
    acc = alpha * acc + _dot(vt, p.astype(BF16))
    return m_new, l, acc


def _topn_bias(score, n):
    nblk, nq = score.shape
    groups = [score[8 * v:8 * v + 8] for v in range(nblk // 8)]
    sub = lax.broadcasted_iota(jnp.int32, (8, nq), 0)
    cnt = [jnp.zeros((8, nq), F32) for _ in groups]
    for jp in range(nblk):
        row = score[jp:jp + 1]
        for v, grp in enumerate(groups):
            if v < jp // 8:
                inc = jnp.where(row > grp, 1.0, 0.0)
            elif v > jp // 8:
                inc = jnp.where(row >= grp, 1.0, 0.0)
            else:
                inc = jnp.where(sub > jp % 8, jnp.where(row >= grp, 1.0, 0.0), jnp.where(row > grp, 1.0, 0.0))
            cnt[v] = cnt[v] + inc
    return jnp.concatenate([jnp.where(c < n, 0.0, -1.0) for c in cnt], axis=0)


def _nsa_prompt_kernel(q_ref, kc_ref, vc_ref, sk_ref, sv_ref, wk_ref, wv_ref, g_ref, ov_ref, o_ref):
    qb = pl.program_id(2)
    qlen = QUERY_BLOCK
    cols = NSA_GQ * qlen
    t0 = qb * qlen
    qt = jnp.concatenate([q_ref[0, g * NSA_HD:(g + 1) * NSA_HD, :] for g in range(NSA_GQ)], axis=1)
    tcol = t0 + (lax.broadcasted_iota(jnp.int32, (1, cols), 1) & (qlen - 1))
    ncmp = kc_ref.shape[2]
    s = _dot(kc_ref[0, 0], qt)
    kend = lax.broadcasted_iota(jnp.int32, (ncmp, 1), 0) * CMP_STRIDE + (CMP_LEN - 1)
    valid = kend <= tcol
    s = jnp.where(valid, s, NEG)
    m = jnp.max(s, axis=0, keepdims=True)
    p = jnp.where(valid, jnp.exp2(s - m), 0.0)
    den = jnp.sum(p, axis=0, keepdims=True)
    p = p * (1.0 / jnp.where(den > 0, den, 1.0))
    o_c = _dot(vc_ref[0, 0], p.astype(BF16))
    psum = p[:, 0:qlen] + p[:, qlen:2 * qlen] + p[:, 2 * qlen:3 * qlen] + p[:, 3 * qlen:4 * qlen]
    nblk = ov_ref.shape[0]
    imp = _dot_hi(ov_ref[...], psum)
    jb = lax.broadcasted_iota(jnp.int32, (nblk, 1), 0)
    t = t0 + lax.broadcasted_iota(jnp.int32, (1, qlen), 1)
    cur = t >> 6
    forced = (jb == 0) | (jb == cur) | (jb == cur - 1)
    score = jnp.where(jb * SEL_BLOCK <= t, imp + jnp.where(forced, SEL_FORCE, 0.0), NEG)
    bias = _topn_bias(score, min(SEL_TOPN, nblk)).astype(BF16)
    qaug = jnp.concatenate([qt, jnp.concatenate([bias] * NSA_GQ, axis=1)], axis=0)
    init = (jnp.full((1, cols), NEG, F32), jnp.zeros((1, cols), F32), jnp.zeros((NSA_HD, cols), F32))

    def sel_scores(k0):
        return _dot(sk_ref[0, 0, pl.ds(k0, SEL_TILE), :], qaug)

    def sel_pair(kp, carry):
        m, l, acc = carry
        k0 = pl.multiple_of(kp * (2 * SEL_TILE), 2 * SEL_TILE)
        k1 = pl.multiple_of(k0 + SEL_TILE, SEL_TILE)
        s0, s1 = sel_scores(k0), sel_scores(k1)
        m_new = jnp.maximum(m, jnp.maximum(jnp.max(s0, axis=0, keepdims=True), jnp.max(s1, axis=0, keepdims=True)))
        alpha = jnp.exp2(m - m_new)
        p0, p1 = jnp.exp2(s0 - m_new), jnp.exp2(s1 - m_new)
        l = alpha * l + jnp.sum(p0, axis=0, keepdims=True) + jnp.sum(p1, axis=0, keepdims=True)
        acc = (alpha * acc + _dot(sv_ref[0, 0, :, pl.ds(k0, SEL_TILE)], p0.astype(BF16))
               + _dot(sv_ref[0, 0, :, pl.ds(k1, SEL_TILE)], p1.astype(BF16)))
        return m_new, l, acc

    def sel_single(kt, carry):
        k0 = pl.multiple_of(kt * SEL_TILE, SEL_TILE)
        return _flash_step(carry, sel_scores(k0), sv_ref[0, 0, :, pl.ds(k0, SEL_TILE)])

    n_full = t0 // SEL_TILE
    carry = lax.fori_loop(0, n_full // 2, sel_pair, init)
    carry = lax.fori_loop(n_full - n_full % 2, n_full, sel_single, carry)
    k0 = pl.multiple_of(n_full * SEL_TILE, SEL_TILE)

    kpos = k0 + lax.broadcasted_iota(jnp.int32, (SEL_TILE, 1), 0)
    sc = jnp.where(kpos <= tcol, sel_scores(k0), NEG)
    _, l_s, acc_s = _flash_step(carry, sc, sv_ref[0, 0, :, pl.ds(k0, SEL_TILE)])
    o_s = acc_s * (1.0 / l_s)

    span = min(WINDOW + qlen, wk_ref.shape[2])
    kw0 = pl.multiple_of(jnp.maximum(t0 - WINDOW, 0), qlen)
    kpos = kw0 + lax.broadcasted_iota(jnp.int32, (span, 1), 0)
    s_w = _dot(wk_ref[0, 0, pl.ds(kw0, span), :], qt)
    s_w = jnp.where(kpos <= tcol, jnp.where(kpos >= tcol - WINDOW, s_w, NEG), NEG)
    p_w = jnp.exp2(s_w - jnp.max(s_w, axis=0, keepdims=True))
    o_w = _dot(wv_ref[0, 0, :, pl.ds(kw0, span)], p_w.astype(BF16)) * (1.0 / jnp.sum(p_w, axis=0, keepdims=True))

    gate = _sigmoid(g_ref[0, 0])
    outs = []
    for g in range(NSA_GQ):
        c = slice(g * qlen, (g + 1) * qlen)
        outs.append(gate[3 * g:3 * g + 1] * o_c[:, c] + gate[3 * g + 1:3 * g + 2] * o_s[:, c]
                    + gate[3 * g + 2:3 * g + 3] * o_w[:, c])
    o_ref[...] = jnp.concatenate(outs, axis=0).T.astype(o_ref.dtype)


def _overlap_matrix(ncmp, nblk):
    ci = np.arange(ncmp)[:, None] * CMP_STRIDE
    sj = np.arange(nblk)[None, :] * SEL_BLOCK
    return jnp.asarray(((ci <= sj + SEL_BLOCK - 1) & (ci + CMP_LEN - 1 >= sj)).astype(np.float32))


def _nsa_prompt(qt, kc, vct, skaug, svt, wkr, wvt, gates_t):
    b, _, length = qt.shape
    nb = length // QUERY_BLOCK
    ncmp = kc.shape[2]
    nblk = NSA_HD
    assert length // SEL_BLOCK <= nblk
    per_kv = lambda r, c: pl.BlockSpec((1, 1, r, c), lambda bi, k, i: (bi, k, 0, 0))
    return pl.pallas_call(
        _nsa_prompt_kernel,
        grid=(b, NSA_KV, nb),
        in_specs=[pl.BlockSpec((1, NSA_GQ * NSA_HD, QUERY_BLOCK), lambda bi, k, i: (bi, k, i)),
                  per_kv(ncmp, NSA_HD), per_kv(NSA_HD, ncmp), per_kv(length, 2 * NSA_HD), per_kv(NSA_HD, length),
                  per_kv(length, NSA_HD), per_kv(NSA_HD, length),
                  pl.BlockSpec((1, 1, 3 * NSA_GQ, QUERY_BLOCK), lambda bi, k, i: (bi, k, 0, i)),
                  pl.BlockSpec((nblk, ncmp), lambda bi, k, i: (0, 0))],
        out_specs=pl.BlockSpec((QUERY_BLOCK, NSA_GQ * NSA_HD), lambda bi, k, i: (bi * nb + i, k)),
        out_shape=jax.ShapeDtypeStruct((b * length, NSA_HEADS * NSA_HD), BF16),
        compiler_params=_cparams("parallel", "parallel", "arbitrary"),
        name="nsa_prompt",
    )(qt, kc, vct, skaug, svt, wkr, wvt, gates_t, _overlap_matrix(ncmp, nblk).T)


def _mem_prompt_kernel(q_ref, mk_ref, mv_ref, g_ref, o_ref):
    for h in range(MEM_HEADS):
        sl = slice(h * MEM_HD, (h + 1) * MEM_HD)
        q = q_ref[:, sl]
        ms = jnp.mean(q * q, axis=-1, keepdims=True)
        qn = (q * lax.rsqrt(ms + EPS) * g_ref[...] * (MEM_HD ** -0.5)).astype(BF16)
        s = _dot_nt(qn, mk_ref[0, :, sl])
        m = jnp.max(s, axis=-1, keepdims=True)
        p = jnp.exp(s - m)
        p = p / jnp.sum(p, axis=-1, keepdims=True)
        o_ref[:, sl] = _dot(p.astype(BF16), mv_ref[0, :, sl]).astype(o_ref.dtype)


def _mem_prompt(q_mem, mk, mv, qnorm, b, length, tq=256):
    nl = length // tq
    width = MEM_HEADS * MEM_HD
    mem = pl.BlockSpec((1, mk.shape[1], width), lambda bi, i: (bi, 0, 0))
    return pl.pallas_call(
        _mem_prompt_kernel,
        grid=(b, nl),
        in_specs=[pl.BlockSpec((tq, width), lambda bi, i: (bi * nl + i, 0)), mem, mem,
                  pl.BlockSpec((1, MEM_HD), lambda bi, i: (0, 0))],
        out_specs=pl.BlockSpec((tq, width), lambda bi, i: (bi * nl + i, 0)),
        out_shape=jax.ShapeDtypeStruct((b * length, width), BF16),
        compiler_params=_cparams("parallel", "parallel"),
        name="mem_prompt",
    )(q_mem, mk, mv, qnorm.reshape(1, -1))


def _merge_kernel(x_ref, ys_ref, yn_ref, ym_ref, gm_ref, bm_ref, ws_ref, wn_ref, wm_ref, wo_ref, o_ref):
    gate = _sigmoid(gm_ref[...] + bm_ref[...])
    mixed = (gate[:, :D_MODEL] * _dot(ys_ref[...], ws_ref[...])
             + gate[:, D_MODEL:2 * D_MODEL] * _dot(yn_ref[...], wn_ref[...])
             + gate[:, 2 * D_MODEL:] * _dot(ym_ref[...], wm_ref[...]))
    o_ref[...] = x_ref[...] + _dot(mixed.astype(BF16), wo_ref[...])


def _merge(x, ys, yn, ym, gm, wb, tm=256):
    m = x.shape[0]
    tm = min(tm, m)
    row = lambda wd: pl.BlockSpec((tm, wd), lambda i: (i, 0))
    full = lambda shape: pl.BlockSpec(shape, lambda i: (0,) * len(shape))
    return pl.pallas_call(
        _merge_kernel,
        grid=(m // tm,),
        in_specs=[row(D_MODEL), row(D_INNER), row(D_MODEL), row(D_MODEL), row(3 * D_MODEL), full((1, 3 * D_MODEL)),
                  full((D_INNER, D_MODEL)), full((D_MODEL, D_MODEL)), full((D_MODEL, D_MODEL)),
                  full((D_MODEL, D_MODEL))],
        out_specs=row(D_MODEL),
        out_shape=jax.ShapeDtypeStruct((m, D_MODEL), F32),
        compiler_params=_cparams("parallel"),
        name="merge",
    )(x, ys, yn, ym, gm, wb['b_merge'].reshape(1, -1), wb['w_ssd_o'], wb['w_nsa_o'], wb['w_mem_o'], wb['w_out'])


FFN_COLS = 2816


def _ffn_kernel(*refs, seq):
    if seq:
        x_ref, g_ref, wu_ref, cw_ref, cb_ref, wd_ref, o_ref, ul_ref, carry_ref = refs
    else:
        x_ref, g_ref, wu_ref, cw_ref, cb_ref, wd_ref, p0_ref, p1_ref, o_ref, ul_ref = refs
    x = x_ref[...]
    tm = x.shape[0]
    ms = jnp.mean(x * x, axis=-1, keepdims=True)
    xn = (x * lax.rsqrt(ms + EPS) * g_ref[...]).astype(BF16)
    if seq:
        @pl.when(pl.program_id(1) == 0)
        def _():
            carry_ref[...] = jnp.zeros_like(carry_ref)

        row = lax.broadcasted_iota(jnp.int32, (8, 1), 0)

    def conv_cols(lo):
        sl = slice(lo, lo + FFN_COLS)
        u = _dot(xn, wu_ref[:, sl])
        if seq:
            prev = carry_ref[:, sl]
            u1 = _shift_rows(u, 1, prev, row)
            u2 = _shift_rows(u, 2, prev, row)
            carry_ref[:, sl] = u[tm - 8:, :]
        else:
            u2, u1 = p0_ref[:, sl], p1_ref[:, sl]
            ul_ref[:, sl] = u
        return cb_ref[:, sl] + cw_ref[0:1, sl] * u2 + cw_ref[1:2, sl] * u1 + cw_ref[2:3, sl] * u

    acc = x
    for j in range(D_FF // FFN_COLS):
        act = (_silu(conv_cols(j * FFN_COLS)) * conv_cols(D_FF + j * FFN_COLS)).astype(BF16)
        acc = acc + _dot(act, wd_ref[j * FFN_COLS:(j + 1) * FFN_COLS, :])
    o_ref[...] = acc
    if seq:
        @pl.when(pl.program_id(1) == pl.num_programs(1) - 1)
        def _():
            ul_ref[0] = carry_ref[...]


def _ffn_prompt(x1, w, wb, b, length, tm=256):
    nl = length // tm
    row = lambda wd: pl.BlockSpec((tm, wd), lambda bi, i: (bi * nl + i, 0))
    const = lambda shape: pl.BlockSpec(shape, lambda bi, i: (0,) * len(shape), pipeline_mode=pl.Buffered(1))
    return pl.pallas_call(
        functools.partial(_ffn_kernel, seq=True),
        grid=(b, nl),
        in_specs=[row(D_MODEL), const((1, D_MODEL)), const((D_MODEL, 2 * D_FF)), const((FFN_CONV, 2 * D_FF)),
                  const((1, 2 * D_FF)), const((D_FF, D_MODEL))],
        out_specs=[row(D_MODEL), pl.BlockSpec((1, 8, 2 * D_FF), lambda bi, i: (bi, 0, 0))],
        out_shape=[jax.ShapeDtypeStruct((b * length, D_MODEL), F32), jax.ShapeDtypeStruct((b, 8, 2 * D_FF), F32)],
        scratch_shapes=[pltpu.VMEM((8, 2 * D_FF), F32)],
        compiler_params=_cparams("parallel", "arbitrary"),
        name="ffn_prompt",
    )(x1, w['norm_ffn'].reshape(1, -1), wb['w_up'], wb['ffn_conv_w'], wb['ffn_conv_b'].reshape(1, -1), wb['w_down'])


def _ffn_sample(x1, prev, w, wb):
    m = x1.shape[0]
    full = lambda shape: pl.BlockSpec(shape, lambda i: (0,) * len(shape))
    return pl.pallas_call(
        functools.partial(_ffn_kernel, seq=False),
        grid=(1,),
        in_specs=[full((m, D_MODEL)), full((1, D_MODEL)), full((D_MODEL, 2 * D_FF)), full((FFN_CONV, 2 * D_FF)),
                  full((1, 2 * D_FF)), full((D_FF, D_MODEL)), full((m, 2 * D_FF)), full((m, 2 * D_FF))],
        out_specs=[full((m, D_MODEL)), full((m, 2 * D_FF))],
        out_shape=[jax.ShapeDtypeStruct((m, D_MODEL), F32), jax.ShapeDtypeStruct((m, 2 * D_FF), F32)],
        compiler_params=_cparams("arbitrary"),
        name="ffn_sample",
    )(x1, w['norm_ffn'].reshape(1, -1), wb['w_up'], wb['ffn_conv_w'], wb['ffn_conv_b'].reshape(1, -1), wb['w_down'],
      prev[:, 0], prev[:, 1])


def _compress_sample(pool_k, pool_v, page_table, w):
    bs, npages = page_table.shape
    page = pool_k.shape[1]
    spb = 2 if bs % 2 == 0 else 1
    specs = [pl.BlockSpec((None, NSA_KV * NSA_HD, page),
                          functools.partial(lambda b, pt, s, j: (pt[b * spb + s, j], 0, 0), s=s, j=j))
             for s in range(spb) for j in range(npages)]
    return _compress_call([_pool_t(pool_k)] * len(specs) + [_pool_t(pool_v)] * len(specs), specs + specs,
                          (bs // spb,), npages * page // CMP_STRIDE, bs, w, n_prefetch=1, prefetch=(page_table,),
                          spb=spb)


def _pool_t(pool):
    n, page = pool.shape[:2]
    return pool.transpose(0, 2, 3, 1).reshape(n, NSA_KV * NSA_HD, page)


def _rows8(x):
    return jnp.broadcast_to(x, (8, x.shape[1]))


def _ssd_step_kernel(xbc_ref, prev_ref, z_ref, dt_ref, h_ref, cw_ref, cb_ref, dtb_ref, a_ref, dskip_ref, nw_ref,
                     e64_ref, e128_ref, y_ref, hn_ref):
    for s in range(xbc_ref.shape[0]):
        _ssd_step_one(s, xbc_ref, prev_ref, z_ref, dt_ref, h_ref, cw_ref, cb_ref, dtb_ref, a_ref, dskip_ref, nw_ref,
                      e64_ref, e128_ref, y_ref, hn_ref)


def _ssd_step_one(s, xbc_ref, prev_ref, z_ref, dt_ref, h_ref, cw_ref, cb_ref, dtb_ref, a_ref, dskip_ref, nw_ref,
                  e64_ref, e128_ref, y_ref, hn_ref):
    conv = cb_ref[...] + cw_ref[3:4, :] * xbc_ref[s]
    for k in range(SSD_CONV - 1):
        conv = conv + cw_ref[k:k + 1, :] * prev_ref[s, k:k + 1, :]
    xc = _silu(conv)
    xs = xc[:, :D_INNER]
    nb = SSD_GROUPS * SSD_STATE
    bm = xc[:, D_INNER:D_INNER + nb]
    cm = xc[:, D_INNER + nb:]
    v = dt_ref[s] + dtb_ref[...]
    dt = jnp.maximum(v, 0.0) + jnp.log(1.0 + jnp.exp(-jnp.abs(v)))
    dec = jnp.exp(dt * a_ref[...])
    xdt = xs * _dot_x3(_rows8(dt), e64_ref[...], 0)[0:1]
    dec128 = _dot_x3(_rows8(dec), e128_ref[...], 0)
    pieces = []
    for c in range(D_INNER // 128):
        g = c // (SSD_HPG // 2)
        xcol = jnp.broadcast_to(xdt[:, c * 128:(c + 1) * 128], (128, 128)).T
        bg = jnp.broadcast_to(bm[:, g * SSD_STATE:(g + 1) * SSD_STATE], (128, SSD_STATE))
        decv = jnp.concatenate(
            [jnp.concatenate([dec128[:, hh * 128:(hh + 1) * 128]] * (SSD_HEAD_DIM // 8), axis=0)
             for hh in (2 * c, 2 * c + 1)], axis=0)
        hnew = decv * h_ref[s, c * 128:(c + 1) * 128, :] + xcol * bg
        hn_ref[s, c * 128:(c + 1) * 128, :] = hnew
        cg = _rows8(cm[:, g * SSD_STATE:(g + 1) * SSD_STATE]).astype(BF16)
        pieces.append(_dot_nt(cg, hnew.astype(BF16))[0:1])
    y = jnp.concatenate(pieces, axis=1) + dskip_ref[...] * xs
    yz = y * _silu(z_ref[s])
    ms = jnp.mean(yz * yz, axis=-1, keepdims=True)
    y_ref[s] = (yz * lax.rsqrt(ms + EPS) * nw_ref[...]).astype(y_ref.dtype)


def _ssd_step(xbc, prev, z, dt_raw, h0, w):
    bs = xbc.shape[0]
    spb = 2 if bs % 2 == 0 else 1
    a = -jnp.exp(w['ssd_a_log'].astype(F32))
    row = lambda wd: pl.BlockSpec((spb, 1, wd), lambda i: (i, 0, 0))
    full = lambda shape: pl.BlockSpec(shape, lambda i: (0,) * len(shape))
    st = pl.BlockSpec((spb, D_INNER, SSD_STATE), lambda i: (i, 0, 0))
    y, hn = pl.pallas_call(
        _ssd_step_kernel,
        grid=(bs // spb,),
        in_specs=[row(SSD_CONV_DIM), pl.BlockSpec((spb, SSD_CONV - 1, SSD_CONV_DIM), lambda i: (i, 0, 0)),
                  row(D_INNER), row(SSD_HEADS), st,
                  full((SSD_CONV, SSD_CONV_DIM)), full((1, SSD_CONV_DIM)), full((1, SSD_HEADS)), full((1, SSD_HEADS)),
                  full((1, D_INNER)), full((1, D_INNER)), full((SSD_HEADS, D_INNER)),
                  full((SSD_HEADS, SSD_HEADS * 128))],
        out_specs=[row(D_INNER), st],
        out_shape=[jax.ShapeDtypeStruct((bs, 1, D_INNER), BF16),
                   jax.ShapeDtypeStruct((bs, D_INNER, SSD_STATE), F32)],
        compiler_params=_cparams("parallel"),
        name="ssd_step",
    )(xbc.reshape(bs, 1, -1), prev, z.reshape(bs, 1, -1), dt_raw.reshape(bs, 1, -1),
      h0.reshape(bs, D_INNER, SSD_STATE), w['ssd_conv_w'], w['ssd_conv_b'].reshape(1, -1),
      w['ssd_dt_bias'].reshape(1, -1), a.reshape(1, -1),
      jnp.repeat(w['ssd_d'].astype(F32), SSD_HEAD_DIM).reshape(1, -1), w['ssd_norm'].reshape(1, -1),
      _head_expand(), _head_expand(128))
    return y.reshape(bs, D_INNER), hn.reshape(bs, SSD_HEADS, SSD_HEAD_DIM, SSD_STATE)


def _softmax_with_extra(s, s_new, valid=None):
    if valid is not None:
        s = jnp.where(valid, s, NEG)
    m = jnp.maximum(jnp.max(s, axis=-1, keepdims=True), s_new)
    p = jnp.exp(s - m)
    if valid is not None:
        p = jnp.where(valid, p, 0.0)
    p_new = jnp.exp(s_new - m)
    inv = 1.0 / (jnp.sum(p, axis=-1, keepdims=True) + p_new)
    return p * inv, p_new * inv


def _nsa_sample_kernel(*refs, npages, past_len, spb):
    q_ref, kv_ref, gate_ref = refs[1:4]
    skp = refs[4:4 + spb * npages]
    svp = refs[4 + spb * npages:4 + 2 * spb * npages]
    (kc_ref, vc_ref, wk_ref, wv_ref, qg_ref, kg_ref, cos_ref, sin_ref, segm_ref, ov_ref, eblk_ref, fold_ref,
     foldt_ref, o_ref, skn_ref, wkn_ref, wko_ref, wvo_ref) = refs[4 + 2 * spb * npages:]
    kvw = NSA_KV * NSA_HD
    segm = segm_ref[...]
    cos, sin = _rows8(cos_ref[...]), _rows8(sin_ref[...])
    hrow = lax.broadcasted_iota(jnp.int32, (NSA_HEADS, NSA_HEADS * NSA_HD), 0)
    hcol = lax.broadcasted_iota(jnp.int32, (NSA_HEADS, NSA_HEADS * NSA_HD), 1)
    own = (hcol >> 6) == hrow
    ncmp = kc_ref.shape[1]
    kend = lax.broadcasted_iota(jnp.int32, (1, ncmp), 1) * CMP_STRIDE + (CMP_LEN - 1)
    valid_c = kend <= past_len
    gi = lax.broadcasted_iota(jnp.int32, (NSA_HEADS, NSA_HEADS), 0) // NSA_GQ
    gj = lax.broadcasted_iota(jnp.int32, (NSA_HEADS, NSA_HEADS), 1) // NSA_GQ
    same_group = jnp.where(gi == gj, 1.0, 0.0)

    first = []
    for s in range(spb):
        q = _rope(_head_rmsnorm(_rows8(q_ref[s]), qg_ref[...], segm), cos, sin) * (NSA_HD ** -0.5)
        kv = kv_ref[s]
        sk_new = _rope(_head_rmsnorm(_rows8(kv[:, 2 * kvw:3 * kvw]), kg_ref[1:2, :], segm), cos, sin)[0:1]
        wk_new = _rope(_head_rmsnorm(_rows8(kv[:, 4 * kvw:5 * kvw]), kg_ref[2:3, :], segm), cos, sin)[0:1]
        skn_ref[s] = sk_new
        wkn_ref[s] = wk_new
        q16 = jnp.where(own, jnp.concatenate([q, q], axis=0), 0.0).astype(BF16)
        qbd = _dot(q16, fold_ref[...])
        qbd_b = qbd.astype(BF16)
        s_c = jnp.where(valid_c, _dot_nt(qbd_b, kc_ref[s].astype(BF16)), NEG)
        m_c = jnp.max(s_c, axis=-1, keepdims=True)
        p_c = jnp.where(valid_c, jnp.exp(s_c - m_c), 0.0)
        den = jnp.sum(p_c, axis=-1, keepdims=True)
        p_c = p_c / jnp.where(den > 0, den, 1.0)
        o_c = _dot(p_c.astype(BF16), vc_ref[s].astype(BF16))
        first.append((kv, sk_new, wk_new, qbd, qbd_b, o_c, _dot_hi(same_group, p_c)))

    psum = jnp.concatenate([f[6] for f in first] + [jnp.zeros((128 - spb * NSA_HEADS, ncmp), F32)], axis=0)
    imp = lax.dot_general(ov_ref[...], psum, (((1,), (1,)), ((), ())), preferred_element_type=F32,
                          precision=HI)
    nslot = ov_ref.shape[0]
    jb = lax.broadcasted_iota(jnp.int32, (nslot, 1), 0)
    cur = past_len // SEL_BLOCK
    forced = (jb == 0) | (jb == cur) | (jb == cur - 1)
    score = jnp.where(jb * SEL_BLOCK <= past_len, imp + jnp.where(forced, SEL_FORCE, 0.0), NEG)
    bias_all = _topn_bias(score, SEL_TOPN).T

    nwin = wk_ref.shape[2]
    lane = lax.broadcasted_iota(jnp.int32, (kvw, nwin), 1)

    def column(rowvec):
        col = jnp.broadcast_to(rowvec, (128, kvw)).T
        return jnp.concatenate([col] * (nwin // 128), axis=1)

    for s, (kv, sk_new, wk_new, qbd, qbd_b, o_c, _) in enumerate(first):
        sv_new = kv[:, 3 * kvw:4 * kvw]
        wv_new = kv[:, 5 * kvw:6 * kvw]
        bias = bias_all[s * NSA_HEADS:(s + 1) * NSA_HEADS]
        kmask = _dot(bias.astype(BF16), eblk_ref[...])
        pages = slice(s * npages, (s + 1) * npages)
        sk_all = jnp.concatenate([r[...] for r in skp[pages]], axis=1).astype(BF16)
        sv_all = jnp.concatenate([r[...] for r in svp[pages]], axis=1).astype(BF16)
        s_s = _dot(qbd_b, sk_all) + kmask
        s_new = jnp.sum(qbd * sk_new, axis=-1, keepdims=True)
        p_s, p_new = _softmax_with_extra(s_s, s_new)
        o_s = _dot_nt(p_s.astype(BF16), sv_all) + p_new * sv_new
        wk = wk_ref[s]
        wv = wv_ref[s]
        s_w = _dot(qbd_b, wk.astype(BF16))
        s_wn = jnp.sum(qbd * wk_new, axis=-1, keepdims=True)
        p_w, p_wn = _softmax_with_extra(s_w, s_wn)
        o_w = _dot_nt(p_w.astype(BF16), wv.astype(BF16)) + p_wn * wv_new
        g16 = jnp.where((lax.broadcasted_iota(jnp.int32, (NSA_HEADS, 3 * NSA_HEADS), 1) // 3)
                        == lax.broadcasted_iota(jnp.int32, (NSA_HEADS, 3 * NSA_HEADS), 0),
                        jnp.broadcast_to(_sigmoid(gate_ref[s]), (NSA_HEADS, 3 * NSA_HEADS)), 0.0)
        br = lax.broadcasted_iota(jnp.int32, (NSA_HEADS, 3 * NSA_HEADS), 1) % 3
        gsel = lambda r: jnp.sum(jnp.where(br == r, g16, 0.0), axis=-1, keepdims=True)
        o16 = gsel(0) * o_c + gsel(1) * o_s + gsel(2) * o_w
        ox = _dot(o16.astype(BF16), foldt_ref[...])
        o_ref[s] = jnp.sum(jnp.where(own, ox, 0.0), axis=0, keepdims=True).astype(o_ref.dtype)
        wko_ref[s] = jnp.where(lane == nwin - 1, column(wk_new), pltpu.roll(wk, nwin - 1, axis=1))
        wvo_ref[s] = jnp.where(lane == nwin - 1, column(wv_new), pltpu.roll(wv, nwin - 1, axis=1))


def _nsa_sample(q, kv, gates, pool_sk, pool_sv, page_table, kc, vc, win_k, win_v, w):
    bs, npages = page_table.shape
    page = pool_sk.shape[1]
    past_len = npages * page
    kvw = NSA_KV * NSA_HD
    nwin = win_k.shape[1]
    assert nwin == WINDOW and past_len % SEL_BLOCK == 0
    ncmp = kc.shape[1]
    nslot = 64
    assert past_len // SEL_BLOCK + 1 <= nslot
    cos128, sin128 = _rope_tables(jnp.full((1,), past_len, jnp.int32))
    overlap = _overlap_matrix(ncmp, nslot).T
    key_blk = np.arange(past_len) // SEL_BLOCK
    eblk = jnp.asarray((np.arange(nslot)[:, None] == key_blk[None, :]).astype(np.float32) * MASK_BIG, BF16)
    src = np.arange(NSA_HEADS * NSA_HD)
    dst = (src // NSA_HD // NSA_GQ) * NSA_HD + src % NSA_HD
    fold_np = np.zeros((NSA_HEADS * NSA_HD, kvw), np.float32)
    fold_np[src, dst] = 1.0
    fold = jnp.asarray(fold_np, BF16)
    foldt = jnp.asarray(fold_np.T, BF16)
    spb = 2 if bs % 2 == 0 else 1
    r3 = lambda t: t.reshape(bs, 1, -1)
    row = lambda wd: pl.BlockSpec((spb, 1, wd), lambda b, pt: (b, 0, 0))
    full = lambda shape: pl.BlockSpec(shape, lambda b, pt: (0,) * len(shape))
    per_b = lambda r, c: pl.BlockSpec((spb, r, c), lambda b, pt: (b, 0, 0))
    pages = [pl.BlockSpec((None, kvw, page), functools.partial(lambda b, pt, s, j: (pt[b * spb + s, j], 0, 0), s=s, j=j))
             for s in range(spb) for j in range(npages)]
    npages_all = spb * npages
    pk3 = _pool_t(pool_sk)
    pv3 = _pool_t(pool_sv)
    gs = pltpu.PrefetchScalarGridSpec(
        num_scalar_prefetch=1, grid=(bs // spb,),
        in_specs=[row(NSA_HEADS * NSA_HD), row(6 * kvw), row(3 * NSA_HEADS)] + pages + pages
        + [per_b(ncmp, kvw), per_b(ncmp, kvw), per_b(kvw, nwin), per_b(kvw, nwin),
           full((1, NSA_HEADS * NSA_HD)), full((3, kvw)), full((1, 128)), full((1, 128)), full((128, 128)),
           full((nslot, ncmp)), full((nslot, past_len)), full(fold.shape), full(foldt.shape)],
        out_specs=[row(NSA_HEADS * NSA_HD), row(kvw), row(kvw), per_b(kvw, nwin), per_b(kvw, nwin)])
    return pl.pallas_call(
        functools.partial(_nsa_sample_kernel, npages=npages, past_len=past_len, spb=spb),
        grid_spec=gs,
        out_shape=[jax.ShapeDtypeStruct((bs, 1, NSA_HEADS * NSA_HD), BF16),
                   jax.ShapeDtypeStruct((bs, 1, kvw), F32), jax.ShapeDtypeStruct((bs, 1, kvw), F32),
                   jax.ShapeDtypeStruct((bs, kvw, nwin), F32), jax.ShapeDtypeStruct((bs, kvw, nwin), F32)],
        compiler_params=_cparams("parallel"),
        name="nsa_sample",
    )(page_table, r3(q), r3(kv), r3(gates), *([pk3] * npages_all), *([pv3] * npages_all), kc, vc,
      _pool_t(win_k), _pool_t(win_v),
      jnp.tile(w['nsa_q_norm'], NSA_HEADS).reshape(1, -1), jnp.tile(w['nsa_k_norm'], (1, NSA_KV)),
      cos128, sin128, _segment_ones(), overlap, eblk, fold, foldt)


def _mem_sample_kernel(q_ref, mk_ref, mv_ref, g_ref, o_ref):
    spb, mtok = mk_ref.shape[0], mk_ref.shape[1]
    nrow = mtok * MEM_HEADS
    rows = []
    for s in range(spb):
        q = q_ref[s]
        for h in range(MEM_HEADS):
            qh = q[:, h * MEM_HD:(h + 1) * MEM_HD]
            ms = jnp.mean(qh * qh, axis=-1, keepdims=True)
            rows.append(qh * lax.rsqrt(ms + EPS) * g_ref[...] * (MEM_HD ** -0.5))
    q8 = jnp.concatenate(rows, axis=0).astype(BF16)
    mk = jnp.concatenate([mk_ref[s].reshape(nrow, MEM_HD) for s in range(spb)], axis=0).astype(BF16)
    mv = jnp.concatenate([mv_ref[s].reshape(nrow, MEM_HD) for s in range(spb)], axis=0).astype(BF16)
    col = lax.broadcasted_iota(jnp.int32, (spb * MEM_HEADS, spb * nrow), 1)
    own = ((col // nrow) * MEM_HEADS + (col & (MEM_HEADS - 1))
           == lax.broadcasted_iota(jnp.int32, (spb * MEM_HEADS, spb * nrow), 0))
    s = jnp.where(own, _dot_nt(q8, mk), NEG)
    m = jnp.max(s, axis=-1, keepdims=True)
    p = jnp.exp(s - m)
    p = p / jnp.sum(p, axis=-1, keepdims=True)
    o8 = _dot(p.astype(BF16), mv)
    for s in range(spb):
        for h in range(MEM_HEADS):
            r = s * MEM_HEADS + h
            o_ref[s, :, h * MEM_HD:(h + 1) * MEM_HD] = o8[r:r + 1].astype(o_ref.dtype)


def _mem_sample(q_mem, mem_k, mem_v, qnorm):
    bs = q_mem.shape[0]
    width = MEM_HEADS * MEM_HD
    mtok = mem_k.shape[1]
    spb = 4
    assert bs % spb == 0
    row = pl.BlockSpec((spb, 1, width), lambda i: (i, 0, 0))
    mem = pl.BlockSpec((spb, mtok, MEM_HEADS, MEM_HD), lambda i: (i, 0, 0, 0))
    return pl.pallas_call(
        _mem_sample_kernel,
        grid=(bs // spb,),
        in_specs=[row, mem, mem, pl.BlockSpec((1, MEM_HD), lambda i: (0, 0))],
        out_specs=row,
        out_shape=jax.ShapeDtypeStruct((bs, 1, width), BF16),
        compiler_params=_cparams("parallel"),
        name="mem_sample",
    )(q_mem.reshape(bs, 1, width), mem_k, mem_v, qnorm.reshape(1, -1)).reshape(bs, width)


def _split_w_in(w_in):
    offs = np.cumsum((0,) + IN_SIZES)
    piece = lambda i: w_in[:, int(offs[i]):int(offs[i + 1])]
    small = jnp.concatenate([piece(2), piece(5)], axis=1)
    small = jnp.pad(small, ((0, 0), (0, 128 - small.shape[1])))
    return dict(z=piece(0), xbc=piece(1), small=small, q=piece(3), kv=piece(4), qmem=piece(6), gmerge=piece(7))


def _multi_mm_kernel(*refs, transposed):
    n_out = len(transposed)
    x_ref, g_ref = refs[:2]
    w_refs = refs[2:2 + n_out]
    o_refs = refs[2 + n_out:]
    x = x_ref[...]
    ms = jnp.mean(x * x, axis=-1, keepdims=True)
    xn = (x * lax.rsqrt(ms + EPS) * g_ref[...]).astype(BF16)
    for w_ref, o_ref, tr in zip(w_refs, o_refs, transposed):
        if tr:
            o_ref[0] = _dot_nt(w_ref[...], xn).astype(o_ref.dtype)
        else:
            o_ref[...] = _dot(xn, w_ref[...]).astype(o_ref.dtype)


def _multi_matmul(x, g, weights, transposed, seq_len, tm=256):
    m, k = x.shape
    tm = min(tm, m)
    assert m % tm == 0 and (not any(transposed) or seq_len % tm == 0)
    nl = seq_len // tm if any(transposed) else 1
    const = lambda shape: pl.BlockSpec(shape, lambda i: (0, 0), pipeline_mode=pl.Buffered(1))
    out_specs, out_shape = [], []
    for wm, tr in zip(weights, transposed):
        if tr:
            out_specs.append(pl.BlockSpec((1, wm.shape[0], tm), lambda i: (i // nl, 0, i % nl)))
            out_shape.append(jax.ShapeDtypeStruct((m // seq_len, wm.shape[0], seq_len), F32))
        else:
            out_specs.append(pl.BlockSpec((tm, wm.shape[1]), lambda i: (i, 0)))
            out_shape.append(jax.ShapeDtypeStruct((m, wm.shape[1]), F32))
    return pl.pallas_call(
        functools.partial(_multi_mm_kernel, transposed=tuple(transposed)),
        grid=(m // tm,),
        in_specs=[pl.BlockSpec((tm, k), lambda i: (i, 0)), const((1, k))] + [const(wm.shape) for wm in weights],
        out_specs=out_specs,
        out_shape=out_shape,
        compiler_params=_cparams("parallel"),
        name="in_proj",
    )(x, g.reshape(1, k).astype(F32), *weights)


def _in_proj(x, norm_w, wi, feature_major=(), seq_len=1):
    names = list(wi)
    weights = [wi[n].T if n in feature_major else wi[n] for n in names]
    proj = dict(zip(names, _multi_matmul(x, norm_w, weights, [n in feature_major for n in names], seq_len)))
    proj['dt'] = proj['small'][:, :SSD_HEADS]
    proj['gnsa'] = proj['small'][:, SSD_HEADS:SSD_HEADS + 3 * NSA_HEADS]
    return proj


def _prompt_layer(x_prompt, mem_prompt, w, wb, wi):
    b, length, _ = x_prompt.shape
    n = b * length
    kvw = NSA_KV * NSA_HD
    x = x_prompt.reshape(n, D_MODEL)
    proj = _in_proj(x, w['norm_mix'], wi, feature_major=('q', 'kv'), seq_len=length)
    xbc = proj['xbc'].reshape(b, length, SSD_CONV_DIM)
    y_ssd, ssd_state = _ssd_prompt(xbc, proj['z'].reshape(b, length, D_INNER),
                                   proj['dt'].reshape(b, length, SSD_HEADS), w)
    kvt = proj['kv']
    qt, skaug, skf, svt, wkr, wkf, wvt = _nsa_prep(proj['q'], kvt, w)
    kc, vc = _compress_prompt(kvt, w)
    heads = lambda t: t.reshape(b, -1, NSA_KV, NSA_HD).astype(BF16)
    t4 = lambda t: t.reshape(b, NSA_KV, NSA_HD, -1)
    gates_t = proj['gnsa'].reshape(b, length, NSA_KV, 3 * NSA_GQ).transpose(0, 2, 3, 1)
    o_nsa = _nsa_prompt(qt, heads(kc).transpose(0, 2, 1, 3), heads(vc).transpose(0, 2, 3, 1), skaug, t4(svt), wkr,
                        t4(wvt), gates_t)
    mem = mem_prompt.reshape(-1, D_MODEL)
    mkv, = _multi_matmul(mem, w['mem_norm'], [wb['w_mem_kv']], [False], 1)
    mtok = mem_prompt.shape[1]
    mk = mkv[:, :MEM_HEADS * MEM_HD].reshape(b, mtok, MEM_HEADS, MEM_HD)
    mk = mk * lax.rsqrt(jnp.mean(mk * mk, axis=-1, keepdims=True) + EPS) * w['mem_k_norm']
    mv = mkv[:, MEM_HEADS * MEM_HD:].reshape(b, mtok, MEM_HEADS, MEM_HD)
    o_mem = _mem_prompt(proj['qmem'], mk.reshape(b, mtok, -1).astype(BF16), mv.reshape(b, mtok, -1).astype(BF16),
                        w['mem_q_norm'], b, length)
    x1 = _merge(x, y_ssd.reshape(n, D_INNER), o_nsa, o_mem, proj['gmerge'], wb)
    y, u_last = _ffn_prompt(x1, w, wb, b, length)
    from_t = lambda t: t4(t).transpose(0, 3, 1, 2)
    keep = min(WINDOW, length)
    state = (from_t(kvt[:, 0:kvw]), from_t(kvt[:, kvw:2 * kvw]), from_t(skf), from_t(kvt[:, 3 * kvw:4 * kvw]),
             from_t(wkf[:, :, -keep:]), from_t(kvt[:, 5 * kvw:6 * kvw, -keep:]),
             xbc[:, -(SSD_CONV - 1):], ssd_state, u_last[:, -(FFN_CONV - 1):], mk, mv)
    return y.reshape(b, length, D_MODEL), state


def _sample_layer(x_sample, caches, page_table, w, wb, wi):
    (pool_ck, pool_cv, pool_sk, pool_sv, win_k, win_v, conv_prev, ssd_h0, mem_k, mem_v, ffn_prev) = caches
    bs = x_sample.shape[0]
    kvw = NSA_KV * NSA_HD
    x = x_sample.reshape(bs, D_MODEL)
    proj = _in_proj(x, w['norm_mix'], wi)
    y_ssd, ssd_state = _ssd_step(proj['xbc'], conv_prev, proj['z'], proj['dt'], ssd_h0, w)
    kv = proj['kv']
    kc, vc = _compress_sample(pool_ck, pool_cv, page_table, w)
    o_nsa, sk_new, wk_new, win_k_new, win_v_new = _nsa_sample(
        proj['q'], kv, proj['gnsa'], pool_sk, pool_sv, page_table, kc, vc, win_k, win_v, w)
    o_mem = _mem_sample(proj['qmem'], mem_k, mem_v, w['mem_q_norm'])
    x1 = _merge(x, y_ssd, o_nsa.reshape(bs, -1), o_mem, proj['gmerge'], wb)
    y, u = _ffn_sample(x1, ffn_prev, w, wb)
    r4 = lambda t: t.reshape(bs, -1, NSA_KV, NSA_HD)
    from_t = lambda t: t.reshape(bs, NSA_KV, NSA_HD, -1).transpose(0, 3, 1, 2)
    state = (r4(kv[:, 0:kvw]), r4(kv[:, kvw:2 * kvw]), r4(sk_new), r4(kv[:, 3 * kvw:4 * kvw]),
             from_t(win_k_new), from_t(win_v_new),
             jnp.concatenate([conv_prev[:, 1:], proj['xbc'][:, None]], axis=1), ssd_state,
             jnp.concatenate([ffn_prev[:, 1:], u[:, None]], axis=1))
    return y.reshape(bs, 1, D_MODEL), state


def kernel(x_prompt, x_sample, cache_nsa_cmp_k, cache_nsa_cmp_v, cache_nsa_sel_k, cache_nsa_sel_v, state_nsa_win_k, state_nsa_win_v, state_ssd_conv, state_ssd, cache_mem_k, cache_mem_v, state_ffn_conv, page_table, mem_prompt, norm_mix, w_in, b_merge, ssd_conv_w, ssd_conv_b, ssd_dt_bias, ssd_a_log, ssd_d, ssd_norm, w_ssd_o, nsa_q_norm, nsa_k_norm, cmp_k_pe, cmp_k_w1, cmp_k_w2, cmp_v_pe, cmp_v_w1, cmp_v_w2, w_nsa_o, mem_norm, w_mem_kv, mem_q_norm, mem_k_norm, w_mem_o, w_out, norm_ffn, w_up, ffn_conv_w, ffn_conv_b, w_down):
    weights = dict(norm_mix=norm_mix, w_in=w_in, b_merge=b_merge, ssd_conv_w=ssd_conv_w,
                   ssd_conv_b=ssd_conv_b, ssd_dt_bias=ssd_dt_bias, ssd_a_log=ssd_a_log, ssd_d=ssd_d,
                   ssd_norm=ssd_norm, w_ssd_o=w_ssd_o, nsa_q_norm=nsa_q_norm, nsa_k_norm=nsa_k_norm,
                   cmp_k_pe=cmp_k_pe, cmp_k_w1=cmp_k_w1, cmp_k_w2=cmp_k_w2, cmp_v_pe=cmp_v_pe,
                   cmp_v_w1=cmp_v_w1, cmp_v_w2=cmp_v_w2, w_nsa_o=w_nsa_o, mem_norm=mem_norm,
                   w_mem_kv=w_mem_kv, mem_q_norm=mem_q_norm, mem_k_norm=mem_k_norm, w_mem_o=w_mem_o,
                   w_out=w_out, norm_ffn=norm_ffn, w_up=w_up, ffn_conv_w=ffn_conv_w,
                   ffn_conv_b=ffn_conv_b, w_down=w_down)
    w = {name: arr[0] for name, arr in weights.items()}
    wb = dict(w)
    for name in ('w_ssd_o', 'w_nsa_o', 'w_mem_o', 'w_out', 'w_up', 'w_down', 'w_mem_kv'):
        wb[name] = w[name].astype(BF16)
    wi = {name: piece.astype(BF16) for name, piece in _split_w_in(w['w_in']).items()}
    y_p, st_p = _prompt_layer(x_prompt, mem_prompt, w, wb, wi)
    caches = (cache_nsa_cmp_k[0], cache_nsa_cmp_v[0], cache_nsa_sel_k[0], cache_nsa_sel_v[0], state_nsa_win_k[0],
              state_nsa_win_v[0], state_ssd_conv[0], state_ssd[0], cache_mem_k[0], cache_mem_v[0], state_ffn_conv[0])
    y_s, st_s = _sample_layer(x_sample, caches, page_table, w, wb, wi)
    return (y_p, y_s) + tuple(s[None] for s in st_p) + tuple(s[None] for s in st_s)
```
